```python
import jax, jax.numpy as jnp
from jax import lax
import numpy as np

D_MODEL = 2048
BATCH = 4
SEQ = 4096
DEPTH = 2

MEM_LEN = 256
HEAD_DIM = 128
MIX_W = D_MODEL
MEM_HEADS = 4
MEM_W = MEM_HEADS * HEAD_DIM
POOL_W = MIX_W - MEM_W
POOL_WINDOWS = (2, 4, 8, 16)
POOL_GROUPS = len(POOL_WINDOWS)
POOL_GC = POOL_W // POOL_GROUPS
NSA_W = MIX_W - MEM_W
NSA_HEADS = NSA_W // HEAD_DIM
NSA_KV_HEADS = 4
NSA_GROUP = NSA_HEADS // NSA_KV_HEADS
NSA_KV_W = NSA_KV_HEADS * HEAD_DIM
CMP_LEN = 32
CMP_STRIDE = 16
CMP_HID = 256
SEL_LEN = 64
SEL_TOPK = 16
WINDOW = 512
Q_CHUNK = 32
ROT_DIM = HEAD_DIM // 4
ROPE_THETA = 500000.0
NORM_EPS = 1e-6
FORCE_SCORE = 1e4
NEG_INF = -1e30

kernel_name = "yoco_pool_nsa_memory_hybrid"


def rmsnorm(x, g):
    xf = x.astype(jnp.float32)
    y = xf * lax.rsqrt(jnp.mean(xf * xf, axis=-1, keepdims=True) + NORM_EPS)
    return (y * g.astype(jnp.float32)).astype(x.dtype)


def rope(x, pos):
    half = ROT_DIM // 2
    inv = ROPE_THETA ** (-jnp.arange(half, dtype=jnp.float32) * 2.0 / ROT_DIM)
    ang = pos.astype(jnp.float32)[..., None] * inv
    cos = jnp.cos(ang)[:, :, None, :]
    sin = jnp.sin(ang)[:, :, None, :]
    xr = x[..., :ROT_DIM].astype(jnp.float32)
    x1, x2 = xr[..., :half], xr[..., half:]
    rot = jnp.concatenate([x1 * cos - x2 * sin, x2 * cos + x1 * sin], axis=-1)
    return jnp.concatenate([rot.astype(x.dtype), x[..., ROT_DIM:]], axis=-1)


def masked_softmax(s, mask):
    s = jnp.where(mask, s.astype(jnp.float32), NEG_INF)
    p = jax.nn.softmax(s, axis=-1)
    return jnp.where(mask, p, 0.0)


def memory_attention(q, mem, mem_g, w_mem_kv):
    b, m, _ = mem.shape
    mn = rmsnorm(mem, mem_g)
    kv = (mn @ w_mem_kv).reshape(b, m, 2, MEM_HEADS, HEAD_DIM)
    k, v = kv[:, :, 0], kv[:, :, 1]
    s = jnp.einsum('bshd,bmhd->bhsm', q, k).astype(jnp.float32)
    p = jax.nn.softmax(s, axis=-1).astype(v.dtype)
    o = jnp.einsum('bhsm,bmhd->bshd', p, v)
    return o.reshape(q.shape[0], q.shape[1], MEM_W)


def multiscale_pool(u, w_pool, scale):
    b, s, _ = u.shape
    ug = u.reshape(b, s, POOL_GROUPS, POOL_GC)
    t1 = jnp.arange(1, s + 1, dtype=jnp.float32)[None, :, None]
    outs = []
    for gi, win in enumerate(POOL_WINDOWS):
        v = ug[:, :, gi].astype(jnp.float32)
        c = jnp.cumsum(v, axis=1)
        c_lag = jnp.pad(c, ((0, 0), (win, 0), (0, 0)))[:, :s]
        outs.append((c - c_lag) / jnp.minimum(t1, float(win)) - v)
    pooled = jnp.stack(outs, axis=2).astype(u.dtype)
    mixed = jnp.einsum('bsgc,gcd->bsgd', pooled, w_pool).reshape(b, s, POOL_W)
    return mixed * scale


def nsa_shared_kv(h, kv_norm_g, w_kv, cmp_pe, cmp_w1, cmp_w2, positions):
    b, s, _ = h.shape
    hn = rmsnorm(h, kv_norm_g)
    kv = (hn @ w_kv).reshape(b, s, 6, NSA_KV_HEADS, HEAD_DIM)
    k_c, v_c, k_s, v_s, k_w, v_w = [kv[:, :, i] for i in range(6)]
    k_s = rope(k_s, positions)
    k_w = rope(k_w, positions)
    n_cmp = (s - CMP_LEN) // CMP_STRIDE + 1
    blk = jnp.arange(n_cmp)[:, None] * CMP_STRIDE + jnp.arange(CMP_LEN)[None, :]

    def compress(z, pe, w1, w2):
        zb = z[:, blk] + pe[None, None, :, None, :]
        zb = zb.transpose(0, 1, 3, 2, 4).reshape(b, n_cmp, NSA_KV_HEADS, CMP_LEN * HEAD_DIM)
        return jax.nn.silu(zb @ w1) @ w2

    k_cmp = compress(k_c, cmp_pe[0], cmp_w1[0], cmp_w2[0])
    v_cmp = compress(v_c, cmp_pe[1], cmp_w1[1], cmp_w2[1])
    k_cmp = rope(k_cmp, positions[:, blk[:, -1]])
    n_sel = s // SEL_LEN
    k_blk = k_s.reshape(b, n_sel, SEL_LEN, NSA_KV_HEADS, HEAD_DIM).transpose(0, 3, 1, 2, 4)
    v_blk = v_s.reshape(b, n_sel, SEL_LEN, NSA_KV_HEADS, HEAD_DIM).transpose(0, 3, 1, 2, 4)
    pad = ((0, 0), (WINDOW, 0), (0, 0), (0, 0))
    return (k_cmp, v_cmp, k_blk, v_blk, jnp.pad(k_w, pad), jnp.pad(v_w, pad))


def nsa_attention(q, gates, k_cmp, v_cmp, k_blk, v_blk, k_win, v_win):
    b, s, _, _ = q.shape
    n_cmp = k_cmp.shape[1]
    n_sel = k_blk.shape[2]
    n_top = min(SEL_TOPK, n_sel)
    G, R, L = NSA_KV_HEADS, NSA_GROUP, SEL_LEN
    cmp_start = jnp.arange(n_cmp) * CMP_STRIDE
    cmp_end = cmp_start + CMP_LEN - 1
    sel_start = jnp.arange(n_sel) * SEL_LEN
    ov = (jnp.minimum(cmp_end[:, None], sel_start[None, :] + L - 1)
          - jnp.maximum(cmp_start[:, None], sel_start[None, :]) + 1)
    overlap = jnp.maximum(ov, 0).astype(jnp.float32) / CMP_LEN
    bi = jnp.arange(b)[:, None, None, None]
    gi = jnp.arange(G)[None, :, None, None]
    jsel = jnp.arange(n_sel)

    def chunk(c):
        t0 = c * Q_CHUNK
        t = t0 + jnp.arange(Q_CHUNK)
        qc = lax.dynamic_slice_in_dim(q, t0, Q_CHUNK, axis=1).reshape(b, Q_CHUNK, G, R, HEAD_DIM)
        gc = lax.dynamic_slice_in_dim(gates, t0, Q_CHUNK, axis=1).reshape(b, Q_CHUNK, G, R, 3)
        sc = jnp.einsum('bqgrd,bngd->bgrqn', qc, k_cmp)
        p_cmp = masked_softmax(sc, cmp_end[None, :] <= t[:, None])
        o_cmp = jnp.einsum('bgrqn,bngd->bqgrd', p_cmp.astype(v_cmp.dtype), v_cmp)
        imp = jnp.einsum('bgrqn,nj->bgqj', p_cmp, overlap)
        cur = t // L
        forced = (jsel[None, :] == 0) | (jsel[None, :] == cur[:, None]) | (jsel[None, :] == cur[:, None] - 1)
        ok = sel_start[None, :] <= t[:, None]
        score = jnp.where(ok, jnp.where(forced, FORCE_SCORE, imp), NEG_INF)
        _, idx = lax.top_k(score, n_top)
        k_sel = k_blk[bi, gi, idx].reshape(b, G, Q_CHUNK, n_top * L, HEAD_DIM)
        v_sel = v_blk[bi, gi, idx].reshape(b, G, Q_CHUNK, n_top * L, HEAD_DIM)
        key_pos = (idx[..., None] * L + jnp.arange(L)).reshape(b, G, 1, Q_CHUNK, n_top * L)
        ss = jnp.einsum('bqgrd,bgqmd->bgrqm', qc, k_sel)
        p_sel = masked_softmax(ss, key_pos <= t[None, None, None, :, None])
        o_sel = jnp.einsum('bgrqm,bgqmd->bqgrd', p_sel.astype(v_sel.dtype), v_sel)
        kw = lax.dynamic_slice_in_dim(k_win, t0, WINDOW + Q_CHUNK, axis=1)
        vw = lax.dynamic_slice_in_dim(v_win, t0, WINDOW + Q_CHUNK, axis=1)
        spos = t0 - WINDOW + jnp.arange(WINDOW + Q_CHUNK)
        dist = t[:, None] - spos[None, :]
        win_ok = (dist >= 0) & (dist < WINDOW) & (spos[None, :] >= 0)
        sw = jnp.einsum('bqgrd,bsgd->bgrqs', qc, kw)
        p_w = masked_softmax(sw, win_ok)
        o_win = jnp.einsum('bgrqs,bsgd->bqgrd', p_w.astype(vw.dtype), vw)
        return gc[..., 0:1] * o_cmp + gc[..., 1:2] * o_sel + gc[..., 2:3] * o_win

    out = lax.map(chunk, jnp.arange(s // Q_CHUNK))
    return jnp.moveaxis(out, 0, 1).reshape(b, s, NSA_W)


def pool_layer(x, mem, norm_g, w_in, w_pool, pool_scale, mem_g, w_mem_kv, w_out):
    b, s, _ = x.shape
    h = rmsnorm(x, norm_g)
    proj = h @ w_in
    u, zu, qm, zm = jnp.split(proj, [POOL_W, 2 * POOL_W, 2 * POOL_W + MEM_W], axis=-1)
    y_pool = multiscale_pool(u, w_pool, pool_scale) * jax.nn.silu(zu)
    qm = qm.reshape(b, s, MEM_HEADS, HEAD_DIM) * (HEAD_DIM ** -0.5)
    y_mem = memory_attention(qm, mem, mem_g, w_mem_kv) * jax.nn.silu(zm)
    return x + jnp.concatenate([y_pool, y_mem], axis=-1) @ w_out


def nsa_layer(x, mem, positions, norm_g, w_in, mem_g, w_mem_kv, w_out, shared):
    b, s, _ = x.shape
    h = rmsnorm(x, norm_g)
    proj = h @ w_in
    o1 = NSA_W
    o2 = o1 + 3 * NSA_HEADS
    o3 = o2 + NSA_W
    o4 = o3 + MEM_W
    q, gl, zq, qm, zm = jnp.split(proj, [o1, o2, o3, o4], axis=-1)
    q = rope(q.reshape(b, s, NSA_HEADS, HEAD_DIM), positions) * (HEAD_DIM ** -0.5)
    gates = jax.nn.sigmoid(gl.reshape(b, s, NSA_HEADS, 3))
    y_nsa = nsa_attention(q, gates, *shared) * jax.nn.silu(zq)
    qm = qm.reshape(b, s, MEM_HEADS, HEAD_DIM) * (HEAD_DIM ** -0.5)
    y_mem = memory_attention(qm, mem, mem_g, w_mem_kv) * jax.nn.silu(zm)
    return x + jnp.concatenate([y_nsa, y_mem], axis=-1) @ w_out


def setup_inputs(seed: int = 0) -> dict:
    key = jax.random.key(seed)
    ks = jax.random.split(key, 20)
    n_a = DEPTH // 2
    n_b = DEPTH - n_a
    f32 = jnp.float32

    def w(k, shape, fan_in):
        return jax.random.normal(k, shape, f32) * (fan_in ** -0.5)

    def gain(k, shape):
        return 1.0 + 0.05 * jax.random.normal(k, shape, f32)

    x = jax.random.normal(ks[0], (BATCH, SEQ, D_MODEL), f32)
    mem = jax.random.normal(ks[1], (BATCH, MEM_LEN, D_MODEL), f32)
    offs = jax.random.randint(ks[2], (BATCH, 1), 0, 1024, dtype=jnp.int32)
    positions = (offs + jnp.arange(SEQ, dtype=jnp.int32)[None, :]).astype(jnp.int32)
    b_in_w = 2 * NSA_W + 3 * NSA_HEADS + 2 * MEM_W
    return {
        "x": x,
        "mem": mem,
        "positions": positions,
        "norm_g": gain(ks[3], (DEPTH, D_MODEL)),
        "mem_norm_g": gain(ks[4], (DEPTH, D_MODEL)),
        "w_mem_kv": w(ks[5], (DEPTH, D_MODEL, 2 * MEM_W), D_MODEL),
        "w_out": w(ks[6], (DEPTH, MIX_W, D_MODEL), MIX_W),
        "a_w_in": w(ks[7], (n_a, D_MODEL, 2 * POOL_W + 2 * MEM_W), D_MODEL),
        "a_w_pool": w(ks[8], (n_a, POOL_GROUPS, POOL_GC, POOL_GC), POOL_GC),
        "a_pool_scale": gain(ks[9], (n_a, POOL_W)),
        "b_w_in": w(ks[10], (n_b, D_MODEL, b_in_w), D_MODEL),
        "kv_norm_g": gain(ks[11], (D_MODEL,)),
        "w_kv": w(ks[12], (D_MODEL, 6 * NSA_KV_W), D_MODEL),
        "cmp_pe": 0.1 * jax.random.normal(ks[13], (2, CMP_LEN, HEAD_DIM), f32),
        "cmp_w1": w(ks[14], (2, CMP_LEN * HEAD_DIM, CMP_HID), CMP_LEN * HEAD_DIM),
        "cmp_w2": w(ks[15], (2, CMP_HID, HEAD_DIM), CMP_HID),
        "final_g": gain(ks[16], (D_MODEL,)),
    }


def reference(x, mem, positions, norm_g, mem_norm_g, w_mem_kv, w_out, a_w_in, a_w_pool,
              a_pool_scale, b_w_in, kv_norm_g, w_kv, cmp_pe, cmp_w1, cmp_w2, final_g):
    n_a = DEPTH // 2
    h = x
    shared = None
    for layer in range(DEPTH):
        if layer < n_a:
            h = pool_layer(h, mem, norm_g[layer], a_w_in[layer], a_w_pool[layer],
                           a_pool_scale[layer], mem_norm_g[layer], w_mem_kv[layer], w_out[layer])
        else:
            if layer == n_a:
                shared = nsa_shared_kv(h, kv_norm_g, w_kv, cmp_pe, cmp_w1, cmp_w2, positions)
            h = nsa_layer(h, mem, positions, norm_g[layer], b_w_in[layer - n_a],
                          mem_norm_g[layer], w_mem_kv[layer], w_out[layer], shared)
    return rmsnorm(h, final_g)
```

```python
import functools

import jax
import jax.numpy as jnp
from jax import lax
from jax.experimental import pallas as pl
from jax.experimental.pallas import tpu as pltpu

D_MODEL = 2048
MEM_LEN = 256
HEAD_DIM = 128
MEM_HEADS = 4
MEM_W = MEM_HEADS * HEAD_DIM
POOL_W = D_MODEL - MEM_W
POOL_WINDOWS = (2, 4, 8, 16)
POOL_GC = POOL_W // len(POOL_WINDOWS)
POOL_HALO = 16
NSA_W = D_MODEL - MEM_W
NSA_HEADS = NSA_W // HEAD_DIM
NSA_KV_HEADS = 4
NSA_GROUP = NSA_HEADS // NSA_KV_HEADS
NSA_KV_W = NSA_KV_HEADS * HEAD_DIM
CMP_LEN = 32
CMP_STRIDE = 16
CMP_HID = 256
SEL_LEN = 64
SEL_TOPK = 16
WINDOW = 512
ROT_DIM = HEAD_DIM // 4
ROT_HALF = ROT_DIM // 2
ROPE_THETA = 500000.0
NORM_EPS = 1e-6
FORCE_SCORE = 1e4
NEG_INF = -1e30
SEL_BIAS = -1e9
QK_SCALE = HEAD_DIM ** -0.5

LANES = 128
TOK_TILE = 256
Q_TILE = 256
COL_CHUNK = 512
GATE_PAD = LANES
VMEM_LIMIT = 56 * 1024 * 1024

F32 = jnp.float32
BF16 = jnp.bfloat16


def _resident(shape):
    nd = len(shape)
    return pl.BlockSpec(shape, lambda *_: (0,) * nd, pipeline_mode=pl.Buffered(1))


def _rms_scale(x, g):
    ms = jnp.mean(x * x, axis=-1, keepdims=True)
    return x * lax.rsqrt(ms + NORM_EPS) * g


def _silu(z):
    return z * (1.0 / (1.0 + jnp.exp(-z)))


def _dot(a, b):
    return jnp.dot(a, b, preferred_element_type=F32)


def _dot_nt(a, b):
    return lax.dot_general(a, b, (((1,), (1,)), ((), ())), preferred_element_type=F32)


def _memory_attention(qm, zm, k_ref, v_ref, y_ref, col0):
    for h in range(MEM_HEADS):
        cs = slice(h * HEAD_DIM, (h + 1) * HEAD_DIM)
        q = (qm[:, cs] * QK_SCALE).astype(BF16)
        s = _dot_nt(q, k_ref[:, cs])
        e = jnp.exp(s - jnp.max(s, axis=-1, keepdims=True))
        p = e / jnp.sum(e, axis=-1, keepdims=True)
        o = _dot(p.astype(BF16), v_ref[:, cs])
        y_ref[:, col0 + h * HEAD_DIM:col0 + (h + 1) * HEAD_DIM] = (o * _silu(zm[:, cs])).astype(y_ref.dtype)


def _mem_kv_kernel(mem_ref, g_ref, w_ref, o_ref):
    hn = _rms_scale(mem_ref[...], g_ref[...]).astype(BF16)
    o_ref[...] = _dot(hn, w_ref[...]).astype(o_ref.dtype)


def _mem_kv(mem2d, mem_norm_g, w_mem_kv_bf16):
    depth = w_mem_kv_bf16.shape[0]
    rows = mem2d.shape[0]
    T = TOK_TILE
    return pl.pallas_call(
        _mem_kv_kernel,
        grid=(depth, rows // T),
        in_specs=[
            pl.BlockSpec((T, D_MODEL), lambda l, i: (i, 0)),
            pl.BlockSpec((None, 1, D_MODEL), lambda l, i: (l, 0, 0)),
            pl.BlockSpec((None, D_MODEL, 2 * MEM_W), lambda l, i: (l, 0, 0)),
        ],
        out_specs=pl.BlockSpec((None, T, 2 * MEM_W), lambda l, i: (l, i, 0)),
        out_shape=jax.ShapeDtypeStruct((depth, rows, 2 * MEM_W), BF16),
        compiler_params=pltpu.CompilerParams(dimension_semantics=("arbitrary", "arbitrary"),
                                             vmem_limit_bytes=VMEM_LIMIT),
        name="mem_kv",
    )(mem2d, mem_norm_g.reshape(depth, 1, D_MODEL), w_mem_kv_bf16)


def _layer0_kernel(tiles_per_seq, x_ref, g_ref, win_ref, wpool_ref, scale_ref, mk_ref, mv_ref, wout_ref,
                   o_ref, hn_ref, proj_ref, uext_ref, y_ref):
    T = x_ref.shape[0]
    tb = pl.program_id(0) % tiles_per_seq
    x = x_ref[...]
    hn_ref[...] = _rms_scale(x, g_ref[...]).astype(BF16)

    @pl.when(tb == 0)
    def _():
        uext_ref[0:POOL_HALO, :] = jnp.zeros((POOL_HALO, POOL_W), F32)

    @pl.when(tb != 0)
    def _():
        uext_ref[0:POOL_HALO, :] = uext_ref[T:T + POOL_HALO, :]

    n_in = win_ref.shape[1]
    for c in range(0, n_in, COL_CHUNK):
        blk = _dot(hn_ref[...], win_ref[:, c:c + COL_CHUNK])
        if c < POOL_W:
            uext_ref[POOL_HALO:POOL_HALO + T, c:c + COL_CHUNK] = blk
        else:
            proj_ref[:, c - POOL_W:c - POOL_W + COL_CHUNK] = blk

    t1 = (tb * T + lax.broadcasted_iota(jnp.int32, (T, 1), 0) + 1).astype(F32)
    for gi, win in enumerate(POOL_WINDOWS):
        cs = slice(gi * POOL_GC, (gi + 1) * POOL_GC)
        u = uext_ref[POOL_HALO:POOL_HALO + T, cs]
        acc = u
        for k in range(1, win):
            acc = acc + uext_ref[POOL_HALO - k:POOL_HALO - k + T, cs]
        pooled = acc / jnp.minimum(t1, float(win)) - u
        mixed = _dot(pooled.astype(BF16), wpool_ref[gi])
        y_ref[:, cs] = (mixed * scale_ref[:, cs] * _silu(proj_ref[:, cs])).astype(BF16)

    qm = proj_ref[:, POOL_W:POOL_W + MEM_W]
    zm = proj_ref[:, POOL_W + MEM_W:POOL_W + 2 * MEM_W]
    _memory_attention(qm, zm, mk_ref, mv_ref, y_ref, POOL_W)

    for c in range(0, D_MODEL, COL_CHUNK):
        o_ref[:, c:c + COL_CHUNK] = x_ref[:, c:c + COL_CHUNK] + _dot(y_ref[...], wout_ref[:, c:c + COL_CHUNK])


def _layer0(x2d, seq, norm_g, w_in, w_pool, pool_scale, mem_k, mem_v, w_out):
    n = x2d.shape[0]
    T = TOK_TILE
    tiles_per_seq = seq // T
    n_in = w_in.shape[1]
    return pl.pallas_call(
        functools.partial(_layer0_kernel, tiles_per_seq),
        grid=(n // T,),
        in_specs=[
            pl.BlockSpec((T, D_MODEL), lambda i: (i, 0)),
            _resident((1, D_MODEL)),
            _resident((D_MODEL, n_in)),
            _resident(w_pool.shape),
            _resident((1, POOL_W)),
            pl.BlockSpec((None, MEM_LEN, MEM_W), lambda i: (i // tiles_per_seq, 0, 0)),
            pl.BlockSpec((None, MEM_LEN, MEM_W), lambda i: (i // tiles_per_seq, 0, 0)),
            _resident((D_MODEL, D_MODEL)),
        ],
        out_specs=pl.BlockSpec((T, D_MODEL), lambda i: (i, 0)),
        out_shape=jax.ShapeDtypeStruct((n, D_MODEL), F32),
        scratch_shapes=[
            pltpu.VMEM((T, D_MODEL), BF16),
            pltpu.VMEM((T, n_in - POOL_W), F32),
            pltpu.VMEM((T + POOL_HALO, POOL_W), F32),
            pltpu.VMEM((T, D_MODEL), BF16),
        ],
        compiler_params=pltpu.CompilerParams(dimension_semantics=("arbitrary",), vmem_limit_bytes=VMEM_LIMIT),
        name="layer0",
    )(x2d, norm_g.reshape(1, D_MODEL), w_in, w_pool, pool_scale.reshape(1, POOL_W), mem_k, mem_v, w_out)


def _proj1_kernel(h_ref, gkv_ref, g1_ref, wkv_ref, w1_ref, kv_ref, p1_ref, hn_ref):
    h = h_ref[...]
    ms = jnp.mean(h * h, axis=-1, keepdims=True)
    hs = h * lax.rsqrt(ms + NORM_EPS)
    hn_ref[...] = (hs * gkv_ref[...]).astype(BF16)
    n_kv = wkv_ref.shape[1]
    for c in range(0, n_kv, COL_CHUNK):
        kv_ref[:, c:c + COL_CHUNK] = _dot(hn_ref[...], wkv_ref[:, c:c + COL_CHUNK])
    hn_ref[...] = (hs * g1_ref[...]).astype(BF16)
    n_1 = w1_ref.shape[1]
    for c in range(0, n_1, COL_CHUNK):
        w = min(COL_CHUNK, n_1 - c)
        p1_ref[:, c:c + w] = _dot(hn_ref[...], w1_ref[:, c:c + w])


def _proj1(h2d, kv_norm_g, norm_g1, w_kv, w_1):
    n = h2d.shape[0]
    T = TOK_TILE
    n_kv, n_1 = w_kv.shape[1], w_1.shape[1]
    return pl.pallas_call(
        _proj1_kernel,
        grid=(n // T,),
        in_specs=[
            pl.BlockSpec((T, D_MODEL), lambda i: (i, 0)),
            _resident((1, D_MODEL)),
            _resident((1, D_MODEL)),
            _resident((D_MODEL, n_kv)),
            _resident((D_MODEL, n_1)),
        ],
        out_specs=[
            pl.BlockSpec((T, n_kv), lambda i: (i, 0)),
            pl.BlockSpec((T, n_1), lambda i: (i, 0)),
        ],
        out_shape=[
            jax.ShapeDtypeStruct((n, n_kv), F32),
            jax.ShapeDtypeStruct((n, n_1), F32),
        ],
        scratch_shapes=[pltpu.VMEM((T, D_MODEL), BF16)],
        compiler_params=pltpu.CompilerParams(dimension_semantics=("arbitrary",), vmem_limit_bytes=VMEM_LIMIT),
        name="proj1",
    )(h2d, kv_norm_g.reshape(1, D_MODEL), norm_g1.reshape(1, D_MODEL), w_kv, w_1)


def _rope(x, cos_t, sin_t):
    lane = lax.broadcasted_iota(jnp.int32, x.shape, 1)
    swapped = jnp.where(lane < ROT_HALF, pltpu.roll(x, LANES - ROT_HALF, axis=1), pltpu.roll(x, ROT_HALF, axis=1))
    return x * cos_t + swapped * sin_t


KV_ROW_TILE = 512


def _kv_rot_kernel(seq, ksv_ref, kwv_ref, pos_ref, inv_ref, sgn_ref, ksx_ref, vso_ref, kwo_ref, vwo_ref, cos_ref,
                   sin_ref):
    rows = ksv_ref.shape[0]
    ang = pos_ref[...].astype(F32) * inv_ref[...]
    c = jnp.cos(ang)
    s = jnp.sin(ang) * sgn_ref[...]
    cos_ref[...] = c
    sin_ref[...] = s
    r0 = (pl.program_id(0) * rows) % seq
    blk = (r0 + lax.broadcasted_iota(jnp.int32, (rows, LANES), 0)) // SEL_LEN
    lane = lax.broadcasted_iota(jnp.int32, (rows, LANES), 1)
    ind = jnp.where(blk == lane, 1.0, 0.0).astype(BF16)
    for h in range(NSA_KV_HEADS):
        hs = slice(h * HEAD_DIM, (h + 1) * HEAD_DIM)
        vsl = slice(NSA_KV_W + h * HEAD_DIM, NSA_KV_W + (h + 1) * HEAD_DIM)
        ksx_ref[:, 2 * h * HEAD_DIM:(2 * h + 1) * HEAD_DIM] = _rope(ksv_ref[:, hs], c, s).astype(BF16)
        ksx_ref[:, (2 * h + 1) * HEAD_DIM:(2 * h + 2) * HEAD_DIM] = ind
        kwo_ref[:, hs] = _rope(kwv_ref[:, hs], c, s).astype(BF16)
        vso_ref[:, hs] = ksv_ref[:, vsl].astype(BF16)
        vwo_ref[:, hs] = kwv_ref[:, vsl].astype(BF16)


def _kv_rot(kvp, seq, pos_col, inv_full, sgn_full):
    n = kvp.shape[0]
    T = KV_ROW_TILE
    pair_w = 2 * NSA_KV_W
    row = lambda w: pl.BlockSpec((T, w), lambda i: (i, 0))
    return pl.pallas_call(
        functools.partial(_kv_rot_kernel, seq),
        grid=(n // T,),
        in_specs=[
            pl.BlockSpec((T, pair_w), lambda i: (i, 1)),
            pl.BlockSpec((T, pair_w), lambda i: (i, 2)),
            row(1),
            pl.BlockSpec((1, LANES), lambda i: (0, 0)),
            pl.BlockSpec((1, LANES), lambda i: (0, 0)),
        ],
        out_specs=[row(2 * NSA_KV_W), row(NSA_KV_W), row(NSA_KV_W), row(NSA_KV_W), row(LANES), row(LANES)],
        out_shape=[
            jax.ShapeDtypeStruct((n, 2 * NSA_KV_W), BF16),
            jax.ShapeDtypeStruct((n, NSA_KV_W), BF16),
            jax.ShapeDtypeStruct((n, NSA_KV_W), BF16),
            jax.ShapeDtypeStruct((n, NSA_KV_W), BF16),
            jax.ShapeDtypeStruct((n, LANES), F32),
            jax.ShapeDtypeStruct((n, LANES), F32),
        ],
        compiler_params=pltpu.CompilerParams(dimension_semantics=("arbitrary",), vmem_limit_bytes=VMEM_LIMIT),
        name="kv_rot",
    )(kvp, kvp, pos_col, inv_full, sgn_full)


def _compress_kernel(kc_ref, vc_ref, cos_ref, sin_ref, pe_ref, w1_ref, w2_ref, kcmp_ref, vcmp_ref):
    S = kc_ref.shape[0]
    n_half = S // CMP_STRIDE
    assert CMP_LEN == 2 * CMP_STRIDE
    for part, (src_ref, dst_ref) in enumerate(((kc_ref, kcmp_ref), (vc_ref, vcmp_ref))):
        acc_first = jnp.zeros((n_half, CMP_HID), F32)
        acc_second = jnp.zeros((n_half, CMP_HID), F32)
        for l in range(CMP_STRIDE):
            z = src_ref[pl.ds(l, n_half, stride=CMP_STRIDE), :]
            z1 = (z + pe_ref[part, l:l + 1, :]).astype(BF16)
            acc_first = acc_first + _dot(z1, w1_ref[part, l * HEAD_DIM:(l + 1) * HEAD_DIM, :])
            l2 = l + CMP_STRIDE
            z2 = (z + pe_ref[part, l2:l2 + 1, :]).astype(BF16)
            acc_second = acc_second + _dot(z2, w1_ref[part, l2 * HEAD_DIM:(l2 + 1) * HEAD_DIM, :])
        pre = acc_first + pltpu.roll(acc_second, n_half - 1, axis=0)
        out = _dot(_silu(pre).astype(BF16), w2_ref[part])
        if part == 0:
            c_end = pltpu.roll(cos_ref[pl.ds(CMP_STRIDE - 1, n_half, stride=CMP_STRIDE), :], n_half - 1, axis=0)
            s_end = pltpu.roll(sin_ref[pl.ds(CMP_STRIDE - 1, n_half, stride=CMP_STRIDE), :], n_half - 1, axis=0)
            out = _rope(out, c_end, s_end)
        dst_ref[...] = out.astype(BF16)


def _compress(kvp, batch, seq, cos_tab, sin_tab, cmp_pe, cmp_w1, cmp_w2):
    n_half = seq // CMP_STRIDE
    cmp_out = pl.BlockSpec((None, None, n_half, HEAD_DIM), lambda b, g: (b, g, 0, 0))
    tab = pl.BlockSpec((seq, LANES), lambda b, g: (b, 0))
    return pl.pallas_call(
        _compress_kernel,
        grid=(batch, NSA_KV_HEADS),
        in_specs=[
            pl.BlockSpec((seq, HEAD_DIM), lambda b, g: (b, g)),
            pl.BlockSpec((seq, HEAD_DIM), lambda b, g: (b, NSA_KV_HEADS + g)),
            tab, tab,
            _resident(cmp_pe.shape), _resident(cmp_w1.shape), _resident(cmp_w2.shape),
        ],
        out_specs=[cmp_out, cmp_out],
        out_shape=[
            jax.ShapeDtypeStruct((batch, NSA_KV_HEADS, n_half, HEAD_DIM), BF16),
            jax.ShapeDtypeStruct((batch, NSA_KV_HEADS, n_half, HEAD_DIM), BF16),
        ],
        compiler_params=pltpu.CompilerParams(dimension_semantics=("arbitrary", "arbitrary"),
                                             vmem_limit_bytes=VMEM_LIMIT),
        name="compress",
    )(kvp, kvp, cos_tab, sin_tab, cmp_pe, cmp_w1, cmp_w2)


def _flash_first(s, v, m_ref, l_ref, acc_ref):
    m = jnp.max(s, axis=1, keepdims=True)
    p = jnp.exp(s - m)
    m_ref[...] = m
    l_ref[...] = jnp.sum(p, axis=1, keepdims=True)
    acc_ref[...] = _dot(p.astype(BF16), v)


def _flash_next(s, v, m_ref, l_ref, acc_ref):
    m_prev = m_ref[...]
    m_new = jnp.maximum(m_prev, jnp.max(s, axis=1, keepdims=True))
    alpha = jnp.exp(m_prev - m_new)
    p = jnp.exp(s - m_new)
    m_ref[...] = m_new
    l_ref[...] = alpha * l_ref[...] + jnp.sum(p, axis=1, keepdims=True)
    acc_ref[...] = alpha * acc_ref[...] + _dot(p.astype(BF16), v)


def _nsa_kernel(q_ref, zq_ref, gl_ref, cos_ref, sin_ref, ksx_ref, vs_ref, kw_ref, vw_ref, kcmp_ref, vcmp_ref,
                o_ref, qx_ref, sc_ref, m_ref, l_ref, acc_ref, ocmp_ref, osel_ref):
    TQ = q_ref.shape[0]
    R = NSA_GROUP
    M = R * TQ
    n_cmp_pad = kcmp_ref.shape[0]
    n_sel = ksx_ref.shape[0] // SEL_LEN
    g = pl.program_id(1)
    qi = pl.program_id(2)
    t0 = qi * TQ

    cos_t = cos_ref[...]
    sin_t = sin_ref[...]
    for r in range(R):
        qr = _rope(q_ref[:, r * HEAD_DIM:(r + 1) * HEAD_DIM], cos_t, sin_t) * QK_SCALE
        qx_ref[r * TQ:(r + 1) * TQ, 0:HEAD_DIM] = qr.astype(BF16)

    row_t = t0 + lax.broadcasted_iota(jnp.int32, (M, 1), 0) % TQ
    q_all = qx_ref[:, 0:HEAD_DIM]

    s = _dot_nt(q_all, kcmp_ref[...])
    cmp_end = lax.broadcasted_iota(jnp.int32, (M, n_cmp_pad), 1) * CMP_STRIDE + (CMP_LEN - 1)
    valid = cmp_end <= row_t
    s = jnp.where(valid, s, NEG_INF)
    e = jnp.where(valid, jnp.exp(s - jnp.max(s, axis=1, keepdims=True)), 0.0)
    p_cmp = e / jnp.maximum(jnp.sum(e, axis=1, keepdims=True), 1e-30)
    ocmp_ref[...] = _dot(p_cmp.astype(BF16), vcmp_ref[...])

    p_sum = p_cmp[0:TQ]
    for r in range(1, R):
        p_sum = p_sum + p_cmp[r * TQ:(r + 1) * TQ]
    jn = lax.broadcasted_iota(jnp.int32, (n_sel, n_cmp_pad), 0) * SEL_LEN
    cn = lax.broadcasted_iota(jnp.int32, (n_sel, n_cmp_pad), 1) * CMP_STRIDE
    ov = jnp.minimum(cn + (CMP_LEN - 1), jn + (SEL_LEN - 1)) - jnp.maximum(cn, jn) + 1
    ov_t = (jnp.maximum(ov, 0).astype(F32) / CMP_LEN).astype(BF16)
    p_hi = p_sum.astype(BF16)
    p_lo = (p_sum - p_hi.astype(F32)).astype(BF16)
    imp = _dot_nt(ov_t, p_hi) + _dot_nt(ov_t, p_lo)

    jb = lax.broadcasted_iota(jnp.int32, (n_sel, TQ), 0)
    tq = t0 + lax.broadcasted_iota(jnp.int32, (n_sel, TQ), 1)
    cur = tq // SEL_LEN
    forced = (jb == 0) | (jb == cur) | (jb == cur - 1)
    ok = jb * SEL_LEN <= tq
    score = jnp.where(ok, jnp.where(forced, FORCE_SCORE, imp), NEG_INF)
    sc_ref[...] = score

    rank = jnp.zeros((n_sel, TQ), jnp.int32)
    for j2 in range(n_sel):
        other = sc_ref[j2:j2 + 1, :]
        before = (other > score) | ((other == score) & (jb > j2))
        rank = rank + jnp.where(before, 1, 0)
    n_top = min(SEL_TOPK, n_sel)
    bias_t = jnp.where(rank < n_top, 0.0, SEL_BIAS)
    bias_t = jnp.concatenate([bias_t, jnp.zeros((LANES - n_sel, TQ), F32)], axis=0)
    bias = bias_t.T.astype(BF16)
    for r in range(R):
        qx_ref[r * TQ:(r + 1) * TQ, HEAD_DIM:2 * HEAD_DIM] = bias

    key_lane = lax.broadcasted_iota(jnp.int32, (M, TQ), 1)
    q_in_tile = lax.broadcasted_iota(jnp.int32, (M, TQ), 0) % TQ
    causal = key_lane <= q_in_tile

    diag = pl.ds(pl.multiple_of(t0, TQ), TQ)
    s = _dot_nt(qx_ref[...], ksx_ref[diag, :])
    _flash_first(jnp.where(causal, s, NEG_INF), vs_ref[diag, :], m_ref, l_ref, acc_ref)

    def sel_body(kt, carry):
        rows = pl.ds(pl.multiple_of(kt * TQ, TQ), TQ)
        _flash_next(_dot_nt(qx_ref[...], ksx_ref[rows, :]), vs_ref[rows, :], m_ref, l_ref, acc_ref)
        return carry
    lax.fori_loop(0, qi, sel_body, 0)
    osel_ref[...] = acc_ref[...] / l_ref[...]

    assert WINDOW == 2 * TQ
    s = _dot_nt(q_all, kw_ref[diag, :])
    _flash_first(jnp.where(causal, s, NEG_INF), vw_ref[diag, :], m_ref, l_ref, acc_ref)

    @pl.when(qi >= 1)
    def _():
        rows = pl.ds(pl.multiple_of(t0 - TQ, TQ), TQ)
        _flash_next(_dot_nt(q_all, kw_ref[rows, :]), vw_ref[rows, :], m_ref, l_ref, acc_ref)

    @pl.when(qi >= 2)
    def _():
        rows = pl.ds(pl.multiple_of(t0 - 2 * TQ, TQ), TQ)
        s2 = _dot_nt(q_all, kw_ref[rows, :])
        _flash_next(jnp.where(key_lane > q_in_tile, s2, NEG_INF), vw_ref[rows, :], m_ref, l_ref, acc_ref)

    o_win = acc_ref[...] / l_ref[...]

    gates = 1.0 / (1.0 + jnp.exp(-gl_ref[...]))
    gates = pltpu.roll(gates, (LANES - 3 * R * g) % LANES, axis=1)
    for r in range(R):
        rs = slice(r * TQ, (r + 1) * TQ)
        mix = (gates[:, 3 * r:3 * r + 1] * ocmp_ref[rs, :] + gates[:, 3 * r + 1:3 * r + 2] * osel_ref[rs, :]
               + gates[:, 3 * r + 2:3 * r + 3] * o_win[rs, :])
        cs = slice(r * HEAD_DIM, (r + 1) * HEAD_DIM)
        o_ref[:, cs] = (mix * _silu(zq_ref[:, cs])).astype(o_ref.dtype)


def _nsa(p1, batch, seq, cos_tab, sin_tab, ksx, vs, kw, vw, kcmp, vcmp):
    n = p1.shape[0]
    TQ = Q_TILE
    R = NSA_GROUP
    q_tiles = seq // TQ
    gw = R * HEAD_DIM
    n_cmp_pad = kcmp.shape[2]
    n_sel = seq // SEL_LEN
    tok = lambda b, g, i: b * q_tiles + i
    head_kv = pl.BlockSpec((seq, HEAD_DIM), lambda b, g, i: (b, g))
    cmp_kv = pl.BlockSpec((None, None, n_cmp_pad, HEAD_DIM), lambda b, g, i: (b, g, 0, 0))
    return pl.pallas_call(
        _nsa_kernel,
        grid=(batch, NSA_KV_HEADS, q_tiles),
        in_specs=[
            pl.BlockSpec((TQ, gw), lambda b, g, i: (tok(b, g, i), g)),
            pl.BlockSpec((TQ, gw), lambda b, g, i: (tok(b, g, i), NSA_W // gw + g)),
            pl.BlockSpec((TQ, GATE_PAD), lambda b, g, i: (tok(b, g, i), (2 * NSA_W + 2 * MEM_W) // GATE_PAD)),
            pl.BlockSpec((TQ, LANES), lambda b, g, i: (tok(b, g, i), 0)),
            pl.BlockSpec((TQ, LANES), lambda b, g, i: (tok(b, g, i), 0)),
            pl.BlockSpec((seq, 2 * HEAD_DIM), lambda b, g, i: (b, g)),
            head_kv, head_kv, head_kv, cmp_kv, cmp_kv,
        ],
        out_specs=pl.BlockSpec((TQ, gw), lambda b, g, i: (tok(b, g, i), g)),
        out_shape=jax.ShapeDtypeStruct((n, NSA_W), BF16),
        scratch_shapes=[
            pltpu.VMEM((R * TQ, 2 * HEAD_DIM), BF16),
            pltpu.VMEM((n_sel, TQ), F32),
            pltpu.VMEM((R * TQ, 1), F32),
            pltpu.VMEM((R * TQ, 1), F32),
            pltpu.VMEM((R * TQ, HEAD_DIM), F32),
            pltpu.VMEM((R * TQ, HEAD_DIM), F32),
            pltpu.VMEM((R * TQ, HEAD_DIM), F32),
        ],
        compiler_params=pltpu.CompilerParams(dimension_semantics=("arbitrary", "arbitrary", "arbitrary"),
                                             vmem_limit_bytes=VMEM_LIMIT),
        name="nsa",
    )(p1, p1, p1, cos_tab, sin_tab, ksx, vs, kw, vw, kcmp, vcmp)


def _tail_kernel(y_ref, qz_ref, mk_ref, mv_ref, h_ref, wout_ref, g_ref, o_ref, ymem_ref, acc_ref):
    qm = qz_ref[:, 0:MEM_W]
    zm = qz_ref[:, MEM_W:2 * MEM_W]
    _memory_attention(qm, zm, mk_ref, mv_ref, ymem_ref, 0)
    for c in range(0, D_MODEL, COL_CHUNK):
        cs = slice(c, c + COL_CHUNK)
        acc_ref[:, cs] = (h_ref[:, cs] + _dot(y_ref[...], wout_ref[0:NSA_W, cs])
                          + _dot(ymem_ref[...], wout_ref[NSA_W:D_MODEL, cs]))
    o_ref[...] = _rms_scale(acc_ref[...], g_ref[...])


def _tail(y_nsa, p1, seq, mem_k, mem_v, h2d, w_out, final_g):
    n = h2d.shape[0]
    T = TOK_TILE
    tiles_per_seq = seq // T
    return pl.pallas_call(
        _tail_kernel,
        grid=(n // T,),
        in_specs=[
            pl.BlockSpec((T, NSA_W), lambda i: (i, 0)),
            pl.BlockSpec((T, 2 * MEM_W), lambda i: (i, 2 * NSA_W // (2 * MEM_W))),
            pl.BlockSpec((None, MEM_LEN, MEM_W), lambda i: (i // tiles_per_seq, 0, 0)),
            pl.BlockSpec((None, MEM_LEN, MEM_W), lambda i: (i // tiles_per_seq, 0, 0)),
            pl.BlockSpec((T, D_MODEL), lambda i: (i, 0)),
            _resident((D_MODEL, D_MODEL)),
            _resident((1, D_MODEL)),
        ],
        out_specs=pl.BlockSpec((T, D_MODEL), lambda i: (i, 0)),
        out_shape=jax.ShapeDtypeStruct((n, D_MODEL), F32),
        scratch_shapes=[pltpu.VMEM((T, MEM_W), BF16), pltpu.VMEM((T, D_MODEL), F32)],
        compiler_params=pltpu.CompilerParams(dimension_semantics=("arbitrary",), vmem_limit_bytes=VMEM_LIMIT),
        name="tail",
    )(y_nsa, p1, mem_k, mem_v, h2d, w_out, final_g.reshape(1, D_MODEL))


def kernel(x, mem, positions, norm_g, mem_norm_g, w_mem_kv, w_out, a_w_in, a_w_pool, a_pool_scale, b_w_in,
           kv_norm_g, w_kv, cmp_pe, cmp_w1, cmp_w2, final_g):
    batch, seq, _ = x.shape
    assert seq % Q_TILE == 0 and seq % TOK_TILE == 0 and Q_TILE % SEL_LEN == 0
    n = batch * seq
    x2d = x.reshape(n, D_MODEL)

    mkv = _mem_kv(mem.reshape(batch * MEM_LEN, D_MODEL), mem_norm_g, w_mem_kv.astype(BF16))
    mkv = mkv.reshape(mkv.shape[0], batch, MEM_LEN, 2 * MEM_W)
    mem_k, mem_v = mkv[..., :MEM_W], mkv[..., MEM_W:]

    h1 = _layer0(x2d, seq, norm_g[0], a_w_in[0].astype(BF16), a_w_pool[0].astype(BF16), a_pool_scale[0],
                 mem_k[0], mem_v[0], w_out[0].astype(BF16))

    wb = b_w_in[0]
    o1 = NSA_W
    o2 = o1 + 3 * NSA_HEADS
    o3 = o2 + NSA_W
    wb = jnp.concatenate([wb[:, :o1], wb[:, o2:o3], wb[:, o3:], wb[:, o1:o2],
                          jnp.zeros((D_MODEL, GATE_PAD - 3 * NSA_HEADS), wb.dtype)], axis=1).astype(BF16)
    kvp, p1 = _proj1(h1, kv_norm_g, norm_g[1], w_kv.astype(BF16), wb)

    half = jnp.arange(ROT_HALF, dtype=F32)
    inv = ROPE_THETA ** (-half * 2.0 / ROT_DIM)
    pad = jnp.zeros((LANES - ROT_DIM,), F32)
    inv_full = jnp.concatenate([inv, inv, pad]).reshape(1, LANES)
    sgn_full = jnp.concatenate([-jnp.ones((ROT_HALF,), F32), jnp.ones((ROT_HALF,), F32), pad]).reshape(1, LANES)
    ksx, vs, kw, vw, cos_tab, sin_tab = _kv_rot(kvp, seq, positions.reshape(n, 1), inv_full, sgn_full)
    kcmp, vcmp = _compress(kvp, batch, seq, cos_tab, sin_tab, cmp_pe, cmp_w1.astype(BF16), cmp_w2.astype(BF16))

    y_nsa = _nsa(p1, batch, seq, cos_tab, sin_tab, ksx, vs, kw, vw, kcmp, vcmp)
    out = _tail(y_nsa, p1, seq, mem_k[1], mem_v[1], h1, w_out[1].astype(BF16), final_g)
    return out.reshape(batch, seq, D_MODEL)
```

```python
import functools

import jax
import jax.numpy as jnp
from jax import lax
from jax.experimental import pallas as pl
from jax.experimental.pallas import tpu as pltpu

D_MODEL = 2048
MEM_LEN = 256
HEAD_DIM = 128
MEM_HEADS = 4
MEM_W = MEM_HEADS * HEAD_DIM
POOL_W = D_MODEL - MEM_W
POOL_WINDOWS = (2, 4, 8, 16)
POOL_GC = POOL_W // len(POOL_WINDOWS)
POOL_HALO = 16
NSA_W = D_MODEL - MEM_W
NSA_HEADS = NSA_W // HEAD_DIM
NSA_KV_HEADS = 4
NSA_GROUP = NSA_HEADS // NSA_KV_HEADS
NSA_KV_W = NSA_KV_HEADS * HEAD_DIM
CMP_LEN = 32
CMP_STRIDE = 16
CMP_HID = 256
SEL_LEN = 64
SEL_TOPK = 16
WINDOW = 512
ROT_DIM = HEAD_DIM // 4
ROT_HALF = ROT_DIM // 2
ROPE_THETA = 500000.0
NORM_EPS = 1e-6
FORCE_SCORE = 1e4
NEG_INF = -1e30
SEL_BIAS = -1e9
QK_SCALE = HEAD_DIM ** -0.5
LOG2E = 1.4426950408889634

LANES = 128
TOK_TILE = 256
Q_TILE = 256
COL_CHUNK = 512
GATE_PAD = LANES
VMEM_LIMIT = 56 * 1024 * 1024

F32 = jnp.float32
BF16 = jnp.bfloat16


def _resident(shape):
    nd = len(shape)
    return pl.BlockSpec(shape, lambda *_: (0,) * nd, pipeline_mode=pl.Buffered(1))


def _rms_scale(x, g):
    ms = jnp.mean(x * x, axis=-1, keepdims=True)
    return x * lax.rsqrt(ms + NORM_EPS) * g


def _silu(z):
    return z * (1.0 / (1.0 + jnp.exp(-z)))


def _dot(a, b):
    return jnp.dot(a, b, preferred_element_type=F32)


def _dot_nt(a, b):
    return lax.dot_general(a, b, (((1,), (1,)), ((), ())), preferred_element_type=F32)


def _memory_attention(qm, zm, k_ref, v_ref, y_ref, col0):
    for h in range(MEM_HEADS):
        cs = slice(h * HEAD_DIM, (h + 1) * HEAD_DIM)
        q = (qm[:, cs] * QK_SCALE).astype(BF16)
        s = _dot_nt(q, k_ref[:, cs])
        e = jnp.exp(s - jnp.max(s, axis=-1, keepdims=True))
        p = e / jnp.sum(e, axis=-1, keepdims=True)
        o = _dot(p.astype(BF16), v_ref[:, cs])
        y_ref[:, col0 + h * HEAD_DIM:col0 + (h + 1) * HEAD_DIM] = (o * _silu(zm[:, cs])).astype(y_ref.dtype)


def _mem_kv_kernel(mem_ref, g_ref, w_ref, o_ref):
    hn = _rms_scale(mem_ref[...], g_ref[...]).astype(BF16)
    o_ref[...] = _dot(hn, w_ref[...]).astype(o_ref.dtype)


def _mem_kv(mem2d, mem_norm_g, w_mem_kv_bf16):
    depth = w_mem_kv_bf16.shape[0]
    rows = mem2d.shape[0]
    T = TOK_TILE
    return pl.pallas_call(
        _mem_kv_kernel,
        grid=(depth, rows // T),
        in_specs=[
            pl.BlockSpec((T, D_MODEL), lambda l, i: (i, 0)),
            pl.BlockSpec((None, 1, D_MODEL), lambda l, i: (l, 0, 0)),
            pl.BlockSpec((None, D_MODEL, 2 * MEM_W), lambda l, i: (l, 0, 0)),
        ],
        out_specs=pl.BlockSpec((None, T, 2 * MEM_W), lambda l, i: (l, i, 0)),
        out_shape=jax.ShapeDtypeStruct((depth, rows, 2 * MEM_W), BF16),
        compiler_params=pltpu.CompilerParams(dimension_semantics=("arbitrary", "arbitrary"),
                                             vmem_limit_bytes=VMEM_LIMIT),
        name="mem_kv",
    )(mem2d, mem_norm_g.reshape(depth, 1, D_MODEL), w_mem_kv_bf16)


def _layer0_kernel(tiles_per_seq, x_ref, g_ref, win_ref, wpool_ref, scale_ref, mk_ref, mv_ref, wout_ref,
                   o_ref, hn_ref, proj_ref, uext_ref, y_ref):
    T = x_ref.shape[0]
    tb = pl.program_id(0) % tiles_per_seq
    x = x_ref[...]
    hn_ref[...] = _rms_scale(x, g_ref[...]).astype(BF16)

    @pl.when(tb == 0)
    def _():
        uext_ref[0:POOL_HALO, :] = jnp.zeros((POOL_HALO, POOL_W), F32)

    @pl.when(tb != 0)
    def _():
        uext_ref[0:POOL_HALO, :] = uext_ref[T:T + POOL_HALO, :]

    n_in = win_ref.shape[1]
    for c in range(0, n_in, COL_CHUNK):
        blk = _dot(hn_ref[...], win_ref[:, c:c + COL_CHUNK])
        if c < POOL_W:
            uext_ref[POOL_HALO:POOL_HALO + T, c:c + COL_CHUNK] = blk
        else:
            proj_ref[:, c - POOL_W:c - POOL_W + COL_CHUNK] = blk

    t1 = (tb * T + lax.broadcasted_iota(jnp.int32, (T, 1), 0) + 1).astype(F32)
    for gi, win in enumerate(POOL_WINDOWS):
        cs = slice(gi * POOL_GC, (gi + 1) * POOL_GC)
        u = uext_ref[POOL_HALO:POOL_HALO + T, cs]
        acc = u
        for k in range(1, win):
            acc = acc + uext_ref[POOL_HALO - k:POOL_HALO - k + T, cs]
        pooled = acc / jnp.minimum(t1, float(win)) - u
        mixed = _dot(pooled.astype(BF16), wpool_ref[gi])
        y_ref[:, cs] = (mixed * scale_ref[:, cs] * _silu(proj_ref[:, cs])).astype(BF16)

    qm = proj_ref[:, POOL_W:POOL_W + MEM_W]
    zm = proj_ref[:, POOL_W + MEM_W:POOL_W + 2 * MEM_W]
    _memory_attention(qm, zm, mk_ref, mv_ref, y_ref, POOL_W)

    for c in range(0, D_MODEL, COL_CHUNK):
        o_ref[:, c:c + COL_CHUNK] = x_ref[:, c:c + COL_CHUNK] + _dot(y_ref[...], wout_ref[:, c:c + COL_CHUNK])


def _layer0(x2d, seq, norm_g, w_in, w_pool, pool_scale, mem_k, mem_v, w_out):
    n = x2d.shape[0]
    T = TOK_TILE
    tiles_per_seq = seq // T
    n_in = w_in.shape[1]
    return pl.pallas_call(
        functools.partial(_layer0_kernel, tiles_per_seq),
        grid=(n // T,),
        in_specs=[
            pl.BlockSpec((T, D_MODEL), lambda i: (i, 0)),
            _resident((1, D_MODEL)),
            _resident((D_MODEL, n_in)),
            _resident(w_pool.shape),
            _resident((1, POOL_W)),
            pl.BlockSpec((None, MEM_LEN, MEM_W), lambda i: (i // tiles_per_seq, 0, 0)),
            pl.BlockSpec((None, MEM_LEN, MEM_W), lambda i: (i // tiles_per_seq, 0, 0)),
            _resident((D_MODEL, D_MODEL)),
        ],
        out_specs=pl.BlockSpec((T, D_MODEL), lambda i: (i, 0)),
        out_shape=jax.ShapeDtypeStruct((n, D_MODEL), F32),
        scratch_shapes=[
            pltpu.VMEM((T, D_MODEL), BF16),
            pltpu.VMEM((T, n_in - POOL_W), F32),
            pltpu.VMEM((T + POOL_HALO, POOL_W), F32),
            pltpu.VMEM((T, D_MODEL), BF16),
        ],
        compiler_params=pltpu.CompilerParams(dimension_semantics=("arbitrary",), vmem_limit_bytes=VMEM_LIMIT),
        name="layer0",
    )(x2d, norm_g.reshape(1, D_MODEL), w_in, w_pool, pool_scale.reshape(1, POOL_W), mem_k, mem_v, w_out)


def _proj1_kernel(h_ref, gkv_ref, g1_ref, wkv_ref, w1_ref, kv_ref, p1_ref, hn_ref):
    h = h_ref[...]
    ms = jnp.mean(h * h, axis=-1, keepdims=True)
    hs = h * lax.rsqrt(ms + NORM_EPS)
    hn_ref[...] = (hs * gkv_ref[...]).astype(BF16)
    n_kv = wkv_ref.shape[1]
    for c in range(0, n_kv, COL_CHUNK):
        kv_ref[:, c:c + COL_CHUNK] = _dot(hn_ref[...], wkv_ref[:, c:c + COL_CHUNK])
    hn_ref[...] = (hs * g1_ref[...]).astype(BF16)
    n_1 = w1_ref.shape[1]
    for c in range(0, n_1, COL_CHUNK):
        w = min(COL_CHUNK, n_1 - c)
        p1_ref[:, c:c + w] = _dot(hn_ref[...], w1_ref[:, c:c + w])


def _proj1(h2d, kv_norm_g, norm_g1, w_kv, w_1):
    n = h2d.shape[0]
    T = TOK_TILE
    n_kv, n_1 = w_kv.shape[1], w_1.shape[1]
    return pl.pallas_call(
        _proj1_kernel,
        grid=(n // T,),
        in_specs=[
            pl.BlockSpec((T, D_MODEL), lambda i: (i, 0)),
            _resident((1, D_MODEL)),
            _resident((1, D_MODEL)),
            _resident((D_MODEL, n_kv)),
            _resident((D_MODEL, n_1)),
        ],
        out_specs=[
            pl.BlockSpec((T, n_kv), lambda i: (i, 0)),
            pl.BlockSpec((T, n_1), lambda i: (i, 0)),
        ],
        out_shape=[
            jax.ShapeDtypeStruct((n, n_kv), F32),
            jax.ShapeDtypeStruct((n, n_1), F32),
        ],
        scratch_shapes=[pltpu.VMEM((T, D_MODEL), BF16)],
        compiler_params=pltpu.CompilerParams(dimension_semantics=("arbitrary",), vmem_limit_bytes=VMEM_LIMIT),
        name="proj1",
    )(h2d, kv_norm_g.reshape(1, D_MODEL), norm_g1.reshape(1, D_MODEL), w_kv, w_1)


def _rope(x, cos_t, sin_t):
    lane = lax.broadcasted_iota(jnp.int32, x.shape, 1)
    swapped = jnp.where(lane < ROT_HALF, pltpu.roll(x, LANES - ROT_HALF, axis=1), pltpu.roll(x, ROT_HALF, axis=1))
    return x * cos_t + swapped * sin_t


KV_ROW_TILE = 512


def _kv_rot_kernel(seq, ksv_ref, kwv_ref, pos_ref, inv_ref, sgn_ref, ksx_ref, vso_ref, kwo_ref, vwo_ref, cos_ref,
                   sin_ref):
    rows = ksv_ref.shape[0]
    ang = pos_ref[...].astype(F32) * inv_ref[...]
    c = jnp.cos(ang)
    s = jnp.sin(ang) * sgn_ref[...]
    cos_ref[...] = c
    sin_ref[...] = s
    r0 = (pl.program_id(0) * rows) % seq
    blk = (r0 + lax.broadcasted_iota(jnp.int32, (rows, LANES), 0)) // SEL_LEN
    lane = lax.broadcasted_iota(jnp.int32, (rows, LANES), 1)
    ind = jnp.where(blk == lane, 1.0, 0.0).astype(BF16)
    for h in range(NSA_KV_HEADS):
        hs = slice(h * HEAD_DIM, (h + 1) * HEAD_DIM)
        vsl = slice(NSA_KV_W + h * HEAD_DIM, NSA_KV_W + (h + 1) * HEAD_DIM)
        ksx_ref[:, 2 * h * HEAD_DIM:(2 * h + 1) * HEAD_DIM] = _rope(ksv_ref[:, hs], c, s).astype(BF16)
        ksx_ref[:, (2 * h + 1) * HEAD_DIM:(2 * h + 2) * HEAD_DIM] = ind
        kwo_ref[:, hs] = _rope(kwv_ref[:, hs], c, s).astype(BF16)
        vs_t = ksv_ref[:, vsl].T.astype(BF16)
        vw_t = kwv_ref[:, vsl].T.astype(BF16)
        for j in range(rows // Q_TILE):
            vso_ref[h, j] = vs_t[:, j * Q_TILE:(j + 1) * Q_TILE]
            vwo_ref[h, j] = vw_t[:, j * Q_TILE:(j + 1) * Q_TILE]


def _kv_rot(kvp, seq, pos_col, inv_full, sgn_full):
    n = kvp.shape[0]
    T = KV_ROW_TILE
    pair_w = 2 * NSA_KV_W
    row = lambda w: pl.BlockSpec((T, w), lambda i: (i, 0))
    vt_shape = (NSA_KV_HEADS, n // Q_TILE, HEAD_DIM, Q_TILE)
    vt_spec = pl.BlockSpec((NSA_KV_HEADS, T // Q_TILE, HEAD_DIM, Q_TILE), lambda i: (0, i, 0, 0))
    return pl.pallas_call(
        functools.partial(_kv_rot_kernel, seq),
        grid=(n // T,),
        in_specs=[
            pl.BlockSpec((T, pair_w), lambda i: (i, 1)),
            pl.BlockSpec((T, pair_w), lambda i: (i, 2)),
            row(1),
            pl.BlockSpec((1, LANES), lambda i: (0, 0)),
            pl.BlockSpec((1, LANES), lambda i: (0, 0)),
        ],
        out_specs=[row(2 * NSA_KV_W), vt_spec, row(NSA_KV_W), vt_spec, row(LANES), row(LANES)],
        out_shape=[
            jax.ShapeDtypeStruct((n, 2 * NSA_KV_W), BF16),
            jax.ShapeDtypeStruct(vt_shape, BF16),
            jax.ShapeDtypeStruct((n, NSA_KV_W), BF16),
            jax.ShapeDtypeStruct(vt_shape, BF16),
            jax.ShapeDtypeStruct((n, LANES), F32),
            jax.ShapeDtypeStruct((n, LANES), F32),
        ],
        compiler_params=pltpu.CompilerParams(dimension_semantics=("arbitrary",), vmem_limit_bytes=VMEM_LIMIT),
        name="kv_rot",
    )(kvp, kvp, pos_col, inv_full, sgn_full)


def _compress_kernel(kc_ref, vc_ref, cos_ref, sin_ref, pe_ref, w1_ref, w2_ref, kcmp_ref, vcmp_ref):
    S = kc_ref.shape[0]
    n_half = S // CMP_STRIDE
    assert CMP_LEN == 2 * CMP_STRIDE
    for part, (src_ref, dst_ref) in enumerate(((kc_ref, kcmp_ref), (vc_ref, vcmp_ref))):
        acc_first = jnp.zeros((n_half, CMP_HID), F32)
        acc_second = jnp.zeros((n_half, CMP_HID), F32)
        for l in range(CMP_STRIDE):
            z = src_ref[pl.ds(l, n_half, stride=CMP_STRIDE), :]
            z1 = (z + pe_ref[part, l:l + 1, :]).astype(BF16)
            acc_first = acc_first + _dot(z1, w1_ref[part, l * HEAD_DIM:(l + 1) * HEAD_DIM, :])
            l2 = l + CMP_STRIDE
            z2 = (z + pe_ref[part, l2:l2 + 1, :]).astype(BF16)
            acc_second = acc_second + _dot(z2, w1_ref[part, l2 * HEAD_DIM:(l2 + 1) * HEAD_DIM, :])
        pre = acc_first + pltpu.roll(acc_second, n_half - 1, axis=0)
        out = _dot(_silu(pre).astype(BF16), w2_ref[part])
        if part == 0:
            c_end = pltpu.roll(cos_ref[pl.ds(CMP_STRIDE - 1, n_half, stride=CMP_STRIDE), :], n_half - 1, axis=0)
            s_end = pltpu.roll(sin_ref[pl.ds(CMP_STRIDE - 1, n_half, stride=CMP_STRIDE), :], n_half - 1, axis=0)
            out = _rope(out, c_end, s_end)
            dst_ref[...] = out.astype(BF16)
        else:
            dst_ref[...] = out.T.astype(BF16)


def _compress(kvp, batch, seq, cos_tab, sin_tab, cmp_pe, cmp_w1, cmp_w2):
    n_half = seq // CMP_STRIDE
    cmp_out = pl.BlockSpec((None, None, n_half, HEAD_DIM), lambda b, g: (b, g, 0, 0))
    tab = pl.BlockSpec((seq, LANES), lambda b, g: (b, 0))
    return pl.pallas_call(
        _compress_kernel,
        grid=(batch, NSA_KV_HEADS),
        in_specs=[
            pl.BlockSpec((seq, HEAD_DIM), lambda b, g: (b, g)),
            pl.BlockSpec((seq, HEAD_DIM), lambda b, g: (b, NSA_KV_HEADS + g)),
            tab, tab,
            _resident(cmp_pe.shape), _resident(cmp_w1.shape), _resident(cmp_w2.shape),
        ],
        out_specs=[cmp_out, pl.BlockSpec((None, None, HEAD_DIM, n_half), lambda b, g: (b, g, 0, 0))],
        out_shape=[
            jax.ShapeDtypeStruct((batch, NSA_KV_HEADS, n_half, HEAD_DIM), BF16),
            jax.ShapeDtypeStruct((batch, NSA_KV_HEADS, HEAD_DIM, n_half), BF16),
        ],
        compiler_params=pltpu.CompilerParams(dimension_semantics=("arbitrary", "arbitrary"),
                                             vmem_limit_bytes=VMEM_LIMIT),
        name="compress",
    )(kvp, kvp, cos_tab, sin_tab, cmp_pe, cmp_w1, cmp_w2)


def _flash_first(s_t, v_t, m_ref, l_ref, acc_ref):
    m = jnp.max(s_t, axis=0, keepdims=True)
    p = jnp.exp2(s_t - m)
    m_ref[...] = m
    l_ref[...] = jnp.sum(p, axis=0, keepdims=True)
    acc_ref[...] = _dot(v_t, p.astype(BF16))


def _flash_next(s_t, v_t, m_ref, l_ref, acc_ref):
    m_prev = m_ref[...]
    m_new = jnp.maximum(m_prev, jnp.max(s_t, axis=0, keepdims=True))
    alpha = jnp.exp2(m_prev - m_new)
    p = jnp.exp2(s_t - m_new)
    m_ref[...] = m_new
    l_ref[...] = alpha * l_ref[...] + jnp.sum(p, axis=0, keepdims=True)
    acc_ref[...] = alpha * acc_ref[...] + _dot(v_t, p.astype(BF16))


def _nsa_kernel(q_ref, zq_ref, gl_ref, cos_ref, sin_ref, ksx_ref, vst_ref, kw_ref, vwt_ref, kcmp_ref, vcmpt_ref,
                o_ref, qxt_ref, sc_ref, m_ref, l_ref, acc_ref, ocmp_ref, osel_ref):
    TQ = q_ref.shape[0]
    R = NSA_GROUP
    M = R * TQ
    n_cmp_pad = kcmp_ref.shape[0]
    n_sel = ksx_ref.shape[0] // SEL_LEN
    g = pl.program_id(1)
    qi = pl.program_id(2)
    t0 = qi * TQ

    cos_t = cos_ref[...]
    sin_t = sin_ref[...]
    for r in range(R):
        qr = _rope(q_ref[:, r * HEAD_DIM:(r + 1) * HEAD_DIM], cos_t, sin_t) * (QK_SCALE * LOG2E)
        qxt_ref[0:HEAD_DIM, r * TQ:(r + 1) * TQ] = qr.T.astype(BF16)
    qxt_ref[HEAD_DIM + n_sel:2 * HEAD_DIM, :] = jnp.zeros((HEAD_DIM - n_sel, M), BF16)

    lane_q = lax.broadcasted_iota(jnp.int32, (1, M), 1) % TQ
    q_t = qxt_ref[0:HEAD_DIM, :]

    s_t = _dot(kcmp_ref[...], q_t)
    cmp_end = lax.broadcasted_iota(jnp.int32, (n_cmp_pad, 1), 0) * CMP_STRIDE + (CMP_LEN - 1)
    valid = cmp_end <= t0 + lane_q
    s_t = jnp.where(valid, s_t, NEG_INF)
    e = jnp.where(valid, jnp.exp2(s_t - jnp.max(s_t, axis=0, keepdims=True)), 0.0)
    p_cmp = e / jnp.maximum(jnp.sum(e, axis=0, keepdims=True), 1e-30)
    ocmp_ref[...] = _dot(vcmpt_ref[...], p_cmp.astype(BF16))

    p_sum = p_cmp[:, 0:TQ]
    for r in range(1, R):
        p_sum = p_sum + p_cmp[:, r * TQ:(r + 1) * TQ]
    jn = lax.broadcasted_iota(jnp.int32, (n_sel, n_cmp_pad), 0) * SEL_LEN
    cn = lax.broadcasted_iota(jnp.int32, (n_sel, n_cmp_pad), 1) * CMP_STRIDE
    ov = jnp.minimum(cn + (CMP_LEN - 1), jn + (SEL_LEN - 1)) - jnp.maximum(cn, jn) + 1
    ov_t = (jnp.maximum(ov, 0).astype(F32) / CMP_LEN).astype(BF16)
    p_hi = p_sum.astype(BF16)
    p_lo = (p_sum - p_hi.astype(F32)).astype(BF16)
    imp = _dot(ov_t, p_hi) + _dot(ov_t, p_lo)

    jb = lax.broadcasted_iota(jnp.int32, (n_sel, TQ), 0)
    tq = t0 + lax.broadcasted_iota(jnp.int32, (n_sel, TQ), 1)
    cur = tq // SEL_LEN
    forced = (jb == 0) | (jb == cur) | (jb == cur - 1)
    ok = jb * SEL_LEN <= tq
    score = jnp.where(ok, jnp.where(forced, FORCE_SCORE, imp), NEG_INF)
    sc_ref[...] = score

    rank = jnp.zeros((n_sel, TQ), jnp.int32)
    for j2 in range(n_sel):
        other = sc_ref[j2:j2 + 1, :]
        before = (other > score) | ((other == score) & (jb > j2))
        rank = rank + jnp.where(before, 1, 0)
    n_top = min(SEL_TOPK, n_sel)
    bias_t = jnp.where(rank < n_top, 0.0, SEL_BIAS).astype(BF16)
    for r in range(R):
        qxt_ref[HEAD_DIM:HEAD_DIM + n_sel, r * TQ:(r + 1) * TQ] = bias_t

    key_sub = lax.broadcasted_iota(jnp.int32, (TQ, 1), 0)
    causal = key_sub <= lane_q

    diag = pl.ds(pl.multiple_of(t0, TQ), TQ)
    s_t = _dot(ksx_ref[diag, :], qxt_ref[...])
    _flash_first(jnp.where(causal, s_t, NEG_INF), vst_ref[qi], m_ref, l_ref, acc_ref)

    def sel_body(kt, carry):
        rows = pl.ds(pl.multiple_of(kt * TQ, TQ), TQ)
        _flash_next(_dot(ksx_ref[rows, :], qxt_ref[...]), vst_ref[kt], m_ref, l_ref, acc_ref)
        return carry
    lax.fori_loop(0, qi, sel_body, 0)
    osel_ref[...] = acc_ref[...] / l_ref[...]

    assert WINDOW == 2 * TQ
    s_t = _dot(kw_ref[diag, :], q_t)
    _flash_first(jnp.where(causal, s_t, NEG_INF), vwt_ref[qi], m_ref, l_ref, acc_ref)

    @pl.when(qi >= 1)
    def _():
        rows = pl.ds(pl.multiple_of(t0 - TQ, TQ), TQ)
        _flash_next(_dot(kw_ref[rows, :], q_t), vwt_ref[qi - 1], m_ref, l_ref, acc_ref)

    @pl.when(qi >= 2)
    def _():
        rows = pl.ds(pl.multiple_of(t0 - 2 * TQ, TQ), TQ)
        s2 = _dot(kw_ref[rows, :], q_t)
        _flash_next(jnp.where(key_sub > lane_q, s2, NEG_INF), vwt_ref[qi - 2], m_ref, l_ref, acc_ref)

    o_win = acc_ref[...] / l_ref[...]

    gates = 1.0 / (1.0 + jnp.exp(-gl_ref[...]))
    gates_t = pltpu.roll(gates, (LANES - 3 * R * g) % LANES, axis=1).T
    for r in range(R):
        rs = slice(r * TQ, (r + 1) * TQ)
        mix_t = (gates_t[3 * r:3 * r + 1, :] * ocmp_ref[:, rs] + gates_t[3 * r + 1:3 * r + 2, :] * osel_ref[:, rs]
                 + gates_t[3 * r + 2:3 * r + 3, :] * o_win[:, rs])
        cs = slice(r * HEAD_DIM, (r + 1) * HEAD_DIM)
        o_ref[:, cs] = (mix_t.T * _silu(zq_ref[:, cs])).astype(o_ref.dtype)


def _nsa(p1, batch, seq, cos_tab, sin_tab, ksx, vs_t, kw, vw_t, kcmp, vcmp_t):
    n = p1.shape[0]
    TQ = Q_TILE
    R = NSA_GROUP
    q_tiles = seq // TQ
    gw = R * HEAD_DIM
    n_cmp_pad = kcmp.shape[2]
    n_sel = seq // SEL_LEN
    tok = lambda b, g, i: b * q_tiles + i
    head_kv = pl.BlockSpec((seq, HEAD_DIM), lambda b, g, i: (b, g))
    head_vt = pl.BlockSpec((None, q_tiles, HEAD_DIM, TQ), lambda b, g, i: (g, b, 0, 0))
    cmp_kv = pl.BlockSpec((None, None, n_cmp_pad, HEAD_DIM), lambda b, g, i: (b, g, 0, 0))
    cmp_vt = pl.BlockSpec((None, None, HEAD_DIM, n_cmp_pad), lambda b, g, i: (b, g, 0, 0))
    return pl.pallas_call(
        _nsa_kernel,
        grid=(batch, NSA_KV_HEADS, q_tiles),
        in_specs=[
            pl.BlockSpec((TQ, gw), lambda b, g, i: (tok(b, g, i), g)),
            pl.BlockSpec((TQ, gw), lambda b, g, i: (tok(b, g, i), NSA_W // gw + g)),
            pl.BlockSpec((TQ, GATE_PAD), lambda b, g, i: (tok(b, g, i), (2 * NSA_W + 2 * MEM_W) // GATE_PAD)),
            pl.BlockSpec((TQ, LANES), lambda b, g, i: (tok(b, g, i), 0)),
            pl.BlockSpec((TQ, LANES), lambda b, g, i: (tok(b, g, i), 0)),
            pl.BlockSpec((seq, 2 * HEAD_DIM), lambda b, g, i: (b, g)),
            head_vt, head_kv, head_vt, cmp_kv, cmp_vt,
        ],
        out_specs=pl.BlockSpec((TQ, gw), lambda b, g, i: (tok(b, g, i), g)),
        out_shape=jax.ShapeDtypeStruct((n, NSA_W), BF16),
        scratch_shapes=[
            pltpu.VMEM((2 * HEAD_DIM, R * TQ), BF16),
            pltpu.VMEM((n_sel, TQ), F32),
            pltpu.VMEM((1, R * TQ), F32),
            pltpu.VMEM((1, R * TQ), F32),
            pltpu.VMEM((HEAD_DIM, R * TQ), F32),
            pltpu.VMEM((HEAD_DIM, R * TQ), F32),
            pltpu.VMEM((HEAD_DIM, R * TQ), F32),
        ],
        compiler_params=pltpu.CompilerParams(dimension_semantics=("arbitrary", "arbitrary", "arbitrary"),
                                             vmem_limit_bytes=VMEM_LIMIT),
        name="nsa",
    )(p1, p1, p1, cos_tab, sin_tab, ksx, vs_t, kw, vw_t, kcmp, vcmp_t)


def _tail_kernel(y_ref, qz_ref, mk_ref, mv_ref, h_ref, wout_ref, g_ref, o_ref, ymem_ref, acc_ref):
    qm = qz_ref[:, 0:MEM_W]
    zm = qz_ref[:, MEM_W:2 * MEM_W]
    _memory_attention(qm, zm, mk_ref, mv_ref, ymem_ref, 0)
    for c in range(0, D_MODEL, COL_CHUNK):
        cs = slice(c, c + COL_CHUNK)
        acc_ref[:, cs] = (h_ref[:, cs] + _dot(y_ref[...], wout_ref[0:NSA_W, cs])
                          + _dot(ymem_ref[...], wout_ref[NSA_W:D_MODEL, cs]))
    o_ref[...] = _rms_scale(acc_ref[...], g_ref[...])


def _tail(y_nsa, p1, seq, mem_k, mem_v, h2d, w_out, final_g):
    n = h2d.shape[0]
    T = TOK_TILE
    tiles_per_seq = seq // T
    return pl.pallas_call(
        _tail_kernel,
        grid=(n // T,),
        in_specs=[
            pl.BlockSpec((T, NSA_W), lambda i: (i, 0)),
            pl.BlockSpec((T, 2 * MEM_W), lambda i: (i, 2 * NSA_W // (2 * MEM_W))),
            pl.BlockSpec((None, MEM_LEN, MEM_W), lambda i: (i // tiles_per_seq, 0, 0)),
            pl.BlockSpec((None, MEM_LEN, MEM_W), lambda i: (i // tiles_per_seq, 0, 0)),
            pl.BlockSpec((T, D_MODEL), lambda i: (i, 0)),
            _resident((D_MODEL, D_MODEL)),
            _resident((1, D_MODEL)),
        ],
        out_specs=pl.BlockSpec((T, D_MODEL), lambda i: (i, 0)),
        out_shape=jax.ShapeDtypeStruct((n, D_MODEL), F32),
        scratch_shapes=[pltpu.VMEM((T, MEM_W), BF16), pltpu.VMEM((T, D_MODEL), F32)],
        compiler_params=pltpu.CompilerParams(dimension_semantics=("arbitrary",), vmem_limit_bytes=VMEM_LIMIT),
        name="tail",
    )(y_nsa, p1, mem_k, mem_v, h2d, w_out, final_g.reshape(1, D_MODEL))


def kernel(x, mem, positions, norm_g, mem_norm_g, w_mem_kv, w_out, a_w_in, a_w_pool, a_pool_scale, b_w_in,
           kv_norm_g, w_kv, cmp_pe, cmp_w1, cmp_w2, final_g):
    batch, seq, _ = x.shape
    assert seq % Q_TILE == 0 and seq % TOK_TILE == 0 and Q_TILE % SEL_LEN == 0
    n = batch * seq
    x2d = x.reshape(n, D_MODEL)

    mkv = _mem_kv(mem.reshape(batch * MEM_LEN, D_MODEL), mem_norm_g, w_mem_kv.astype(BF16))
    mkv = mkv.reshape(mkv.shape[0], batch, MEM_LEN, 2 * MEM_W)
    mem_k, mem_v = mkv[..., :MEM_W], mkv[..., MEM_W:]

    h1 = _layer0(x2d, seq, norm_g[0], a_w_in[0].astype(BF16), a_w_pool[0].astype(BF16), a_pool_scale[0],
                 mem_k[0], mem_v[0], w_out[0].astype(BF16))

    wb = b_w_in[0]
    o1 = NSA_W
    o2 = o1 + 3 * NSA_HEADS
    o3 = o2 + NSA_W
    wb = jnp.concatenate([wb[:, :o1], wb[:, o2:o3], wb[:, o3:], wb[:, o1:o2],
                          jnp.zeros((D_MODEL, GATE_PAD - 3 * NSA_HEADS), wb.dtype)], axis=1).astype(BF16)
    kvp, p1 = _proj1(h1, kv_norm_g, norm_g[1], w_kv.astype(BF16), wb)

    half = jnp.arange(ROT_HALF, dtype=F32)
    inv = ROPE_THETA ** (-half * 2.0 / ROT_DIM)
    pad = jnp.zeros((LANES - ROT_DIM,), F32)
    inv_full = jnp.concatenate([inv, inv, pad]).reshape(1, LANES)
    sgn_full = jnp.concatenate([-jnp.ones((ROT_HALF,), F32), jnp.ones((ROT_HALF,), F32), pad]).reshape(1, LANES)
    ksx, vs, kw, vw, cos_tab, sin_tab = _kv_rot(kvp, seq, positions.reshape(n, 1), inv_full, sgn_full)
    kcmp, vcmp = _compress(kvp, batch, seq, cos_tab, sin_tab, cmp_pe, cmp_w1.astype(BF16), cmp_w2.astype(BF16))

    y_nsa = _nsa(p1, batch, seq, cos_tab, sin_tab, ksx, vs, kw, vw, kcmp, vcmp)
    out = _tail(y_nsa, p1, seq, mem_k[1], mem_v[1], h1, w_out[1].astype(BF16), final_g)
    return out.reshape(batch, seq, D_MODEL)
```

```python
import functools

import jax
import jax.numpy as jnp
from jax import lax
from jax.experimental import pallas as pl
from jax.experimental.pallas import tpu as pltpu

D_MODEL = 2048
MEM_LEN = 256
HEAD_DIM = 128
MEM_HEADS = 4
MEM_W = MEM_HEADS * HEAD_DIM
POOL_W = D_MODEL - MEM_W
POOL_WINDOWS = (2, 4, 8, 16)
POOL_GC = POOL_W // len(POOL_WINDOWS)
POOL_HALO = 16
NSA_W = D_MODEL - MEM_W
NSA_HEADS = NSA_W // HEAD_DIM
NSA_KV_HEADS = 4
NSA_GROUP = NSA_HEADS // NSA_KV_HEADS
NSA_KV_W = NSA_KV_HEADS * HEAD_DIM
CMP_LEN = 32
CMP_STRIDE = 16
CMP_HID = 256
SEL_LEN = 64
SEL_TOPK = 16
WINDOW = 512
ROT_DIM = HEAD_DIM // 4
ROT_HALF = ROT_DIM // 2
ROPE_THETA = 500000.0
NORM_EPS = 1e-6
FORCE_SCORE = 1e4
NEG_INF = -1e30
SEL_BIAS = -1e9
QK_SCALE = HEAD_DIM ** -0.5
LOG2E = 1.4426950408889634

LANES = 128
TOK_TILE = 256
Q_TILE = 256
COL_CHUNK = 512
GATE_PAD = LANES
VMEM_LIMIT = 56 * 1024 * 1024

F32 = jnp.float32
BF16 = jnp.bfloat16


def _resident(shape):
    nd = len(shape)
    return pl.BlockSpec(shape, lambda *_: (0,) * nd, pipeline_mode=pl.Buffered(1))


def _rms_scale(x, g):
    ms = jnp.mean(x * x, axis=-1, keepdims=True)
    return x * lax.rsqrt(ms + NORM_EPS) * g


def _silu(z):
    return z * (1.0 / (1.0 + jnp.exp(-z)))


def _dot(a, b):
    return jnp.dot(a, b, preferred_element_type=F32)


def _dot_nt(a, b):
    return lax.dot_general(a, b, (((1,), (1,)), ((), ())), preferred_element_type=F32)


def _memory_attention(qm, zm, k_ref, v_ref, y_ref, col0):
    for h in range(MEM_HEADS):
        cs = slice(h * HEAD_DIM, (h + 1) * HEAD_DIM)
        q = (qm[:, cs] * QK_SCALE).astype(BF16)
        s = _dot_nt(q, k_ref[:, cs])
        e = jnp.exp(s - jnp.max(s, axis=-1, keepdims=True))
        p = e / jnp.sum(e, axis=-1, keepdims=True)
        o = _dot(p.astype(BF16), v_ref[:, cs])
        y_ref[:, col0 + h * HEAD_DIM:col0 + (h + 1) * HEAD_DIM] = (o * _silu(zm[:, cs])).astype(y_ref.dtype)


def _mem_kv_kernel(mem_ref, g_ref, w_ref, o_ref):
    hn = _rms_scale(mem_ref[...], g_ref[...]).astype(BF16)
    o_ref[...] = _dot(hn, w_ref[...]).astype(o_ref.dtype)


def _mem_kv(mem2d, mem_norm_g, w_mem_kv_bf16):
    depth = w_mem_kv_bf16.shape[0]
    rows = mem2d.shape[0]
    T = TOK_TILE
    return pl.pallas_call(
        _mem_kv_kernel,
        grid=(depth, rows // T),
        in_specs=[
            pl.BlockSpec((T, D_MODEL), lambda l, i: (i, 0)),
            pl.BlockSpec((None, 1, D_MODEL), lambda l, i: (l, 0, 0)),
            pl.BlockSpec((None, D_MODEL, 2 * MEM_W), lambda l, i: (l, 0, 0)),
        ],
        out_specs=pl.BlockSpec((None, T, 2 * MEM_W), lambda l, i: (l, i, 0)),
        out_shape=jax.ShapeDtypeStruct((depth, rows, 2 * MEM_W), BF16),
        compiler_params=pltpu.CompilerParams(dimension_semantics=("arbitrary", "arbitrary"),
                                             vmem_limit_bytes=VMEM_LIMIT),
        name="mem_kv",
    )(mem2d, mem_norm_g.reshape(depth, 1, D_MODEL), w_mem_kv_bf16)


def _layer0_kernel(tiles_per_seq, x_ref, g_ref, win_ref, wpool_ref, scale_ref, mk_ref, mv_ref, wout_ref,
                   o_ref, hn_ref, proj_ref, uext_ref, y_ref):
    T = x_ref.shape[0]
    tb = pl.program_id(0) % tiles_per_seq
    x = x_ref[...]
    hn_ref[...] = _rms_scale(x, g_ref[...]).astype(BF16)

    @pl.when(tb == 0)
    def _():
        uext_ref[0:POOL_HALO, :] = jnp.zeros((POOL_HALO, POOL_W), F32)

    @pl.when(tb != 0)
    def _():
        uext_ref[0:POOL_HALO, :] = uext_ref[T:T + POOL_HALO, :]

    n_in = win_ref.shape[1]
    for c in range(0, n_in, COL_CHUNK):
        blk = _dot(hn_ref[...], win_ref[:, c:c + COL_CHUNK])
        if c < POOL_W:
            uext_ref[POOL_HALO:POOL_HALO + T, c:c + COL_CHUNK] = blk
        else:
            proj_ref[:, c - POOL_W:c - POOL_W + COL_CHUNK] = blk

    t1 = (tb * T + lax.broadcasted_iota(jnp.int32, (T, 1), 0) + 1).astype(F32)
    for gi, win in enumerate(POOL_WINDOWS):
        cs = slice(gi * POOL_GC, (gi + 1) * POOL_GC)
        u = uext_ref[POOL_HALO:POOL_HALO + T, cs]
        acc = u
        for k in range(1, win):
            acc = acc + uext_ref[POOL_HALO - k:POOL_HALO - k + T, cs]
        pooled = acc / jnp.minimum(t1, float(win)) - u
        mixed = _dot(pooled.astype(BF16), wpool_ref[gi])
        y_ref[:, cs] = (mixed * scale_ref[:, cs] * _silu(proj_ref[:, cs])).astype(BF16)

    qm = proj_ref[:, POOL_W:POOL_W + MEM_W]
    zm = proj_ref[:, POOL_W + MEM_W:POOL_W + 2 * MEM_W]
    _memory_attention(qm, zm, mk_ref, mv_ref, y_ref, POOL_W)

    for c in range(0, D_MODEL, COL_CHUNK):
        o_ref[:, c:c + COL_CHUNK] = x_ref[:, c:c + COL_CHUNK] + _dot(y_ref[...], wout_ref[:, c:c + COL_CHUNK])


def _layer0(x2d, seq, norm_g, w_in, w_pool, pool_scale, mem_k, mem_v, w_out):
    n = x2d.shape[0]
    T = TOK_TILE
    tiles_per_seq = seq // T
    n_in = w_in.shape[1]
    return pl.pallas_call(
        functools.partial(_layer0_kernel, tiles_per_seq),
        grid=(n // T,),
        in_specs=[
            pl.BlockSpec((T, D_MODEL), lambda i: (i, 0)),
            _resident((1, D_MODEL)),
            _resident((D_MODEL, n_in)),
            _resident(w_pool.shape),
            _resident((1, POOL_W)),
            pl.BlockSpec((None, MEM_LEN, MEM_W), lambda i: (i // tiles_per_seq, 0, 0)),
            pl.BlockSpec((None, MEM_LEN, MEM_W), lambda i: (i // tiles_per_seq, 0, 0)),
            _resident((D_MODEL, D_MODEL)),
        ],
        out_specs=pl.BlockSpec((T, D_MODEL), lambda i: (i, 0)),
        out_shape=jax.ShapeDtypeStruct((n, D_MODEL), F32),
        scratch_shapes=[
            pltpu.VMEM((T, D_MODEL), BF16),
            pltpu.VMEM((T, n_in - POOL_W), F32),
            pltpu.VMEM((T + POOL_HALO, POOL_W), F32),
            pltpu.VMEM((T, D_MODEL), BF16),
        ],
        compiler_params=pltpu.CompilerParams(dimension_semantics=("arbitrary",), vmem_limit_bytes=VMEM_LIMIT),
        name="layer0",
    )(x2d, norm_g.reshape(1, D_MODEL), w_in, w_pool, pool_scale.reshape(1, POOL_W), mem_k, mem_v, w_out)


def _proj1_kernel(h_ref, gkv_ref, g1_ref, wkv_ref, w1_ref, kv_ref, p1_ref, hn_ref):
    h = h_ref[...]
    ms = jnp.mean(h * h, axis=-1, keepdims=True)
    hs = h * lax.rsqrt(ms + NORM_EPS)
    hn_ref[...] = (hs * gkv_ref[...]).astype(BF16)
    n_kv = wkv_ref.shape[1]
    for c in range(0, n_kv, COL_CHUNK):
        kv_ref[:, c:c + COL_CHUNK] = _dot(hn_ref[...], wkv_ref[:, c:c + COL_CHUNK])
    hn_ref[...] = (hs * g1_ref[...]).astype(BF16)
    n_1 = w1_ref.shape[1]
    for c in range(0, n_1, COL_CHUNK):
        w = min(COL_CHUNK, n_1 - c)
        p1_ref[:, c:c + w] = _dot(hn_ref[...], w1_ref[:, c:c + w])


def _proj1(h2d, kv_norm_g, norm_g1, w_kv, w_1):
    n = h2d.shape[0]
    T = TOK_TILE
    n_kv, n_1 = w_kv.shape[1], w_1.shape[1]
    return pl.pallas_call(
        _proj1_kernel,
        grid=(n // T,),
        in_specs=[
            pl.BlockSpec((T, D_MODEL), lambda i: (i, 0)),
            _resident((1, D_MODEL)),
            _resident((1, D_MODEL)),
            _resident((D_MODEL, n_kv)),
            _resident((D_MODEL, n_1)),
        ],
        out_specs=[
            pl.BlockSpec((T, n_kv), lambda i: (i, 0)),
            pl.BlockSpec((T, n_1), lambda i: (i, 0)),
        ],
        out_shape=[
            jax.ShapeDtypeStruct((n, n_kv), F32),
            jax.ShapeDtypeStruct((n, n_1), F32),
        ],
        scratch_shapes=[pltpu.VMEM((T, D_MODEL), BF16)],
        compiler_params=pltpu.CompilerParams(dimension_semantics=("arbitrary",), vmem_limit_bytes=VMEM_LIMIT),
        name="proj1",
    )(h2d, kv_norm_g.reshape(1, D_MODEL), norm_g1.reshape(1, D_MODEL), w_kv, w_1)


def _rope(x, cos_t, sin_t):
    lane = lax.broadcasted_iota(jnp.int32, x.shape, 1)
    swapped = jnp.where(lane < ROT_HALF, pltpu.roll(x, LANES - ROT_HALF, axis=1), pltpu.roll(x, ROT_HALF, axis=1))
    return x * cos_t + swapped * sin_t


KV_ROW_TILE = 512


def _kv_rot_kernel(seq, ksv_ref, kwv_ref, pos_ref, inv_ref, sgn_ref, ksx_ref, vso_ref, kwo_ref, vwo_ref, cos_ref,
                   sin_ref):
    rows = ksv_ref.shape[0]
    ang = pos_ref[...].astype(F32) * inv_ref[...]
    c = jnp.cos(ang)
    s = jnp.sin(ang) * sgn_ref[...]
    cos_ref[...] = c
    sin_ref[...] = s
    r0 = (pl.program_id(0) * rows) % seq
    blk = (r0 + lax.broadcasted_iota(jnp.int32, (rows, LANES), 0)) // SEL_LEN
    lane = lax.broadcasted_iota(jnp.int32, (rows, LANES), 1)
    ind = jnp.where(blk == lane, 1.0, 0.0).astype(BF16)
    for h in range(NSA_KV_HEADS):
        hs = slice(h * HEAD_DIM, (h + 1) * HEAD_DIM)
        vsl = slice(NSA_KV_W + h * HEAD_DIM, NSA_KV_W + (h + 1) * HEAD_DIM)
        ksx_ref[:, 2 * h * HEAD_DIM:(2 * h + 1) * HEAD_DIM] = _rope(ksv_ref[:, hs], c, s).astype(BF16)
        ksx_ref[:, (2 * h + 1) * HEAD_DIM:(2 * h + 2) * HEAD_DIM] = ind
        kwo_ref[:, hs] = _rope(kwv_ref[:, hs], c, s).astype(BF16)
        vs_t = ksv_ref[:, vsl].T.astype(BF16)
        vw_t = kwv_ref[:, vsl].T.astype(BF16)
        for j in range(rows // Q_TILE):
            vso_ref[h, j] = vs_t[:, j * Q_TILE:(j + 1) * Q_TILE]
            vwo_ref[h, j] = vw_t[:, j * Q_TILE:(j + 1) * Q_TILE]


def _kv_rot(kvp, seq, pos_col, inv_full, sgn_full):
    n = kvp.shape[0]
    T = KV_ROW_TILE
    pair_w = 2 * NSA_KV_W
    row = lambda w: pl.BlockSpec((T, w), lambda i: (i, 0))
    vt_shape = (NSA_KV_HEADS, n // Q_TILE, HEAD_DIM, Q_TILE)
    vt_spec = pl.BlockSpec((NSA_KV_HEADS, T // Q_TILE, HEAD_DIM, Q_TILE), lambda i: (0, i, 0, 0))
    return pl.pallas_call(
        functools.partial(_kv_rot_kernel, seq),
        grid=(n // T,),
        in_specs=[
            pl.BlockSpec((T, pair_w), lambda i: (i, 1)),
            pl.BlockSpec((T, pair_w), lambda i: (i, 2)),
            row(1),
            pl.BlockSpec((1, LANES), lambda i: (0, 0)),
            pl.BlockSpec((1, LANES), lambda i: (0, 0)),
        ],
        out_specs=[row(2 * NSA_KV_W), vt_spec, row(NSA_KV_W), vt_spec, row(LANES), row(LANES)],
        out_shape=[
            jax.ShapeDtypeStruct((n, 2 * NSA_KV_W), BF16),
            jax.ShapeDtypeStruct(vt_shape, BF16),
            jax.ShapeDtypeStruct((n, NSA_KV_W), BF16),
            jax.ShapeDtypeStruct(vt_shape, BF16),
            jax.ShapeDtypeStruct((n, LANES), F32),
            jax.ShapeDtypeStruct((n, LANES), F32),
        ],
        compiler_params=pltpu.CompilerParams(dimension_semantics=("arbitrary",), vmem_limit_bytes=VMEM_LIMIT),
        name="kv_rot",
    )(kvp, kvp, pos_col, inv_full, sgn_full)


def _compress_kernel(kc_ref, vc_ref, cos_ref, sin_ref, pe_ref, w1_ref, w2_ref, kcmp_ref, vcmp_ref):
    S = kc_ref.shape[0]
    n_half = S // CMP_STRIDE
    assert CMP_LEN == 2 * CMP_STRIDE
    for part, (src_ref, dst_ref) in enumerate(((kc_ref, kcmp_ref), (vc_ref, vcmp_ref))):
        acc_first = jnp.zeros((n_half, CMP_HID), F32)
        acc_second = jnp.zeros((n_half, CMP_HID), F32)
        for l in range(CMP_STRIDE):
            z = src_ref[pl.ds(l, n_half, stride=CMP_STRIDE), :]
            z1 = (z + pe_ref[part, l:l + 1, :]).astype(BF16)
            acc_first = acc_first + _dot(z1, w1_ref[part, l * HEAD_DIM:(l + 1) * HEAD_DIM, :])
            l2 = l + CMP_STRIDE
            z2 = (z + pe_ref[part, l2:l2 + 1, :]).astype(BF16)
            acc_second = acc_second + _dot(z2, w1_ref[part, l2 * HEAD_DIM:(l2 + 1) * HEAD_DIM, :])
        pre = acc_first + pltpu.roll(acc_second, n_half - 1, axis=0)
        out = _dot(_silu(pre).astype(BF16), w2_ref[part])
        if part == 0:
            c_end = pltpu.roll(cos_ref[pl.ds(CMP_STRIDE - 1, n_half, stride=CMP_STRIDE), :], n_half - 1, axis=0)
            s_end = pltpu.roll(sin_ref[pl.ds(CMP_STRIDE - 1, n_half, stride=CMP_STRIDE), :], n_half - 1, axis=0)
            out = _rope(out, c_end, s_end)
            dst_ref[...] = out.astype(BF16)
        else:
            dst_ref[...] = out.T.astype(BF16)


def _compress(kvp, batch, seq, cos_tab, sin_tab, cmp_pe, cmp_w1, cmp_w2):
    n_half = seq // CMP_STRIDE
    cmp_out = pl.BlockSpec((None, None, n_half, HEAD_DIM), lambda b, g: (b, g, 0, 0))
    tab = pl.BlockSpec((seq, LANES), lambda b, g: (b, 0))
    return pl.pallas_call(
        _compress_kernel,
        grid=(batch, NSA_KV_HEADS),
        in_specs=[
            pl.BlockSpec((seq, HEAD_DIM), lambda b, g: (b, g)),
            pl.BlockSpec((seq, HEAD_DIM), lambda b, g: (b, NSA_KV_HEADS + g)),
            tab, tab,
            _resident(cmp_pe.shape), _resident(cmp_w1.shape), _resident(cmp_w2.shape),
        ],
        out_specs=[cmp_out, pl.BlockSpec((None, None, HEAD_DIM, n_half), lambda b, g: (b, g, 0, 0))],
        out_shape=[
            jax.ShapeDtypeStruct((batch, NSA_KV_HEADS, n_half, HEAD_DIM), BF16),
            jax.ShapeDtypeStruct((batch, NSA_KV_HEADS, HEAD_DIM, n_half), BF16),
        ],
        compiler_params=pltpu.CompilerParams(dimension_semantics=("arbitrary", "arbitrary"),
                                             vmem_limit_bytes=VMEM_LIMIT),
        name="compress",
    )(kvp, kvp, cos_tab, sin_tab, cmp_pe, cmp_w1, cmp_w2)


def _flash_tile(k_tile, qt_ref, q_rows, mask, v_t, m_ref, l_ref, acc_ref, first):
    tq = qt_ref.shape[1] // NSA_GROUP
    slabs = [slice(r * tq, (r + 1) * tq) for r in range(NSA_GROUP)]
    scores = [_dot(k_tile, qt_ref[0:q_rows, rs]) for rs in slabs]
    for rs, s_t in zip(slabs, scores):
        if mask is not None:
            s_t = jnp.where(mask, s_t, NEG_INF)
        m_tile = jnp.max(s_t, axis=0, keepdims=True)
        if first:
            p = jnp.exp2(s_t - m_tile)
            m_ref[:, rs] = m_tile
            l_ref[:, rs] = jnp.sum(p, axis=0, keepdims=True)
            acc_ref[:, rs] = _dot(v_t, p.astype(BF16))
        else:
            m_prev = m_ref[:, rs]
            m_new = jnp.maximum(m_prev, m_tile)
            alpha = jnp.exp2(m_prev - m_new)
            p = jnp.exp2(s_t - m_new)
            m_ref[:, rs] = m_new
            l_ref[:, rs] = alpha * l_ref[:, rs] + jnp.sum(p, axis=0, keepdims=True)
            acc_ref[:, rs] = alpha * acc_ref[:, rs] + _dot(v_t, p.astype(BF16))


def _nsa_kernel(q_ref, zq_ref, gl_ref, cos_ref, sin_ref, ksx_ref, vst_ref, kw_ref, vwt_ref, kcmp_ref, vcmpt_ref,
                o_ref, qxt_ref, sc_ref, m_ref, l_ref, acc_ref, ocmp_ref, osel_ref, s_ref, p_ref, alpha_ref):
    TQ = q_ref.shape[0]
    R = NSA_GROUP
    M = R * TQ
    n_cmp_pad = kcmp_ref.shape[0]
    n_sel = ksx_ref.shape[0] // SEL_LEN
    g = pl.program_id(1)
    qi = pl.program_id(2)
    t0 = qi * TQ

    cos_t = cos_ref[...]
    sin_t = sin_ref[...]
    for r in range(R):
        qr = _rope(q_ref[:, r * HEAD_DIM:(r + 1) * HEAD_DIM], cos_t, sin_t) * (QK_SCALE * LOG2E)
        qxt_ref[0:HEAD_DIM, r * TQ:(r + 1) * TQ] = qr.T.astype(BF16)
    qxt_ref[HEAD_DIM + n_sel:2 * HEAD_DIM, :] = jnp.zeros((HEAD_DIM - n_sel, M), BF16)

    lane_q = lax.broadcasted_iota(jnp.int32, (1, M), 1) % TQ
    q_t = qxt_ref[0:HEAD_DIM, :]

    s_t = _dot(kcmp_ref[...], q_t)
    cmp_end = lax.broadcasted_iota(jnp.int32, (n_cmp_pad, 1), 0) * CMP_STRIDE + (CMP_LEN - 1)
    valid = cmp_end <= t0 + lane_q
    s_t = jnp.where(valid, s_t, NEG_INF)
    e = jnp.where(valid, jnp.exp2(s_t - jnp.max(s_t, axis=0, keepdims=True)), 0.0)
    p_cmp = e * (1.0 / jnp.maximum(jnp.sum(e, axis=0, keepdims=True), 1e-30))
    ocmp_ref[...] = _dot(vcmpt_ref[...], p_cmp.astype(BF16))

    p_sum = p_cmp[:, 0:TQ]
    for r in range(1, R):
        p_sum = p_sum + p_cmp[:, r * TQ:(r + 1) * TQ]
    jn = lax.broadcasted_iota(jnp.int32, (n_sel, n_cmp_pad), 0) * SEL_LEN
    cn = lax.broadcasted_iota(jnp.int32, (n_sel, n_cmp_pad), 1) * CMP_STRIDE
    ov = jnp.minimum(cn + (CMP_LEN - 1), jn + (SEL_LEN - 1)) - jnp.maximum(cn, jn) + 1
    ov_t = (jnp.maximum(ov, 0).astype(F32) / CMP_LEN).astype(BF16)
    p_hi = p_sum.astype(BF16)
    p_lo = (p_sum - p_hi.astype(F32)).astype(BF16)
    imp = _dot(ov_t, p_hi) + _dot(ov_t, p_lo)

    jb = lax.broadcasted_iota(jnp.int32, (n_sel, TQ), 0)
    tq = t0 + lax.broadcasted_iota(jnp.int32, (n_sel, TQ), 1)
    cur = tq // SEL_LEN
    forced = (jb == 0) | (jb == cur) | (jb == cur - 1)
    ok = jb * SEL_LEN <= tq
    score = jnp.where(ok, jnp.where(forced, FORCE_SCORE, imp), NEG_INF)
    sc_ref[...] = score

    rank = jnp.zeros((n_sel, TQ), jnp.int32)
    for j2 in range(n_sel):
        other = sc_ref[j2:j2 + 1, :]
        before = (other > score) | ((other == score) & (jb > j2))
        rank = rank + jnp.where(before, 1, 0)
    n_top = min(SEL_TOPK, n_sel)
    bias_t = jnp.where(rank < n_top, 0.0, SEL_BIAS).astype(BF16)
    for r in range(R):
        qxt_ref[HEAD_DIM:HEAD_DIM + n_sel, r * TQ:(r + 1) * TQ] = bias_t

    key_sub = lax.broadcasted_iota(jnp.int32, (TQ, TQ), 0)
    q_lane = lax.broadcasted_iota(jnp.int32, (TQ, TQ), 1)
    causal = key_sub <= q_lane
    qx_rows = 2 * HEAD_DIM

    diag = pl.ds(pl.multiple_of(t0, TQ), TQ)
    slabs = [slice(r * TQ, (r + 1) * TQ) for r in range(R)]

    def sel_scores(k_tile):
        return [_dot(k_tile, qxt_ref[:, rs]) for rs in slabs]

    def sel_pv(v_t):
        return [_dot(v_t, p_ref[:, rs]) for rs in slabs]

    n_full = qi
    s_diag = sel_scores(ksx_ref[diag, :])
    s_first = sel_scores(ksx_ref[0:TQ, :])
    for rs, s_t in zip(slabs, s_diag):
        s_t = jnp.where(causal, s_t, NEG_INF)
        m_tile = jnp.max(s_t, axis=0, keepdims=True)
        p = jnp.exp2(s_t - m_tile)
        m_ref[:, rs] = m_tile
        l_ref[:, rs] = jnp.sum(p, axis=0, keepdims=True)
        p_ref[:, rs] = p.astype(BF16)
    alpha_ref[...] = jnp.ones((1, M), F32)
    acc_ref[...] = jnp.zeros((HEAD_DIM, M), F32)
    for rs, s_t in zip(slabs, s_first):
        s_ref[:, rs] = s_t

    def sel_body(kt, carry):
        alpha_prev = alpha_ref[...]
        prev = jnp.where(kt == 0, qi, kt - 1)
        pv_prev = sel_pv(vst_ref[prev])
        nxt = jnp.minimum(kt + 1, n_full - 1)
        s_next = sel_scores(ksx_ref[pl.ds(pl.multiple_of(nxt * TQ, TQ), TQ), :])
        for rs in slabs:
            s_t = s_ref[:, rs]
            m_prev = m_ref[:, rs]
            m_new = jnp.maximum(m_prev, jnp.max(s_t, axis=0, keepdims=True))
            alpha = jnp.exp2(m_prev - m_new)
            p = jnp.exp2(s_t - m_new)
            m_ref[:, rs] = m_new
            l_ref[:, rs] = alpha * l_ref[:, rs] + jnp.sum(p, axis=0, keepdims=True)
            alpha_ref[:, rs] = alpha
            p_ref[:, rs] = p.astype(BF16)
        for rs, o in zip(slabs, pv_prev):
            acc_ref[:, rs] = alpha_prev[:, rs] * acc_ref[:, rs] + o
        for rs, s_t in zip(slabs, s_next):
            s_ref[:, rs] = s_t
        return carry
    lax.fori_loop(0, n_full, sel_body, 0)

    last = jnp.where(n_full == 0, qi, n_full - 1)
    inv_l = 1.0 / l_ref[...]
    for rs, o in zip(slabs, sel_pv(vst_ref[last])):
        osel_ref[:, rs] = (alpha_ref[:, rs] * acc_ref[:, rs] + o) * inv_l[:, rs]

    assert WINDOW == 2 * TQ
    _flash_tile(kw_ref[diag, :], qxt_ref, HEAD_DIM, causal, vwt_ref[qi], m_ref, l_ref, acc_ref, True)

    @pl.when(qi >= 1)
    def _():
        rows = pl.ds(pl.multiple_of(t0 - TQ, TQ), TQ)
        _flash_tile(kw_ref[rows, :], qxt_ref, HEAD_DIM, None, vwt_ref[qi - 1], m_ref, l_ref, acc_ref, False)

    @pl.when(qi >= 2)
    def _():
        rows = pl.ds(pl.multiple_of(t0 - 2 * TQ, TQ), TQ)
        _flash_tile(kw_ref[rows, :], qxt_ref, HEAD_DIM, key_sub > q_lane, vwt_ref[qi - 2], m_ref, l_ref, acc_ref,
                    False)

    o_win = acc_ref[...] * (1.0 / l_ref[...])

    gates = 1.0 / (1.0 + jnp.exp(-gl_ref[...]))
    gates_t = pltpu.roll(gates, (LANES - 3 * R * g) % LANES, axis=1).T
    for r in range(R):
        rs = slice(r * TQ, (r + 1) * TQ)
        mix_t = (gates_t[3 * r:3 * r + 1, :] * ocmp_ref[:, rs] + gates_t[3 * r + 1:3 * r + 2, :] * osel_ref[:, rs]
                 + gates_t[3 * r + 2:3 * r + 3, :] * o_win[:, rs])
        cs = slice(r * HEAD_DIM, (r + 1) * HEAD_DIM)
        o_ref[:, cs] = (mix_t.T * _silu(zq_ref[:, cs])).astype(o_ref.dtype)


def _nsa(p1, batch, seq, cos_tab, sin_tab, ksx, vs_t, kw, vw_t, kcmp, vcmp_t):
    n = p1.shape[0]
    TQ = Q_TILE
    R = NSA_GROUP
    q_tiles = seq // TQ
    gw = R * HEAD_DIM
    n_cmp_pad = kcmp.shape[2]
    n_sel = seq // SEL_LEN
    tok = lambda b, g, i: b * q_tiles + i
    head_kv = pl.BlockSpec((seq, HEAD_DIM), lambda b, g, i: (b, g))
    head_vt = pl.BlockSpec((None, q_tiles, HEAD_DIM, TQ), lambda b, g, i: (g, b, 0, 0))
    cmp_kv = pl.BlockSpec((None, None, n_cmp_pad, HEAD_DIM), lambda b, g, i: (b, g, 0, 0))
    cmp_vt = pl.BlockSpec((None, None, HEAD_DIM, n_cmp_pad), lambda b, g, i: (b, g, 0, 0))
    return pl.pallas_call(
        _nsa_kernel,
        grid=(batch, NSA_KV_HEADS, q_tiles),
        in_specs=[
            pl.BlockSpec((TQ, gw), lambda b, g, i: (tok(b, g, i), g)),
            pl.BlockSpec((TQ, gw), lambda b, g, i: (tok(b, g, i), NSA_W // gw + g)),
            pl.BlockSpec((TQ, GATE_PAD), lambda b, g, i: (tok(b, g, i), (2 * NSA_W + 2 * MEM_W) // GATE_PAD)),
            pl.BlockSpec((TQ, LANES), lambda b, g, i: (tok(b, g, i), 0)),
            pl.BlockSpec((TQ, LANES), lambda b, g, i: (tok(b, g, i), 0)),
            pl.BlockSpec((seq, 2 * HEAD_DIM), lambda b, g, i: (b, g)),
            head_vt, head_kv, head_vt, cmp_kv, cmp_vt,
        ],
        out_specs=pl.BlockSpec((TQ, gw), lambda b, g, i: (tok(b, g, i), g)),
        out_shape=jax.ShapeDtypeStruct((n, NSA_W), BF16),
        scratch_shapes=[
            pltpu.VMEM((2 * HEAD_DIM, R * TQ), BF16),
            pltpu.VMEM((n_sel, TQ), F32),
            pltpu.VMEM((1, R * TQ), F32),
            pltpu.VMEM((1, R * TQ), F32),
            pltpu.VMEM((HEAD_DIM, R * TQ), F32),
            pltpu.VMEM((HEAD_DIM, R * TQ), F32),
            pltpu.VMEM((HEAD_DIM, R * TQ), F32),
            pltpu.VMEM((TQ, R * TQ), F32),
            pltpu.VMEM((TQ, R * TQ), BF16),
            pltpu.VMEM((1, R * TQ), F32),
        ],
        compiler_params=pltpu.CompilerParams(dimension_semantics=("arbitrary", "arbitrary", "arbitrary"),
                                             vmem_limit_bytes=VMEM_LIMIT),
        name="nsa",
    )(p1, p1, p1, cos_tab, sin_tab, ksx, vs_t, kw, vw_t, kcmp, vcmp_t)


def _tail_kernel(y_ref, qz_ref, mk_ref, mv_ref, h_ref, wout_ref, g_ref, o_ref, ymem_ref, acc_ref):
    qm = qz_ref[:, 0:MEM_W]
    zm = qz_ref[:, MEM_W:2 * MEM_W]
    _memory_attention(qm, zm, mk_ref, mv_ref, ymem_ref, 0)
    for c in range(0, D_MODEL, COL_CHUNK):
        cs = slice(c, c + COL_CHUNK)
        acc_ref[:, cs] = (h_ref[:, cs] + _dot(y_ref[...], wout_ref[0:NSA_W, cs])
                          + _dot(ymem_ref[...], wout_ref[NSA_W:D_MODEL, cs]))
    o_ref[...] = _rms_scale(acc_ref[...], g_ref[...])


def _tail(y_nsa, p1, seq, mem_k, mem_v, h2d, w_out, final_g):
    n = h2d.shape[0]
    T = TOK_TILE
    tiles_per_seq = seq // T
    return pl.pallas_call(
        _tail_kernel,
        grid=(n // T,),
        in_specs=[
            pl.BlockSpec((T, NSA_W), lambda i: (i, 0)),
            pl.BlockSpec((T, 2 * MEM_W), lambda i: (i, 2 * NSA_W // (2 * MEM_W))),
            pl.BlockSpec((None, MEM_LEN, MEM_W), lambda i: (i // tiles_per_seq, 0, 0)),
            pl.BlockSpec((None, MEM_LEN, MEM_W), lambda i: (i // tiles_per_seq, 0, 0)),
            pl.BlockSpec((T, D_MODEL), lambda i: (i, 0)),
            _resident((D_MODEL, D_MODEL)),
            _resident((1, D_MODEL)),
        ],
        out_specs=pl.BlockSpec((T, D_MODEL), lambda i: (i, 0)),
        out_shape=jax.ShapeDtypeStruct((n, D_MODEL), F32),
        scratch_shapes=[pltpu.VMEM((T, MEM_W), BF16), pltpu.VMEM((T, D_MODEL), F32)],
        compiler_params=pltpu.CompilerParams(dimension_semantics=("arbitrary",), vmem_limit_bytes=VMEM_LIMIT),
        name="tail",
    )(y_nsa, p1, mem_k, mem_v, h2d, w_out, final_g.reshape(1, D_MODEL))


def kernel(x, mem, positions, norm_g, mem_norm_g, w_mem_kv, w_out, a_w_in, a_w_pool, a_pool_scale, b_w_in,
           kv_norm_g, w_kv, cmp_pe, cmp_w1, cmp_w2, final_g):
    batch, seq, _ = x.shape
    assert seq % Q_TILE == 0 and seq % TOK_TILE == 0 and Q_TILE % SEL_LEN == 0
    n = batch * seq
    x2d = x.reshape(n, D_MODEL)

    mkv = _mem_kv(mem.reshape(batch * MEM_LEN, D_MODEL), mem_norm_g, w_mem_kv.astype(BF16))
    mkv = mkv.reshape(mkv.shape[0], batch, MEM_LEN, 2 * MEM_W)
    mem_k, mem_v = mkv[..., :MEM_W], mkv[..., MEM_W:]

    h1 = _layer0(x2d, seq, norm_g[0], a_w_in[0].astype(BF16), a_w_pool[0].astype(BF16), a_pool_scale[0],
                 mem_k[0], mem_v[0], w_out[0].astype(BF16))

    wb = b_w_in[0]
    o1 = NSA_W
    o2 = o1 + 3 * NSA_HEADS
    o3 = o2 + NSA_W
    wb = jnp.concatenate([wb[:, :o1], wb[:, o2:o3], wb[:, o3:], wb[:, o1:o2],
                          jnp.zeros((D_MODEL, GATE_PAD - 3 * NSA_HEADS), wb.dtype)], axis=1).astype(BF16)
    kvp, p1 = _proj1(h1, kv_norm_g, norm_g[1], w_kv.astype(BF16), wb)

    half = jnp.arange(ROT_HALF, dtype=F32)
    inv = ROPE_THETA ** (-half * 2.0 / ROT_DIM)
    pad = jnp.zeros((LANES - ROT_DIM,), F32)
    inv_full = jnp.concatenate([inv, inv, pad]).reshape(1, LANES)
    sgn_full = jnp.concatenate([-jnp.ones((ROT_HALF,), F32), jnp.ones((ROT_HALF,), F32), pad]).reshape(1, LANES)
    ksx, vs, kw, vw, cos_tab, sin_tab = _kv_rot(kvp, seq, positions.reshape(n, 1), inv_full, sgn_full)
    kcmp, vcmp = _compress(kvp, batch, seq, cos_tab, sin_tab, cmp_pe, cmp_w1.astype(BF16), cmp_w2.astype(BF16))

    y_nsa = _nsa(p1, batch, seq, cos_tab, sin_tab, ksx, vs, kw, vw, kcmp, vcmp)
    out = _tail(y_nsa, p1, seq, mem_k[1], mem_v[1], h1, w_out[1].astype(BF16), final_g)
    return out.reshape(batch, seq, D_MODEL)
```

```python
import functools

import jax
import jax.numpy as jnp
from jax import lax
from jax.experimental import pallas as pl
from jax.experimental.pallas import tpu as pltpu

D_MODEL = 2048
MEM_LEN = 256
HEAD_DIM = 128
MEM_HEADS = 4
MEM_W = MEM_HEADS * HEAD_DIM
POOL_W = D_MODEL - MEM_W
POOL_WINDOWS = (2, 4, 8, 16)
POOL_GC = POOL_W // len(POOL_WINDOWS)
POOL_HALO = 16
NSA_W = D_MODEL - MEM_W
NSA_HEADS = NSA_W // HEAD_DIM
NSA_KV_HEADS = 4
NSA_GROUP = NSA_HEADS // NSA_KV_HEADS
NSA_KV_W = NSA_KV_HEADS * HEAD_DIM
CMP_LEN = 32
CMP_STRIDE = 16
CMP_HID = 256
SEL_LEN = 64
SEL_TOPK = 16
WINDOW = 512
ROT_DIM = HEAD_DIM // 4
ROT_HALF = ROT_DIM // 2
ROPE_THETA = 500000.0
NORM_EPS = 1e-6
FORCE_SCORE = 1e4
NEG_INF = -1e30
SEL_BIAS = -1e9
QK_SCALE = HEAD_DIM ** -0.5
LOG2E = 1.4426950408889634

LANES = 128
SUBLANES = 8
TOK_TILE = 256
Q_TILE = 256
COL_CHUNK = 512
GATE_PAD = LANES
VMEM_LIMIT = 56 * 1024 * 1024

F32 = jnp.float32
BF16 = jnp.bfloat16


def _resident(shape):
    nd = len(shape)
    return pl.BlockSpec(shape, lambda *_: (0,) * nd, pipeline_mode=pl.Buffered(1))


def _rms_scale(x, g):
    ms = jnp.mean(x * x, axis=-1, keepdims=True)
    return x * lax.rsqrt(ms + NORM_EPS) * g


def _silu(z):
    return z * (1.0 / (1.0 + jnp.exp(-z)))


def _dot(a, b):
    return jnp.dot(a, b, preferred_element_type=F32)


def _dot_nt(a, b):
    return lax.dot_general(a, b, (((1,), (1,)), ((), ())), preferred_element_type=F32)


def _memory_attention(qm, zm, k_ref, v_ref, y_ref, col0):
    for h in range(MEM_HEADS):
        cs = slice(h * HEAD_DIM, (h + 1) * HEAD_DIM)
        q = (qm[:, cs] * QK_SCALE).astype(BF16)
        s = _dot_nt(q, k_ref[:, cs])
        e = jnp.exp(s - jnp.max(s, axis=-1, keepdims=True))
        p = e / jnp.sum(e, axis=-1, keepdims=True)
        o = _dot(p.astype(BF16), v_ref[:, cs])
        y_ref[:, col0 + h * HEAD_DIM:col0 + (h + 1) * HEAD_DIM] = (o * _silu(zm[:, cs])).astype(y_ref.dtype)


def _mem_kv_kernel(mem_ref, g_ref, w_ref, o_ref):
    hn = _rms_scale(mem_ref[...], g_ref[...]).astype(BF16)
    o_ref[...] = _dot(hn, w_ref[...]).astype(o_ref.dtype)


def _mem_kv(mem2d, mem_norm_g, w_mem_kv_bf16):
    depth = w_mem_kv_bf16.shape[0]
    rows = mem2d.shape[0]
    T = TOK_TILE
    return pl.pallas_call(
        _mem_kv_kernel,
        grid=(depth, rows // T),
        in_specs=[
            pl.BlockSpec((T, D_MODEL), lambda l, i: (i, 0)),
            pl.BlockSpec((None, 1, D_MODEL), lambda l, i: (l, 0, 0)),
            pl.BlockSpec((None, D_MODEL, 2 * MEM_W), lambda l, i: (l, 0, 0)),
        ],
        out_specs=pl.BlockSpec((None, T, 2 * MEM_W), lambda l, i: (l, i, 0)),
        out_shape=jax.ShapeDtypeStruct((depth, rows, 2 * MEM_W), BF16),
        compiler_params=pltpu.CompilerParams(dimension_semantics=("arbitrary", "arbitrary"),
                                             vmem_limit_bytes=VMEM_LIMIT),
        name="mem_kv",
    )(mem2d, mem_norm_g.reshape(depth, 1, D_MODEL), w_mem_kv_bf16)


def _layer0_kernel(tiles_per_seq, x_ref, g_ref, win_ref, wpool_ref, scale_ref, mk_ref, mv_ref, wout_ref,
                   o_ref, hn_ref, proj_ref, uext_ref, y_ref):
    T = x_ref.shape[0]
    tb = pl.program_id(0) % tiles_per_seq
    x = x_ref[...]
    hn_ref[...] = _rms_scale(x, g_ref[...]).astype(BF16)

    @pl.when(tb == 0)
    def _():
        uext_ref[0:POOL_HALO, :] = jnp.zeros((POOL_HALO, POOL_W), F32)

    @pl.when(tb != 0)
    def _():
        uext_ref[0:POOL_HALO, :] = uext_ref[T:T + POOL_HALO, :]

    n_in = win_ref.shape[1]
    for c in range(0, n_in, COL_CHUNK):
        blk = _dot(hn_ref[...], win_ref[:, c:c + COL_CHUNK])
        if c < POOL_W:
            uext_ref[POOL_HALO:POOL_HALO + T, c:c + COL_CHUNK] = blk
        else:
            proj_ref[:, c - POOL_W:c - POOL_W + COL_CHUNK] = blk

    t1 = (tb * T + lax.broadcasted_iota(jnp.int32, (T, 1), 0) + 1).astype(F32)
    for gi, win in enumerate(POOL_WINDOWS):
        cs = slice(gi * POOL_GC, (gi + 1) * POOL_GC)
        u = uext_ref[POOL_HALO:POOL_HALO + T, cs]
        acc = u
        for k in range(1, win):
            acc = acc + uext_ref[POOL_HALO - k:POOL_HALO - k + T, cs]
        pooled = acc / jnp.minimum(t1, float(win)) - u
        mixed = _dot(pooled.astype(BF16), wpool_ref[gi])
        y_ref[:, cs] = (mixed * scale_ref[:, cs] * _silu(proj_ref[:, cs])).astype(BF16)

    qm = proj_ref[:, POOL_W:POOL_W + MEM_W]
    zm = proj_ref[:, POOL_W + MEM_W:POOL_W + 2 * MEM_W]
    _memory_attention(qm, zm, mk_ref, mv_ref, y_ref, POOL_W)

    for c in range(0, D_MODEL, COL_CHUNK):
        o_ref[:, c:c + COL_CHUNK] = x_ref[:, c:c + COL_CHUNK] + _dot(y_ref[...], wout_ref[:, c:c + COL_CHUNK])


def _layer0(x2d, seq, norm_g, w_in, w_pool, pool_scale, mem_k, mem_v, w_out):
    n = x2d.shape[0]
    T = TOK_TILE
    tiles_per_seq = seq // T
    n_in = w_in.shape[1]
    return pl.pallas_call(
        functools.partial(_layer0_kernel, tiles_per_seq),
        grid=(n // T,),
        in_specs=[
            pl.BlockSpec((T, D_MODEL), lambda i: (i, 0)),
            _resident((1, D_MODEL)),
            _resident((D_MODEL, n_in)),
            _resident(w_pool.shape),
            _resident((1, POOL_W)),
            pl.BlockSpec((None, MEM_LEN, MEM_W), lambda i: (i // tiles_per_seq, 0, 0)),
            pl.BlockSpec((None, MEM_LEN, MEM_W), lambda i: (i // tiles_per_seq, 0, 0)),
            _resident((D_MODEL, D_MODEL)),
        ],
        out_specs=pl.BlockSpec((T, D_MODEL), lambda i: (i, 0)),
        out_shape=jax.ShapeDtypeStruct((n, D_MODEL), F32),
        scratch_shapes=[
            pltpu.VMEM((T, D_MODEL), BF16),
            pltpu.VMEM((T, n_in - POOL_W), F32),
            pltpu.VMEM((T + POOL_HALO, POOL_W), F32),
            pltpu.VMEM((T, D_MODEL), BF16),
        ],
        compiler_params=pltpu.CompilerParams(dimension_semantics=("arbitrary",), vmem_limit_bytes=VMEM_LIMIT),
        name="layer0",
    )(x2d, norm_g.reshape(1, D_MODEL), w_in, w_pool, pool_scale.reshape(1, POOL_W), mem_k, mem_v, w_out)


def _proj1_kernel(h_ref, gkv_ref, g1_ref, wkv_ref, w1_ref, kv_ref, p1_ref, hn_ref):
    h = h_ref[...]
    ms = jnp.mean(h * h, axis=-1, keepdims=True)
    hs = h * lax.rsqrt(ms + NORM_EPS)
    hn_ref[...] = (hs * gkv_ref[...]).astype(BF16)
    n_kv = wkv_ref.shape[1]
    for c in range(0, n_kv, COL_CHUNK):
        kv_ref[:, c:c + COL_CHUNK] = _dot(hn_ref[...], wkv_ref[:, c:c + COL_CHUNK])
    hn_ref[...] = (hs * g1_ref[...]).astype(BF16)
    n_1 = w1_ref.shape[1]
    for c in range(0, n_1, COL_CHUNK):
        w = min(COL_CHUNK, n_1 - c)
        p1_ref[:, c:c + w] = _dot(hn_ref[...], w1_ref[:, c:c + w])


def _proj1(h2d, kv_norm_g, norm_g1, w_kv, w_1):
    n = h2d.shape[0]
    T = TOK_TILE
    n_kv, n_1 = w_kv.shape[1], w_1.shape[1]
    return pl.pallas_call(
        _proj1_kernel,
        grid=(n // T,),
        in_specs=[
            pl.BlockSpec((T, D_MODEL), lambda i: (i, 0)),
            _resident((1, D_MODEL)),
            _resident((1, D_MODEL)),
            _resident((D_MODEL, n_kv)),
            _resident((D_MODEL, n_1)),
        ],
        out_specs=[
            pl.BlockSpec((T, n_kv), lambda i: (i, 0)),
            pl.BlockSpec((T, n_1), lambda i: (i, 0)),
        ],
        out_shape=[
            jax.ShapeDtypeStruct((n, n_kv), F32),
            jax.ShapeDtypeStruct((n, n_1), F32),
        ],
        scratch_shapes=[pltpu.VMEM((T, D_MODEL), BF16)],
        compiler_params=pltpu.CompilerParams(dimension_semantics=("arbitrary",), vmem_limit_bytes=VMEM_LIMIT),
        name="proj1",
    )(h2d, kv_norm_g.reshape(1, D_MODEL), norm_g1.reshape(1, D_MODEL), w_kv, w_1)


def _rope(x, cos_t, sin_t):
    lane = lax.broadcasted_iota(jnp.int32, x.shape, 1)
    swapped = jnp.where(lane < ROT_HALF, pltpu.roll(x, LANES - ROT_HALF, axis=1), pltpu.roll(x, ROT_HALF, axis=1))
    return x * cos_t + swapped * sin_t


KV_ROW_TILE = 512


def _kv_rot_kernel(seq, ksv_ref, kwv_ref, pos_ref, inv_ref, sgn_ref, ksx_ref, vso_ref, kwo_ref, vwo_ref, cos_ref,
                   sin_ref):
    rows = ksv_ref.shape[0]
    ang = pos_ref[...].astype(F32) * inv_ref[...]
    c = jnp.cos(ang)
    s = jnp.sin(ang) * sgn_ref[...]
    cos_ref[...] = c
    sin_ref[...] = s
    r0 = (pl.program_id(0) * rows) % seq
    blk = (r0 + lax.broadcasted_iota(jnp.int32, (rows, LANES), 0)) // SEL_LEN
    lane = lax.broadcasted_iota(jnp.int32, (rows, LANES), 1)
    ind = jnp.where(blk == lane, 1.0, 0.0).astype(BF16)
    for h in range(NSA_KV_HEADS):
        hs = slice(h * HEAD_DIM, (h + 1) * HEAD_DIM)
        vsl = slice(NSA_KV_W + h * HEAD_DIM, NSA_KV_W + (h + 1) * HEAD_DIM)
        ksx_ref[:, 2 * h * HEAD_DIM:(2 * h + 1) * HEAD_DIM] = _rope(ksv_ref[:, hs], c, s).astype(BF16)
        ksx_ref[:, (2 * h + 1) * HEAD_DIM:(2 * h + 2) * HEAD_DIM] = ind
        kwo_ref[:, hs] = _rope(kwv_ref[:, hs], c, s).astype(BF16)
        vs_t = ksv_ref[:, vsl].T.astype(BF16)
        vw_t = kwv_ref[:, vsl].T.astype(BF16)
        for j in range(rows // Q_TILE):
            vso_ref[h, j] = vs_t[:, j * Q_TILE:(j + 1) * Q_TILE]
            vwo_ref[h, j] = vw_t[:, j * Q_TILE:(j + 1) * Q_TILE]


def _kv_rot(kvp, seq, pos_col, inv_full, sgn_full):
    n = kvp.shape[0]
    T = KV_ROW_TILE
    pair_w = 2 * NSA_KV_W
    row = lambda w: pl.BlockSpec((T, w), lambda i: (i, 0))
    vt_shape = (NSA_KV_HEADS, n // Q_TILE, HEAD_DIM, Q_TILE)
    vt_spec = pl.BlockSpec((NSA_KV_HEADS, T // Q_TILE, HEAD_DIM, Q_TILE), lambda i: (0, i, 0, 0))
    return pl.pallas_call(
        functools.partial(_kv_rot_kernel, seq),
        grid=(n // T,),
        in_specs=[
            pl.BlockSpec((T, pair_w), lambda i: (i, 1)),
            pl.BlockSpec((T, pair_w), lambda i: (i, 2)),
            row(1),
            pl.BlockSpec((1, LANES), lambda i: (0, 0)),
            pl.BlockSpec((1, LANES), lambda i: (0, 0)),
        ],
        out_specs=[row(2 * NSA_KV_W), vt_spec, row(NSA_KV_W), vt_spec, row(LANES), row(LANES)],
        out_shape=[
            jax.ShapeDtypeStruct((n, 2 * NSA_KV_W), BF16),
            jax.ShapeDtypeStruct(vt_shape, BF16),
            jax.ShapeDtypeStruct((n, NSA_KV_W), BF16),
            jax.ShapeDtypeStruct(vt_shape, BF16),
            jax.ShapeDtypeStruct((n, LANES), F32),
            jax.ShapeDtypeStruct((n, LANES), F32),
        ],
        compiler_params=pltpu.CompilerParams(dimension_semantics=("arbitrary",), vmem_limit_bytes=VMEM_LIMIT),
        name="kv_rot",
    )(kvp, kvp, pos_col, inv_full, sgn_full)


def _compress_kernel(kc_ref, vc_ref, cos_ref, sin_ref, pe_ref, w1_ref, w2_ref, kcmp_ref, vcmp_ref):
    S = kc_ref.shape[0]
    n_half = S // CMP_STRIDE
    assert CMP_LEN == 2 * CMP_STRIDE
    for part, (src_ref, dst_ref) in enumerate(((kc_ref, kcmp_ref), (vc_ref, vcmp_ref))):
        acc_first = jnp.zeros((n_half, CMP_HID), F32)
        acc_second = jnp.zeros((n_half, CMP_HID), F32)
        for l in range(CMP_STRIDE):
            z = src_ref[pl.ds(l, n_half, stride=CMP_STRIDE), :]
            z1 = (z + pe_ref[part, l:l + 1, :]).astype(BF16)
            acc_first = acc_first + _dot(z1, w1_ref[part, l * HEAD_DIM:(l + 1) * HEAD_DIM, :])
            l2 = l + CMP_STRIDE
            z2 = (z + pe_ref[part, l2:l2 + 1, :]).astype(BF16)
            acc_second = acc_second + _dot(z2, w1_ref[part, l2 * HEAD_DIM:(l2 + 1) * HEAD_DIM, :])
        pre = acc_first + pltpu.roll(acc_second, n_half - 1, axis=0)
        out = _dot(_silu(pre).astype(BF16), w2_ref[part])
        if part == 0:
            c_end = pltpu.roll(cos_ref[pl.ds(CMP_STRIDE - 1, n_half, stride=CMP_STRIDE), :], n_half - 1, axis=0)
            s_end = pltpu.roll(sin_ref[pl.ds(CMP_STRIDE - 1, n_half, stride=CMP_STRIDE), :], n_half - 1, axis=0)
            out = _rope(out, c_end, s_end)
            dst_ref[...] = out.astype(BF16)
        else:
            dst_ref[...] = out.T.astype(BF16)


def _compress(kvp, batch, seq, cos_tab, sin_tab, cmp_pe, cmp_w1, cmp_w2):
    n_half = seq // CMP_STRIDE
    cmp_out = pl.BlockSpec((None, None, n_half, HEAD_DIM), lambda b, g: (b, g, 0, 0))
    tab = pl.BlockSpec((seq, LANES), lambda b, g: (b, 0))
    return pl.pallas_call(
        _compress_kernel,
        grid=(batch, NSA_KV_HEADS),
        in_specs=[
            pl.BlockSpec((seq, HEAD_DIM), lambda b, g: (b, g)),
            pl.BlockSpec((seq, HEAD_DIM), lambda b, g: (b, NSA_KV_HEADS + g)),
            tab, tab,
            _resident(cmp_pe.shape), _resident(cmp_w1.shape), _resident(cmp_w2.shape),
        ],
        out_specs=[cmp_out, pl.BlockSpec((None, None, HEAD_DIM, n_half), lambda b, g: (b, g, 0, 0))],
        out_shape=[
            jax.ShapeDtypeStruct((batch, NSA_KV_HEADS, n_half, HEAD_DIM), BF16),
            jax.ShapeDtypeStruct((batch, NSA_KV_HEADS, HEAD_DIM, n_half), BF16),
        ],
        compiler_params=pltpu.CompilerParams(dimension_semantics=("arbitrary", "arbitrary"),
                                             vmem_limit_bytes=VMEM_LIMIT),
        name="compress",
    )(kvp, kvp, cos_tab, sin_tab, cmp_pe, cmp_w1, cmp_w2)


def _flash_tile(k_tile, qt_ref, q_rows, mask, v_t, m_ref, l_ref, acc_ref, first):
    tq = qt_ref.shape[1] // NSA_GROUP
    slabs = [slice(r * tq, (r + 1) * tq) for r in range(NSA_GROUP)]
    scores = [_dot(k_tile, qt_ref[0:q_rows, rs]) for rs in slabs]
    for rs, s_t in zip(slabs, scores):
        if mask is not None:
            s_t = jnp.where(mask, s_t, NEG_INF)
        m_tile = jnp.max(s_t, axis=0, keepdims=True)
        if first:
            p = jnp.exp2(s_t - m_tile)
            m_ref[:, rs] = m_tile
            l_ref[:, rs] = jnp.sum(p, axis=0, keepdims=True)
            acc_ref[:, rs] = _dot(v_t, p.astype(BF16))
        else:
            m_prev = m_ref[:, rs]
            m_new = jnp.maximum(m_prev, m_tile)
            alpha = jnp.exp2(m_prev - m_new)
            p = jnp.exp2(s_t - m_new)
            m_ref[:, rs] = m_new
            l_ref[:, rs] = alpha * l_ref[:, rs] + jnp.sum(p, axis=0, keepdims=True)
            acc_ref[:, rs] = alpha * acc_ref[:, rs] + _dot(v_t, p.astype(BF16))


def _nsa_kernel(q_ref, zq_ref, gl_ref, cos_ref, sin_ref, ksx_ref, vst_ref, kw_ref, vwt_ref, kcmp_ref, vcmpt_ref,
                o_ref, qxt_ref, sc_ref, m_ref, l_ref, acc_ref, ocmp_ref, osel_ref, s_ref, p_ref, alpha_ref,
                rank_ref):
    TQ = q_ref.shape[0]
    R = NSA_GROUP
    M = R * TQ
    n_cmp_pad = kcmp_ref.shape[0]
    n_sel = ksx_ref.shape[0] // SEL_LEN
    g = pl.program_id(1)
    qi = pl.program_id(2)
    t0 = qi * TQ

    cos_t = cos_ref[...]
    sin_t = sin_ref[...]
    for r in range(R):
        qr = _rope(q_ref[:, r * HEAD_DIM:(r + 1) * HEAD_DIM], cos_t, sin_t) * (QK_SCALE * LOG2E)
        qxt_ref[0:HEAD_DIM, r * TQ:(r + 1) * TQ] = qr.T.astype(BF16)
    qxt_ref[HEAD_DIM + n_sel:2 * HEAD_DIM, :] = jnp.zeros((HEAD_DIM - n_sel, M), BF16)

    lane_q = lax.broadcasted_iota(jnp.int32, (1, M), 1) % TQ
    q_t = qxt_ref[0:HEAD_DIM, :]

    s_t = _dot(kcmp_ref[...], q_t)
    cmp_end = lax.broadcasted_iota(jnp.int32, (n_cmp_pad, 1), 0) * CMP_STRIDE + (CMP_LEN - 1)
    valid = cmp_end <= t0 + lane_q
    s_t = jnp.where(valid, s_t, NEG_INF)
    e = jnp.where(valid, jnp.exp2(s_t - jnp.max(s_t, axis=0, keepdims=True)), 0.0)
    p_cmp = e * (1.0 / jnp.maximum(jnp.sum(e, axis=0, keepdims=True), 1e-30))
    ocmp_ref[...] = _dot(vcmpt_ref[...], p_cmp.astype(BF16))

    p_sum = p_cmp[:, 0:TQ]
    for r in range(1, R):
        p_sum = p_sum + p_cmp[:, r * TQ:(r + 1) * TQ]
    jn = lax.broadcasted_iota(jnp.int32, (n_sel, n_cmp_pad), 0) * SEL_LEN
    cn = lax.broadcasted_iota(jnp.int32, (n_sel, n_cmp_pad), 1) * CMP_STRIDE
    ov = jnp.minimum(cn + (CMP_LEN - 1), jn + (SEL_LEN - 1)) - jnp.maximum(cn, jn) + 1
    ov_t = (jnp.maximum(ov, 0).astype(F32) / CMP_LEN).astype(BF16)
    p_hi = p_sum.astype(BF16)
    p_lo = (p_sum - p_hi.astype(F32)).astype(BF16)
    imp = _dot(ov_t, p_hi) + _dot(ov_t, p_lo)

    jb = lax.broadcasted_iota(jnp.int32, (n_sel, TQ), 0)
    tq = t0 + lax.broadcasted_iota(jnp.int32, (n_sel, TQ), 1)
    cur = tq // SEL_LEN
    forced = (jb == 0) | (jb == cur) | (jb == cur - 1)
    ok = jb * SEL_LEN <= tq
    score = jnp.where(ok, jnp.where(forced, FORCE_SCORE, imp), NEG_INF)
    sc_ref[...] = score

    rank_ref[...] = jnp.zeros((n_sel, TQ), F32)
    blocks_per_tile = TQ // SEL_LEN
    for kt in range(n_sel // blocks_per_tile):
        @pl.when(kt <= qi)
        def _(kt=kt):
            others = [sc_ref[j2:j2 + 1, :] for j2 in range(kt * blocks_per_tile, (kt + 1) * blocks_per_tile)]
            for gi in range(n_sel // SUBLANES):
                rows = slice(gi * SUBLANES, (gi + 1) * SUBLANES)
                sg = sc_ref[rows, :]
                cnt = rank_ref[rows, :]
                for u, other in enumerate(others):
                    j2 = kt * blocks_per_tile + u
                    if gi * SUBLANES > j2:
                        before = other >= sg
                    elif (gi + 1) * SUBLANES - 1 < j2:
                        before = other > sg
                    else:
                        row_j = gi * SUBLANES + lax.broadcasted_iota(jnp.int32, (SUBLANES, TQ), 0)
                        before = (other > sg) | ((other == sg) & (row_j > j2))
                    cnt = cnt + jnp.where(before, 1.0, 0.0)
                rank_ref[rows, :] = cnt
    n_top = min(SEL_TOPK, n_sel)
    bias_t = jnp.where(rank_ref[...] < n_top, 0.0, SEL_BIAS).astype(BF16)
    for r in range(R):
        qxt_ref[HEAD_DIM:HEAD_DIM + n_sel, r * TQ:(r + 1) * TQ] = bias_t

    key_sub = lax.broadcasted_iota(jnp.int32, (TQ, TQ), 0)
    q_lane = lax.broadcasted_iota(jnp.int32, (TQ, TQ), 1)
    causal = key_sub <= q_lane
    qx_rows = 2 * HEAD_DIM

    diag = pl.ds(pl.multiple_of(t0, TQ), TQ)
    slabs = [slice(r * TQ, (r + 1) * TQ) for r in range(R)]

    def sel_scores(k_tile):
        return [_dot(k_tile, qxt_ref[:, rs]) for rs in slabs]

    def sel_pv(v_t):
        return [_dot(v_t, p_ref[:, rs]) for rs in slabs]

    n_full = qi
    s_diag = sel_scores(ksx_ref[diag, :])
    s_first = sel_scores(ksx_ref[0:TQ, :])
    for rs, s_t in zip(slabs, s_diag):
        s_t = jnp.where(causal, s_t, NEG_INF)
        m_tile = jnp.max(s_t, axis=0, keepdims=True)
        p = jnp.exp2(s_t - m_tile)
        m_ref[:, rs] = m_tile
        l_ref[:, rs] = jnp.sum(p, axis=0, keepdims=True)
        p_ref[:, rs] = p.astype(BF16)
    alpha_ref[...] = jnp.ones((1, M), F32)
    acc_ref[...] = jnp.zeros((HEAD_DIM, M), F32)
    for rs, s_t in zip(slabs, s_first):
        s_ref[:, rs] = s_t

    def sel_body(kt, carry):
        alpha_prev = alpha_ref[...]
        prev = jnp.where(kt == 0, qi, kt - 1)
        pv_prev = sel_pv(vst_ref[prev])
        nxt = jnp.minimum(kt + 1, n_full - 1)
        s_next = sel_scores(ksx_ref[pl.ds(pl.multiple_of(nxt * TQ, TQ), TQ), :])
        for rs in slabs:
            s_t = s_ref[:, rs]
            m_prev = m_ref[:, rs]
            m_new = jnp.maximum(m_prev, jnp.max(s_t, axis=0, keepdims=True))
            alpha = jnp.exp2(m_prev - m_new)
            p = jnp.exp2(s_t - m_new)
            m_ref[:, rs] = m_new
            l_ref[:, rs] = alpha * l_ref[:, rs] + jnp.sum(p, axis=0, keepdims=True)
            alpha_ref[:, rs] = alpha
            p_ref[:, rs] = p.astype(BF16)
        for rs, o in zip(slabs, pv_prev):
            acc_ref[:, rs] = alpha_prev[:, rs] * acc_ref[:, rs] + o
        for rs, s_t in zip(slabs, s_next):
            s_ref[:, rs] = s_t
        return carry
    lax.fori_loop(0, n_full, sel_body, 0)

    last = jnp.where(n_full == 0, qi, n_full - 1)
    inv_l = 1.0 / l_ref[...]
    for rs, o in zip(slabs, sel_pv(vst_ref[last])):
        osel_ref[:, rs] = (alpha_ref[:, rs] * acc_ref[:, rs] + o) * inv_l[:, rs]

    assert WINDOW == 2 * TQ

    @pl.when(qi >= 2)
    def _():
        tiles = (qi - 2, qi - 1, qi)
        masks = (key_sub > q_lane, None, causal)
        k_tiles = [kw_ref[pl.ds(pl.multiple_of(kt * TQ, TQ), TQ), :] for kt in tiles]
        scores = [[_dot(k, qxt_ref[0:HEAD_DIM, rs]) for k in k_tiles] for rs in slabs]
        probs = []
        for rs, s_head in zip(slabs, scores):
            s_head = [s if mk is None else jnp.where(mk, s, NEG_INF) for s, mk in zip(s_head, masks)]
            m_col = jnp.max(s_head[0], axis=0, keepdims=True)
            for s in s_head[1:]:
                m_col = jnp.maximum(m_col, jnp.max(s, axis=0, keepdims=True))
            p_head = [jnp.exp2(s - m_col) for s in s_head]
            l_col = jnp.sum(p_head[0], axis=0, keepdims=True)
            for p in p_head[1:]:
                l_col = l_col + jnp.sum(p, axis=0, keepdims=True)
            l_ref[:, rs] = l_col
            probs.append([p.astype(BF16) for p in p_head])
        for rs, p_head in zip(slabs, probs):
            o = _dot(vwt_ref[tiles[0]], p_head[0])
            for kt, p in zip(tiles[1:], p_head[1:]):
                o = o + _dot(vwt_ref[kt], p)
            acc_ref[:, rs] = o * (1.0 / l_ref[:, rs])

    @pl.when(qi < 2)
    def _():
        _flash_tile(kw_ref[diag, :], qxt_ref, HEAD_DIM, causal, vwt_ref[qi], m_ref, l_ref, acc_ref, True)

        @pl.when(qi >= 1)
        def _():
            rows = pl.ds(pl.multiple_of(t0 - TQ, TQ), TQ)
            _flash_tile(kw_ref[rows, :], qxt_ref, HEAD_DIM, None, vwt_ref[qi - 1], m_ref, l_ref, acc_ref, False)

        acc_ref[...] = acc_ref[...] * (1.0 / l_ref[...])

    o_win = acc_ref[...]

    gates = 1.0 / (1.0 + jnp.exp(-gl_ref[...]))
    gates_t = pltpu.roll(gates, (LANES - 3 * R * g) % LANES, axis=1).T
    for r in range(R):
        rs = slice(r * TQ, (r + 1) * TQ)
        mix_t = (gates_t[3 * r:3 * r + 1, :] * ocmp_ref[:, rs] + gates_t[3 * r + 1:3 * r + 2, :] * osel_ref[:, rs]
                 + gates_t[3 * r + 2:3 * r + 3, :] * o_win[:, rs])
        cs = slice(r * HEAD_DIM, (r + 1) * HEAD_DIM)
        o_ref[:, cs] = (mix_t.T * _silu(zq_ref[:, cs])).astype(o_ref.dtype)


def _nsa(p1, batch, seq, cos_tab, sin_tab, ksx, vs_t, kw, vw_t, kcmp, vcmp_t):
    n = p1.shape[0]
    TQ = Q_TILE
    R = NSA_GROUP
    q_tiles = seq // TQ
    gw = R * HEAD_DIM
    n_cmp_pad = kcmp.shape[2]
    n_sel = seq // SEL_LEN
    tok = lambda b, g, i: b * q_tiles + i
    head_kv = pl.BlockSpec((seq, HEAD_DIM), lambda b, g, i: (b, g))
    head_vt = pl.BlockSpec((None, q_tiles, HEAD_DIM, TQ), lambda b, g, i: (g, b, 0, 0))
    cmp_kv = pl.BlockSpec((None, None, n_cmp_pad, HEAD_DIM), lambda b, g, i: (b, g, 0, 0))
    cmp_vt = pl.BlockSpec((None, None, HEAD_DIM, n_cmp_pad), lambda b, g, i: (b, g, 0, 0))
    return pl.pallas_call(
        _nsa_kernel,
        grid=(batch, NSA_KV_HEADS, q_tiles),
        in_specs=[
            pl.BlockSpec((TQ, gw), lambda b, g, i: (tok(b, g, i), g)),
            pl.BlockSpec((TQ, gw), lambda b, g, i: (tok(b, g, i), NSA_W // gw + g)),
            pl.BlockSpec((TQ, GATE_PAD), lambda b, g, i: (tok(b, g, i), (2 * NSA_W + 2 * MEM_W) // GATE_PAD)),
            pl.BlockSpec((TQ, LANES), lambda b, g, i: (tok(b, g, i), 0)),
            pl.BlockSpec((TQ, LANES), lambda b, g, i: (tok(b, g, i), 0)),
            pl.BlockSpec((seq, 2 * HEAD_DIM), lambda b, g, i: (b, g)),
            head_vt, head_kv, head_vt, cmp_kv, cmp_vt,
        ],
        out_specs=pl.BlockSpec((TQ, gw), lambda b, g, i: (tok(b, g, i), g)),
        out_shape=jax.ShapeDtypeStruct((n, NSA_W), BF16),
        scratch_shapes=[
            pltpu.VMEM((2 * HEAD_DIM, R * TQ), BF16),
            pltpu.VMEM((n_sel, TQ), F32),
            pltpu.VMEM((1, R * TQ), F32),
            pltpu.VMEM((1, R * TQ), F32),
            pltpu.VMEM((HEAD_DIM, R * TQ), F32),
            pltpu.VMEM((HEAD_DIM, R * TQ), F32),
            pltpu.VMEM((HEAD_DIM, R * TQ), F32),
            pltpu.VMEM((TQ, R * TQ), F32),
            pltpu.VMEM((TQ, R * TQ), BF16),
            pltpu.VMEM((1, R * TQ), F32),
            pltpu.VMEM((n_sel, TQ), F32),
        ],
        compiler_params=pltpu.CompilerParams(dimension_semantics=("arbitrary", "arbitrary", "arbitrary"),
                                             vmem_limit_bytes=VMEM_LIMIT),
        name="nsa",
    )(p1, p1, p1, cos_tab, sin_tab, ksx, vs_t, kw, vw_t, kcmp, vcmp_t)


def _tail_kernel(y_ref, qz_ref, mk_ref, mv_ref, h_ref, wout_ref, g_ref, o_ref, ymem_ref, acc_ref):
    qm = qz_ref[:, 0:MEM_W]
    zm = qz_ref[:, MEM_W:2 * MEM_W]
    _memory_attention(qm, zm, mk_ref, mv_ref, ymem_ref, 0)
    for c in range(0, D_MODEL, COL_CHUNK):
        cs = slice(c, c + COL_CHUNK)
        acc_ref[:, cs] = (h_ref[:, cs] + _dot(y_ref[...], wout_ref[0:NSA_W, cs])
                          + _dot(ymem_ref[...], wout_ref[NSA_W:D_MODEL, cs]))
    o_ref[...] = _rms_scale(acc_ref[...], g_ref[...])


def _tail(y_nsa, p1, seq, mem_k, mem_v, h2d, w_out, final_g):
    n = h2d.shape[0]
    T = TOK_TILE
    tiles_per_seq = seq // T
    return pl.pallas_call(
        _tail_kernel,
        grid=(n // T,),
        in_specs=[
            pl.BlockSpec((T, NSA_W), lambda i: (i, 0)),
            pl.BlockSpec((T, 2 * MEM_W), lambda i: (i, 2 * NSA_W // (2 * MEM_W))),
            pl.BlockSpec((None, MEM_LEN, MEM_W), lambda i: (i // tiles_per_seq, 0, 0)),
            pl.BlockSpec((None, MEM_LEN, MEM_W), lambda i: (i // tiles_per_seq, 0, 0)),
            pl.BlockSpec((T, D_MODEL), lambda i: (i, 0)),
            _resident((D_MODEL, D_MODEL)),
            _resident((1, D_MODEL)),
        ],
        out_specs=pl.BlockSpec((T, D_MODEL), lambda i: (i, 0)),
        out_shape=jax.ShapeDtypeStruct((n, D_MODEL), F32),
        scratch_shapes=[pltpu.VMEM((T, MEM_W), BF16), pltpu.VMEM((T, D_MODEL), F32)],
        compiler_params=pltpu.CompilerParams(dimension_semantics=("arbitrary",), vmem_limit_bytes=VMEM_LIMIT),
        name="tail",
    )(y_nsa, p1, mem_k, mem_v, h2d, w_out, final_g.reshape(1, D_MODEL))


def kernel(x, mem, positions, norm_g, mem_norm_g, w_mem_kv, w_out, a_w_in, a_w_pool, a_pool_scale, b_w_in,
           kv_norm_g, w_kv, cmp_pe, cmp_w1, cmp_w2, final_g):
    batch, seq, _ = x.shape
    assert seq % Q_TILE == 0 and seq % TOK_TILE == 0 and Q_TILE % SEL_LEN == 0
    n = batch * seq
    x2d = x.reshape(n, D_MODEL)

    mkv = _mem_kv(mem.reshape(batch * MEM_LEN, D_MODEL), mem_norm_g, w_mem_kv.astype(BF16))
    mkv = mkv.reshape(mkv.shape[0], batch, MEM_LEN, 2 * MEM_W)
    mem_k, mem_v = mkv[..., :MEM_W], mkv[..., MEM_W:]

    h1 = _layer0(x2d, seq, norm_g[0], a_w_in[0].astype(BF16), a_w_pool[0].astype(BF16), a_pool_scale[0],
                 mem_k[0], mem_v[0], w_out[0].astype(BF16))

    wb = b_w_in[0]
    o1 = NSA_W
    o2 = o1 + 3 * NSA_HEADS
    o3 = o2 + NSA_W
    wb = jnp.concatenate([wb[:, :o1], wb[:, o2:o3], wb[:, o3:], wb[:, o1:o2],
                          jnp.zeros((D_MODEL, GATE_PAD - 3 * NSA_HEADS), wb.dtype)], axis=1).astype(BF16)
    kvp, p1 = _proj1(h1, kv_norm_g, norm_g[1], w_kv.astype(BF16), wb)

    half = jnp.arange(ROT_HALF, dtype=F32)
    inv = ROPE_THETA ** (-half * 2.0 / ROT_DIM)
    pad = jnp.zeros((LANES - ROT_DIM,), F32)
    inv_full = jnp.concatenate([inv, inv, pad]).reshape(1, LANES)
    sgn_full = jnp.concatenate([-jnp.ones((ROT_HALF,), F32), jnp.ones((ROT_HALF,), F32), pad]).reshape(1, LANES)
    ksx, vs, kw, vw, cos_tab, sin_tab = _kv_rot(kvp, seq, positions.reshape(n, 1), inv_full, sgn_full)
    kcmp, vcmp = _compress(kvp, batch, seq, cos_tab, sin_tab, cmp_pe, cmp_w1.astype(BF16), cmp_w2.astype(BF16))

    y_nsa = _nsa(p1, batch, seq, cos_tab, sin_tab, ksx, vs, kw, vw, kcmp, vcmp)
    out = _tail(y_nsa, p1, seq, mem_k[1], mem_v[1], h1, w_out[1].astype(BF16), final_g)
    return out.reshape(batch, seq, D_MODEL)
```

```python
import functools

import jax
import jax.numpy as jnp
from jax import lax
from jax.experimental import pallas as pl
from jax.experimental.pallas import tpu as pltpu

D_MODEL = 2048
MEM_LEN = 256
HEAD_DIM = 128
MEM_HEADS = 4
MEM_W = MEM_HEADS * HEAD_DIM
POOL_W = D_MODEL - MEM_W
POOL_WINDOWS = (2, 4, 8, 16)
POOL_GC = POOL_W // len(POOL_WINDOWS)
POOL_HALO = 16
NSA_W = D_MODEL - MEM_W
NSA_HEADS = NSA_W // HEAD_DIM
NSA_KV_HEADS = 4
NSA_GROUP = NSA_HEADS // NSA_KV_HEADS
NSA_KV_W = NSA_KV_HEADS * HEAD_DIM
CMP_LEN = 32
CMP_STRIDE = 16
CMP_HID = 256
SEL_LEN = 64
SEL_TOPK = 16
WINDOW = 512
ROT_DIM = HEAD_DIM // 4
ROT_HALF = ROT_DIM // 2
ROPE_THETA = 500000.0
NORM_EPS = 1e-6
FORCE_SCORE = 1e4
NEG_INF = -1e30
SEL_BIAS = -1e9
QK_SCALE = HEAD_DIM ** -0.5
LOG2E = 1.4426950408889634

LANES = 128
SUBLANES = 8
TOK_TILE = 256
Q_TILE = 256
COL_CHUNK = 512
GATE_PAD = LANES
VMEM_LIMIT = 56 * 1024 * 1024

F32 = jnp.float32
BF16 = jnp.bfloat16


def _resident(shape):
    nd = len(shape)
    return pl.BlockSpec(shape, lambda *_: (0,) * nd, pipeline_mode=pl.Buffered(1))


def _rms_scale(x, g):
    ms = jnp.mean(x * x, axis=-1, keepdims=True)
    return x * lax.rsqrt(ms + NORM_EPS) * g


def _silu(z):
    return z * (1.0 / (1.0 + jnp.exp(-z)))


def _dot(a, b):
    return jnp.dot(a, b, preferred_element_type=F32)


def _dot_nt(a, b):
    return lax.dot_general(a, b, (((1,), (1,)), ((), ())), preferred_element_type=F32)


def _memory_attention(qm, zm, k_ref, v_ref, y_ref, col0):
    for h in range(MEM_HEADS):
        cs = slice(h * HEAD_DIM, (h + 1) * HEAD_DIM)
        q = (qm[:, cs] * QK_SCALE).astype(BF16)
        s = _dot_nt(q, k_ref[:, cs])
        e = jnp.exp(s - jnp.max(s, axis=-1, keepdims=True))
        p = e / jnp.sum(e, axis=-1, keepdims=True)
        o = _dot(p.astype(BF16), v_ref[:, cs])
        y_ref[:, col0 + h * HEAD_DIM:col0 + (h + 1) * HEAD_DIM] = (o * _silu(zm[:, cs])).astype(y_ref.dtype)


def _mem_kv_kernel(mem_ref, g_ref, w_ref, o_ref):
    hn = _rms_scale(mem_ref[...], g_ref[...]).astype(BF16)
    o_ref[...] = _dot(hn, w_ref[...]).astype(o_ref.dtype)


def _mem_kv(mem2d, mem_norm_g, w_mem_kv_bf16):
    depth = w_mem_kv_bf16.shape[0]
    rows = mem2d.shape[0]
    T = TOK_TILE
    return pl.pallas_call(
        _mem_kv_kernel,
        grid=(depth, rows // T),
        in_specs=[
            pl.BlockSpec((T, D_MODEL), lambda l, i: (i, 0)),
            pl.BlockSpec((None, 1, D_MODEL), lambda l, i: (l, 0, 0)),
            pl.BlockSpec((None, D_MODEL, 2 * MEM_W), lambda l, i: (l, 0, 0)),
        ],
        out_specs=pl.BlockSpec((None, T, 2 * MEM_W), lambda l, i: (l, i, 0)),
        out_shape=jax.ShapeDtypeStruct((depth, rows, 2 * MEM_W), BF16),
        compiler_params=pltpu.CompilerParams(dimension_semantics=("arbitrary", "arbitrary"),
                                             vmem_limit_bytes=VMEM_LIMIT),
        name="mem_kv",
    )(mem2d, mem_norm_g.reshape(depth, 1, D_MODEL), w_mem_kv_bf16)


def _layer0_kernel(tiles_per_seq, x_ref, g_ref, win_ref, wpool_ref, scale_ref, mk_ref, mv_ref, wout_ref,
                   o_ref, hn_ref, proj_ref, uext_ref, y_ref):
    T = x_ref.shape[0]
    tb = pl.program_id(0) % tiles_per_seq
    x = x_ref[...]
    hn_ref[...] = _rms_scale(x, g_ref[...]).astype(BF16)

    @pl.when(tb == 0)
    def _():
        uext_ref[0:POOL_HALO, :] = jnp.zeros((POOL_HALO, POOL_W), F32)

    @pl.when(tb != 0)
    def _():
        uext_ref[0:POOL_HALO, :] = uext_ref[T:T + POOL_HALO, :]

    n_in = win_ref.shape[1]
    for c in range(0, n_in, COL_CHUNK):
        blk = _dot(hn_ref[...], win_ref[:, c:c + COL_CHUNK])
        if c < POOL_W:
            uext_ref[POOL_HALO:POOL_HALO + T, c:c + COL_CHUNK] = blk
        else:
            proj_ref[:, c - POOL_W:c - POOL_W + COL_CHUNK] = blk

    t1 = (tb * T + lax.broadcasted_iota(jnp.int32, (T, 1), 0) + 1).astype(F32)
    for gi, win in enumerate(POOL_WINDOWS):
        cs = slice(gi * POOL_GC, (gi + 1) * POOL_GC)
        u = uext_ref[POOL_HALO:POOL_HALO + T, cs]
        acc = u
        for k in range(1, win):
            acc = acc + uext_ref[POOL_HALO - k:POOL_HALO - k + T, cs]
        pooled = acc / jnp.minimum(t1, float(win)) - u
        mixed = _dot(pooled.astype(BF16), wpool_ref[gi])
        y_ref[:, cs] = (mixed * scale_ref[:, cs] * _silu(proj_ref[:, cs])).astype(BF16)

    for c in range(0, D_MODEL, COL_CHUNK):
        cs = slice(c, c + COL_CHUNK)
        o_ref[:, cs] = x_ref[:, cs] + _dot(y_ref[:, 0:POOL_W], wout_ref[0:POOL_W, cs])

    qm = proj_ref[:, POOL_W:POOL_W + MEM_W]
    zm = proj_ref[:, POOL_W + MEM_W:POOL_W + 2 * MEM_W]
    _memory_attention(qm, zm, mk_ref, mv_ref, y_ref, POOL_W)

    for c in range(0, D_MODEL, COL_CHUNK):
        cs = slice(c, c + COL_CHUNK)
        o_ref[:, cs] = o_ref[:, cs] + _dot(y_ref[:, POOL_W:D_MODEL], wout_ref[POOL_W:D_MODEL, cs])


def _layer0(x2d, seq, norm_g, w_in, w_pool, pool_scale, mem_k, mem_v, w_out):
    n = x2d.shape[0]
    T = TOK_TILE
    tiles_per_seq = seq // T
    n_in = w_in.shape[1]
    return pl.pallas_call(
        functools.partial(_layer0_kernel, tiles_per_seq),
        grid=(n // T,),
        in_specs=[
            pl.BlockSpec((T, D_MODEL), lambda i: (i, 0)),
            _resident((1, D_MODEL)),
            _resident((D_MODEL, n_in)),
            _resident(w_pool.shape),
            _resident((1, POOL_W)),
            pl.BlockSpec((None, MEM_LEN, MEM_W), lambda i: (i // tiles_per_seq, 0, 0)),
            pl.BlockSpec((None, MEM_LEN, MEM_W), lambda i: (i // tiles_per_seq, 0, 0)),
            _resident((D_MODEL, D_MODEL)),
        ],
        out_specs=pl.BlockSpec((T, D_MODEL), lambda i: (i, 0)),
        out_shape=jax.ShapeDtypeStruct((n, D_MODEL), F32),
        scratch_shapes=[
            pltpu.VMEM((T, D_MODEL), BF16),
            pltpu.VMEM((T, n_in - POOL_W), F32),
            pltpu.VMEM((T + POOL_HALO, POOL_W), F32),
            pltpu.VMEM((T, D_MODEL), BF16),
        ],
        compiler_params=pltpu.CompilerParams(dimension_semantics=("arbitrary",), vmem_limit_bytes=VMEM_LIMIT),
        name="layer0",
    )(x2d, norm_g.reshape(1, D_MODEL), w_in, w_pool, pool_scale.reshape(1, POOL_W), mem_k, mem_v, w_out)


def _proj1_kernel(h_ref, gkv_ref, g1_ref, wkv_ref, w1_ref, kv_ref, p1_ref, hn_ref):
    h = h_ref[...]
    ms = jnp.mean(h * h, axis=-1, keepdims=True)
    hs = h * lax.rsqrt(ms + NORM_EPS)
    hn_ref[...] = (hs * gkv_ref[...]).astype(BF16)
    n_kv = wkv_ref.shape[1]
    for c in range(0, n_kv, COL_CHUNK):
        kv_ref[:, c:c + COL_CHUNK] = _dot(hn_ref[...], wkv_ref[:, c:c + COL_CHUNK])
    hn_ref[...] = (hs * g1_ref[...]).astype(BF16)
    n_1 = w1_ref.shape[1]
    for c in range(0, n_1, COL_CHUNK):
        w = min(COL_CHUNK, n_1 - c)
        p1_ref[:, c:c + w] = _dot(hn_ref[...], w1_ref[:, c:c + w])


def _proj1(h2d, kv_norm_g, norm_g1, w_kv, w_1):
    n = h2d.shape[0]
    T = TOK_TILE
    n_kv, n_1 = w_kv.shape[1], w_1.shape[1]
    return pl.pallas_call(
        _proj1_kernel,
        grid=(n // T,),
        in_specs=[
            pl.BlockSpec((T, D_MODEL), lambda i: (i, 0)),
            _resident((1, D_MODEL)),
            _resident((1, D_MODEL)),
            _resident((D_MODEL, n_kv)),
            _resident((D_MODEL, n_1)),
        ],
        out_specs=[
            pl.BlockSpec((T, n_kv), lambda i: (i, 0)),
            pl.BlockSpec((T, n_1), lambda i: (i, 0)),
        ],
        out_shape=[
            jax.ShapeDtypeStruct((n, n_kv), F32),
            jax.ShapeDtypeStruct((n, n_1), F32),
        ],
        scratch_shapes=[pltpu.VMEM((T, D_MODEL), BF16)],
        compiler_params=pltpu.CompilerParams(dimension_semantics=("arbitrary",), vmem_limit_bytes=VMEM_LIMIT),
        name="proj1",
    )(h2d, kv_norm_g.reshape(1, D_MODEL), norm_g1.reshape(1, D_MODEL), w_kv, w_1)


def _rope(x, cos_t, sin_t):
    lane = lax.broadcasted_iota(jnp.int32, x.shape, 1)
    swapped = jnp.where(lane < ROT_HALF, pltpu.roll(x, LANES - ROT_HALF, axis=1), pltpu.roll(x, ROT_HALF, axis=1))
    return x * cos_t + swapped * sin_t


KV_ROW_TILE = 512


def _kv_rot_kernel(seq, ksv_ref, kwv_ref, pos_ref, inv_ref, sgn_ref, ksx_ref, vso_ref, kwo_ref, vwo_ref, cos_ref,
                   sin_ref):
    rows = ksv_ref.shape[0]
    ang = pos_ref[...].astype(F32) * inv_ref[...]
    c = jnp.cos(ang)
    s = jnp.sin(ang) * sgn_ref[...]
    cos_ref[...] = c
    sin_ref[...] = s
    r0 = (pl.program_id(0) * rows) % seq
    blk = (r0 + lax.broadcasted_iota(jnp.int32, (rows, LANES), 0)) // SEL_LEN
    lane = lax.broadcasted_iota(jnp.int32, (rows, LANES), 1)
    ind = jnp.where(blk == lane, 1.0, 0.0).astype(BF16)
    for h in range(NSA_KV_HEADS):
        hs = slice(h * HEAD_DIM, (h + 1) * HEAD_DIM)
        vsl = slice(NSA_KV_W + h * HEAD_DIM, NSA_KV_W + (h + 1) * HEAD_DIM)
        ksx_ref[:, 2 * h * HEAD_DIM:(2 * h + 1) * HEAD_DIM] = _rope(ksv_ref[:, hs], c, s).astype(BF16)
        ksx_ref[:, (2 * h + 1) * HEAD_DIM:(2 * h + 2) * HEAD_DIM] = ind
        kwo_ref[:, hs] = _rope(kwv_ref[:, hs], c, s).astype(BF16)
        vs_t = ksv_ref[:, vsl].T.astype(BF16)
        vw_t = kwv_ref[:, vsl].T.astype(BF16)
        for j in range(rows // Q_TILE):
            vso_ref[h, j] = vs_t[:, j * Q_TILE:(j + 1) * Q_TILE]
            vwo_ref[h, j] = vw_t[:, j * Q_TILE:(j + 1) * Q_TILE]


def _kv_rot(kvp, seq, pos_col, inv_full, sgn_full):
    n = kvp.shape[0]
    T = KV_ROW_TILE
    pair_w = 2 * NSA_KV_W
    row = lambda w: pl.BlockSpec((T, w), lambda i: (i, 0))
    vt_shape = (NSA_KV_HEADS, n // Q_TILE, HEAD_DIM, Q_TILE)
    vt_spec = pl.BlockSpec((NSA_KV_HEADS, T // Q_TILE, HEAD_DIM, Q_TILE), lambda i: (0, i, 0, 0))
    return pl.pallas_call(
        functools.partial(_kv_rot_kernel, seq),
        grid=(n // T,),
        in_specs=[
            pl.BlockSpec((T, pair_w), lambda i: (i, 1)),
            pl.BlockSpec((T, pair_w), lambda i: (i, 2)),
            row(1),
            pl.BlockSpec((1, LANES), lambda i: (0, 0)),
            pl.BlockSpec((1, LANES), lambda i: (0, 0)),
        ],
        out_specs=[row(2 * NSA_KV_W), vt_spec, row(NSA_KV_W), vt_spec, row(LANES), row(LANES)],
        out_shape=[
            jax.ShapeDtypeStruct((n, 2 * NSA_KV_W), BF16),
            jax.ShapeDtypeStruct(vt_shape, BF16),
            jax.ShapeDtypeStruct((n, NSA_KV_W), BF16),
            jax.ShapeDtypeStruct(vt_shape, BF16),
            jax.ShapeDtypeStruct((n, LANES), F32),
            jax.ShapeDtypeStruct((n, LANES), F32),
        ],
        compiler_params=pltpu.CompilerParams(dimension_semantics=("arbitrary",), vmem_limit_bytes=VMEM_LIMIT),
        name="kv_rot",
    )(kvp, kvp, pos_col, inv_full, sgn_full)


def _compress_kernel(kc_ref, vc_ref, cos_ref, sin_ref, pe_ref, w1_ref, w2_ref, kcmp_ref, vcmp_ref):
    S = kc_ref.shape[0]
    n_half = S // CMP_STRIDE
    assert CMP_LEN == 2 * CMP_STRIDE
    for part, (src_ref, dst_ref) in enumerate(((kc_ref, kcmp_ref), (vc_ref, vcmp_ref))):
        acc_first = jnp.zeros((n_half, CMP_HID), F32)
        acc_second = jnp.zeros((n_half, CMP_HID), F32)
        for l in range(CMP_STRIDE):
            z = src_ref[pl.ds(l, n_half, stride=CMP_STRIDE), :]
            z1 = (z + pe_ref[part, l:l + 1, :]).astype(BF16)
            acc_first = acc_first + _dot(z1, w1_ref[part, l * HEAD_DIM:(l + 1) * HEAD_DIM, :])
            l2 = l + CMP_STRIDE
            z2 = (z + pe_ref[part, l2:l2 + 1, :]).astype(BF16)
            acc_second = acc_second + _dot(z2, w1_ref[part, l2 * HEAD_DIM:(l2 + 1) * HEAD_DIM, :])
        pre = acc_first + pltpu.roll(acc_second, n_half - 1, axis=0)
        out = _dot(_silu(pre).astype(BF16), w2_ref[part])
        if part == 0:
            c_end = pltpu.roll(cos_ref[pl.ds(CMP_STRIDE - 1, n_half, stride=CMP_STRIDE), :], n_half - 1, axis=0)
            s_end = pltpu.roll(sin_ref[pl.ds(CMP_STRIDE - 1, n_half, stride=CMP_STRIDE), :], n_half - 1, axis=0)
            out = _rope(out, c_end, s_end)
            dst_ref[...] = out.astype(BF16)
        else:
            dst_ref[...] = out.T.astype(BF16)


def _compress(kvp, batch, seq, cos_tab, sin_tab, cmp_pe, cmp_w1, cmp_w2):
    n_half = seq // CMP_STRIDE
    cmp_out = pl.BlockSpec((None, None, n_half, HEAD_DIM), lambda b, g: (b, g, 0, 0))
    tab = pl.BlockSpec((seq, LANES), lambda b, g: (b, 0))
    return pl.pallas_call(
        _compress_kernel,
        grid=(batch, NSA_KV_HEADS),
        in_specs=[
            pl.BlockSpec((seq, HEAD_DIM), lambda b, g: (b, g)),
            pl.BlockSpec((seq, HEAD_DIM), lambda b, g: (b, NSA_KV_HEADS + g)),
            tab, tab,
            _resident(cmp_pe.shape), _resident(cmp_w1.shape), _resident(cmp_w2.shape),
        ],
        out_specs=[cmp_out, pl.BlockSpec((None, None, HEAD_DIM, n_half), lambda b, g: (b, g, 0, 0))],
        out_shape=[
            jax.ShapeDtypeStruct((batch, NSA_KV_HEADS, n_half, HEAD_DIM), BF16),
            jax.ShapeDtypeStruct((batch, NSA_KV_HEADS, HEAD_DIM, n_half), BF16),
        ],
        compiler_params=pltpu.CompilerParams(dimension_semantics=("arbitrary", "arbitrary"),
                                             vmem_limit_bytes=VMEM_LIMIT),
        name="compress",
    )(kvp, kvp, cos_tab, sin_tab, cmp_pe, cmp_w1, cmp_w2)


def _nsa_kernel(q_ref, zq_ref, gl_ref, cos_ref, sin_ref, ksx_ref, vst_ref, kw_ref, vwt_ref, kcmp_ref, vcmpt_ref,
                o_ref, qxt_ref, sc_ref, m_ref, l_ref, acc_ref, ocmp_ref, osel_ref, s_ref, p_ref, alpha_ref,
                rank_ref, owin_ref):
    TQ = q_ref.shape[0]
    R = NSA_GROUP
    M = R * TQ
    n_cmp_pad = kcmp_ref.shape[0]
    n_sel = ksx_ref.shape[0] // SEL_LEN
    g = pl.program_id(1)
    qi = pl.program_id(2)
    t0 = qi * TQ

    cos_t = cos_ref[...]
    sin_t = sin_ref[...]
    for r in range(R):
        qr = _rope(q_ref[:, r * HEAD_DIM:(r + 1) * HEAD_DIM], cos_t, sin_t) * (QK_SCALE * LOG2E)
        qxt_ref[0:HEAD_DIM, r * TQ:(r + 1) * TQ] = qr.T.astype(BF16)
    qxt_ref[HEAD_DIM + n_sel:2 * HEAD_DIM, :] = jnp.zeros((HEAD_DIM - n_sel, M), BF16)

    lane_q = lax.broadcasted_iota(jnp.int32, (1, M), 1) % TQ
    q_t = qxt_ref[0:HEAD_DIM, :]
    slabs = [slice(r * TQ, (r + 1) * TQ) for r in range(R)]
    key_sub = lax.broadcasted_iota(jnp.int32, (TQ, TQ), 0)
    q_lane = lax.broadcasted_iota(jnp.int32, (TQ, TQ), 1)
    causal = key_sub <= q_lane

    assert WINDOW == 2 * TQ
    w_tiles = (jnp.maximum(qi - 2, 0), jnp.maximum(qi - 1, 0), qi)
    w_masks = ((key_sub > q_lane) & (qi >= 2), (key_sub >= 0) & (qi >= 1), causal)
    w_keys = [kw_ref[pl.ds(pl.multiple_of(kt * TQ, TQ), TQ), :] for kt in w_tiles]
    w_scores = [[_dot(k, qxt_ref[0:HEAD_DIM, rs]) for k in w_keys] for rs in slabs]

    s_t = _dot(kcmp_ref[...], q_t)
    cmp_end = lax.broadcasted_iota(jnp.int32, (n_cmp_pad, 1), 0) * CMP_STRIDE + (CMP_LEN - 1)
    valid = cmp_end <= t0 + lane_q
    s_t = jnp.where(valid, s_t, NEG_INF)
    e = jnp.where(valid, jnp.exp2(s_t - jnp.max(s_t, axis=0, keepdims=True)), 0.0)
    p_cmp = e * (1.0 / jnp.maximum(jnp.sum(e, axis=0, keepdims=True), 1e-30))
    ocmp_ref[...] = _dot(vcmpt_ref[...], p_cmp.astype(BF16))

    w_probs = []
    w_inv_l = []
    for s_head in w_scores:
        s_head = [jnp.where(mk, s, NEG_INF) for s, mk in zip(s_head, w_masks)]
        m_col = jnp.max(s_head[0], axis=0, keepdims=True)
        for s in s_head[1:]:
            m_col = jnp.maximum(m_col, jnp.max(s, axis=0, keepdims=True))
        p_head = [jnp.exp2(s - m_col) for s in s_head]
        l_col = jnp.sum(p_head[0], axis=0, keepdims=True)
        for p in p_head[1:]:
            l_col = l_col + jnp.sum(p, axis=0, keepdims=True)
        w_inv_l.append(1.0 / l_col)
        w_probs.append([p.astype(BF16) for p in p_head])

    p_sum = p_cmp[:, 0:TQ]
    for r in range(1, R):
        p_sum = p_sum + p_cmp[:, r * TQ:(r + 1) * TQ]
    jn = lax.broadcasted_iota(jnp.int32, (n_sel, n_cmp_pad), 0) * SEL_LEN
    cn = lax.broadcasted_iota(jnp.int32, (n_sel, n_cmp_pad), 1) * CMP_STRIDE
    ov = jnp.minimum(cn + (CMP_LEN - 1), jn + (SEL_LEN - 1)) - jnp.maximum(cn, jn) + 1
    ov_t = (jnp.maximum(ov, 0).astype(F32) / CMP_LEN).astype(BF16)
    p_hi = p_sum.astype(BF16)
    p_lo = (p_sum - p_hi.astype(F32)).astype(BF16)
    imp = _dot(ov_t, p_hi) + _dot(ov_t, p_lo)

    for rs, p_head, inv_l in zip(slabs, w_probs, w_inv_l):
        o = _dot(vwt_ref[w_tiles[0]], p_head[0])
        for kt, p in zip(w_tiles[1:], p_head[1:]):
            o = o + _dot(vwt_ref[kt], p)
        owin_ref[:, rs] = o * inv_l

    jb = lax.broadcasted_iota(jnp.int32, (n_sel, TQ), 0)
    tq = t0 + lax.broadcasted_iota(jnp.int32, (n_sel, TQ), 1)
    cur = tq // SEL_LEN
    forced = (jb == 0) | (jb == cur) | (jb == cur - 1)
    ok = jb * SEL_LEN <= tq
    score = jnp.where(ok, jnp.where(forced, FORCE_SCORE, imp), NEG_INF)
    sc_ref[...] = score

    rank_ref[...] = jnp.zeros((n_sel, TQ), F32)
    blocks_per_tile = TQ // SEL_LEN
    for kt in range(n_sel // blocks_per_tile):
        @pl.when(kt <= qi)
        def _(kt=kt):
            others = [sc_ref[j2:j2 + 1, :] for j2 in range(kt * blocks_per_tile, (kt + 1) * blocks_per_tile)]
            for gi in range(n_sel // SUBLANES):
                rows = slice(gi * SUBLANES, (gi + 1) * SUBLANES)
                sg = sc_ref[rows, :]
                cnt = rank_ref[rows, :]
                for u, other in enumerate(others):
                    j2 = kt * blocks_per_tile + u
                    if gi * SUBLANES > j2:
                        before = other >= sg
                    elif (gi + 1) * SUBLANES - 1 < j2:
                        before = other > sg
                    else:
                        row_j = gi * SUBLANES + lax.broadcasted_iota(jnp.int32, (SUBLANES, TQ), 0)
                        before = (other > sg) | ((other == sg) & (row_j > j2))
                    cnt = cnt + jnp.where(before, 1.0, 0.0)
                rank_ref[rows, :] = cnt
    n_top = min(SEL_TOPK, n_sel)
    bias_t = jnp.where(rank_ref[...] < n_top, 0.0, SEL_BIAS).astype(BF16)
    for r in range(R):
        qxt_ref[HEAD_DIM:HEAD_DIM + n_sel, r * TQ:(r + 1) * TQ] = bias_t

    diag = pl.ds(pl.multiple_of(t0, TQ), TQ)

    def sel_scores(k_tile):
        return [_dot(k_tile, qxt_ref[:, rs]) for rs in slabs]

    def sel_pv(v_t):
        return [_dot(v_t, p_ref[:, rs]) for rs in slabs]

    n_full = qi
    s_diag = sel_scores(ksx_ref[diag, :])
    s_first = sel_scores(ksx_ref[0:TQ, :])
    for rs, s_t in zip(slabs, s_diag):
        s_t = jnp.where(causal, s_t, NEG_INF)
        m_tile = jnp.max(s_t, axis=0, keepdims=True)
        p = jnp.exp2(s_t - m_tile)
        m_ref[:, rs] = m_tile
        l_ref[:, rs] = jnp.sum(p, axis=0, keepdims=True)
        p_ref[:, rs] = p.astype(BF16)
    alpha_ref[...] = jnp.ones((1, M), F32)
    acc_ref[...] = jnp.zeros((HEAD_DIM, M), F32)
    for rs, s_t in zip(slabs, s_first):
        s_ref[:, rs] = s_t

    def sel_body(kt, carry):
        alpha_prev = alpha_ref[...]
        prev = jnp.where(kt == 0, qi, kt - 1)
        pv_prev = sel_pv(vst_ref[prev])
        nxt = jnp.minimum(kt + 1, n_full - 1)
        s_next = sel_scores(ksx_ref[pl.ds(pl.multiple_of(nxt * TQ, TQ), TQ), :])
        for rs in slabs:
            s_t = s_ref[:, rs]
            m_prev = m_ref[:, rs]
            m_new = jnp.maximum(m_prev, jnp.max(s_t, axis=0, keepdims=True))
            alpha = jnp.exp2(m_prev - m_new)
            p = jnp.exp2(s_t - m_new)
            m_ref[:, rs] = m_new
            l_ref[:, rs] = alpha * l_ref[:, rs] + jnp.sum(p, axis=0, keepdims=True)
            alpha_ref[:, rs] = alpha
            p_ref[:, rs] = p.astype(BF16)
        for rs, o in zip(slabs, pv_prev):
            acc_ref[:, rs] = alpha_prev[:, rs] * acc_ref[:, rs] + o
        for rs, s_t in zip(slabs, s_next):
            s_ref[:, rs] = s_t
        return carry
    lax.fori_loop(0, n_full, sel_body, 0)

    last = jnp.where(n_full == 0, qi, n_full - 1)
    inv_l = 1.0 / l_ref[...]
    for rs, o in zip(slabs, sel_pv(vst_ref[last])):
        osel_ref[:, rs] = (alpha_ref[:, rs] * acc_ref[:, rs] + o) * inv_l[:, rs]

    o_win = owin_ref[...]

    gates = 1.0 / (1.0 + jnp.exp(-gl_ref[...]))
    gates_t = pltpu.roll(gates, (LANES - 3 * R * g) % LANES, axis=1).T
    for r in range(R):
        rs = slice(r * TQ, (r + 1) * TQ)
        mix_t = (gates_t[3 * r:3 * r + 1, :] * ocmp_ref[:, rs] + gates_t[3 * r + 1:3 * r + 2, :] * osel_ref[:, rs]
                 + gates_t[3 * r + 2:3 * r + 3, :] * o_win[:, rs])
        cs = slice(r * HEAD_DIM, (r + 1) * HEAD_DIM)
        o_ref[:, cs] = (mix_t.T * _silu(zq_ref[:, cs])).astype(o_ref.dtype)


def _nsa(p1, batch, seq, cos_tab, sin_tab, ksx, vs_t, kw, vw_t, kcmp, vcmp_t):
    n = p1.shape[0]
    TQ = Q_TILE
    R = NSA_GROUP
    q_tiles = seq // TQ
    gw = R * HEAD_DIM
    n_cmp_pad = kcmp.shape[2]
    n_sel = seq // SEL_LEN
    tok = lambda b, g, i: b * q_tiles + i
    head_kv = pl.BlockSpec((seq, HEAD_DIM), lambda b, g, i: (b, g))
    head_vt = pl.BlockSpec((None, q_tiles, HEAD_DIM, TQ), lambda b, g, i: (g, b, 0, 0))
    cmp_kv = pl.BlockSpec((None, None, n_cmp_pad, HEAD_DIM), lambda b, g, i: (b, g, 0, 0))
    cmp_vt = pl.BlockSpec((None, None, HEAD_DIM, n_cmp_pad), lambda b, g, i: (b, g, 0, 0))
    return pl.pallas_call(
        _nsa_kernel,
        grid=(batch, NSA_KV_HEADS, q_tiles),
        in_specs=[
            pl.BlockSpec((TQ, gw), lambda b, g, i: (tok(b, g, i), g)),
            pl.BlockSpec((TQ, gw), lambda b, g, i: (tok(b, g, i), NSA_W // gw + g)),
            pl.BlockSpec((TQ, GATE_PAD), lambda b, g, i: (tok(b, g, i), (2 * NSA_W + 2 * MEM_W) // GATE_PAD)),
            pl.BlockSpec((TQ, LANES), lambda b, g, i: (tok(b, g, i), 0)),
            pl.BlockSpec((TQ, LANES), lambda b, g, i: (tok(b, g, i), 0)),
            pl.BlockSpec((seq, 2 * HEAD_DIM), lambda b, g, i: (b, g)),
            head_vt, head_kv, head_vt, cmp_kv, cmp_vt,
        ],
        out_specs=pl.BlockSpec((TQ, gw), lambda b, g, i: (tok(b, g, i), g)),
        out_shape=jax.ShapeDtypeStruct((n, NSA_W), BF16),
        scratch_shapes=[
            pltpu.VMEM((2 * HEAD_DIM, R * TQ), BF16),
            pltpu.VMEM((n_sel, TQ), F32),
            pltpu.VMEM((1, R * TQ), F32),
            pltpu.VMEM((1, R * TQ), F32),
            pltpu.VMEM((HEAD_DIM, R * TQ), F32),
            pltpu.VMEM((HEAD_DIM, R * TQ), F32),
            pltpu.VMEM((HEAD_DIM, R * TQ), F32),
            pltpu.VMEM((TQ, R * TQ), F32),
            pltpu.VMEM((TQ, R * TQ), BF16),
            pltpu.VMEM((1, R * TQ), F32),
            pltpu.VMEM((n_sel, TQ), F32),
            pltpu.VMEM((HEAD_DIM, R * TQ), F32),
        ],
        compiler_params=pltpu.CompilerParams(dimension_semantics=("arbitrary", "arbitrary", "arbitrary"),
                                             vmem_limit_bytes=VMEM_LIMIT),
        name="nsa",
    )(p1, p1, p1, cos_tab, sin_tab, ksx, vs_t, kw, vw_t, kcmp, vcmp_t)


def _tail_kernel(y_ref, qz_ref, mk_ref, mv_ref, h_ref, wout_ref, g_ref, o_ref, ymem_ref, acc_ref):
    for c in range(0, D_MODEL, COL_CHUNK):
        cs = slice(c, c + COL_CHUNK)
        acc_ref[:, cs] = h_ref[:, cs] + _dot(y_ref[...], wout_ref[0:NSA_W, cs])
    qm = qz_ref[:, 0:MEM_W]
    zm = qz_ref[:, MEM_W:2 * MEM_W]
    _memory_attention(qm, zm, mk_ref, mv_ref, ymem_ref, 0)
    for c in range(0, D_MODEL, COL_CHUNK):
        cs = slice(c, c + COL_CHUNK)
        acc_ref[:, cs] = acc_ref[:, cs] + _dot(ymem_ref[...], wout_ref[NSA_W:D_MODEL, cs])
    o_ref[...] = _rms_scale(acc_ref[...], g_ref[...])


def _tail(y_nsa, p1, seq, mem_k, mem_v, h2d, w_out, final_g):
    n = h2d.shape[0]
    T = TOK_TILE
    tiles_per_seq = seq // T
    return pl.pallas_call(
        _tail_kernel,
        grid=(n // T,),
        in_specs=[
            pl.BlockSpec((T, NSA_W), lambda i: (i, 0)),
            pl.BlockSpec((T, 2 * MEM_W), lambda i: (i, 2 * NSA_W // (2 * MEM_W))),
            pl.BlockSpec((None, MEM_LEN, MEM_W), lambda i: (i // tiles_per_seq, 0, 0)),
            pl.BlockSpec((None, MEM_LEN, MEM_W), lambda i: (i // tiles_per_seq, 0, 0)),
            pl.BlockSpec((T, D_MODEL), lambda i: (i, 0)),
            _resident((D_MODEL, D_MODEL)),
            _resident((1, D_MODEL)),
        ],
        out_specs=pl.BlockSpec((T, D_MODEL), lambda i: (i, 0)),
        out_shape=jax.ShapeDtypeStruct((n, D_MODEL), F32),
        scratch_shapes=[pltpu.VMEM((T, MEM_W), BF16), pltpu.VMEM((T, D_MODEL), F32)],
        compiler_params=pltpu.CompilerParams(dimension_semantics=("arbitrary",), vmem_limit_bytes=VMEM_LIMIT),
        name="tail",
    )(y_nsa, p1, mem_k, mem_v, h2d, w_out, final_g.reshape(1, D_MODEL))


def kernel(x, mem, positions, norm_g, mem_norm_g, w_mem_kv, w_out, a_w_in, a_w_pool, a_pool_scale, b_w_in,
           kv_norm_g, w_kv, cmp_pe, cmp_w1, cmp_w2, final_g):
    batch, seq, _ = x.shape
    assert seq % Q_TILE == 0 and seq % TOK_TILE == 0 and Q_TILE % SEL_LEN == 0
    n = batch * seq
    x2d = x.reshape(n, D_MODEL)

    mkv = _mem_kv(mem.reshape(batch * MEM_LEN, D_MODEL), mem_norm_g, w_mem_kv.astype(BF16))
    mkv = mkv.reshape(mkv.shape[0], batch, MEM_LEN, 2 * MEM_W)
    mem_k, mem_v = mkv[..., :MEM_W], mkv[..., MEM_W:]

    h1 = _layer0(x2d, seq, norm_g[0], a_w_in[0].astype(BF16), a_w_pool[0].astype(BF16), a_pool_scale[0],
                 mem_k[0], mem_v[0], w_out[0].astype(BF16))

    wb = b_w_in[0]
    o1 = NSA_W
    o2 = o1 + 3 * NSA_HEADS
    o3 = o2 + NSA_W
    wb = jnp.concatenate([wb[:, :o1].astype(BF16), wb[:, o2:].astype(BF16), wb[:, o1:o2].astype(BF16),
                          jnp.zeros((D_MODEL, GATE_PAD - 3 * NSA_HEADS), BF16)], axis=1)
    kvp, p1 = _proj1(h1, kv_norm_g, norm_g[1], w_kv.astype(BF16), wb)

    half = jnp.arange(ROT_HALF, dtype=F32)
    inv = ROPE_THETA ** (-half * 2.0 / ROT_DIM)
    pad = jnp.zeros((LANES - ROT_DIM,), F32)
    inv_full = jnp.concatenate([inv, inv, pad]).reshape(1, LANES)
    sgn_full = jnp.concatenate([-jnp.ones((ROT_HALF,), F32), jnp.ones((ROT_HALF,), F32), pad]).reshape(1, LANES)
    ksx, vs, kw, vw, cos_tab, sin_tab = _kv_rot(kvp, seq, positions.reshape(n, 1), inv_full, sgn_full)
    kcmp, vcmp = _compress(kvp, batch, seq, cos_tab, sin_tab, cmp_pe, cmp_w1.astype(BF16), cmp_w2.astype(BF16))

    y_nsa = _nsa(p1, batch, seq, cos_tab, sin_tab, ksx, vs, kw, vw, kcmp, vcmp)
    out = _tail(y_nsa, p1, seq, mem_k[1], mem_v[1], h1, w_out[1].astype(BF16), final_g)
    return out.reshape(batch, seq, D_MODEL)
```

```python
import functools

import jax
import jax.numpy as jnp
from jax import lax
from jax.experimental import pallas as pl
from jax.experimental.pallas import tpu as pltpu

D_MODEL = 2048
MEM_LEN = 256
HEAD_DIM = 128
MEM_HEADS = 4
MEM_W = MEM_HEADS * HEAD_DIM
POOL_W = D_MODEL - MEM_W
POOL_WINDOWS = (2, 4, 8, 16)
POOL_GC = POOL_W // len(POOL_WINDOWS)
POOL_HALO = 16
NSA_W = D_MODEL - MEM_W
NSA_HEADS = NSA_W // HEAD_DIM
NSA_KV_HEADS = 4
NSA_GROUP = NSA_HEADS // NSA_KV_HEADS
NSA_KV_W = NSA_KV_HEADS * HEAD_DIM
CMP_LEN = 32
CMP_STRIDE = 16
CMP_HID = 256
SEL_LEN = 64
SEL_TOPK = 16
WINDOW = 512
ROT_DIM = HEAD_DIM // 4
ROT_HALF = ROT_DIM // 2
ROPE_THETA = 500000.0
NORM_EPS = 1e-6
FORCE_SCORE = 1e4
NEG_INF = -1e30
SEL_BIAS = -1e9
QK_SCALE = HEAD_DIM ** -0.5
LOG2E = 1.4426950408889634

LANES = 128
SUBLANES = 8
TOK_TILE = 256
Q_TILE = 256
COL_CHUNK = 512
GATE_PAD = LANES
VMEM_LIMIT = 56 * 1024 * 1024

F32 = jnp.float32
BF16 = jnp.bfloat16


def _resident(shape):
    nd = len(shape)
    return pl.BlockSpec(shape, lambda *_: (0,) * nd, pipeline_mode=pl.Buffered(1))


def _rms_scale(x, g):
    ms = jnp.mean(x * x, axis=-1, keepdims=True)
    return x * lax.rsqrt(ms + NORM_EPS) * g


def _silu(z):
    return z * (1.0 / (1.0 + jnp.exp(-z)))


def _dot(a, b):
    return jnp.dot(a, b, preferred_element_type=F32)


def _dot_nt(a, b):
    return lax.dot_general(a, b, (((1,), (1,)), ((), ())), preferred_element_type=F32)


def _memory_attention(qm, zm, k_ref, v_ref, y_ref, col0):
    for h in range(MEM_HEADS):
        cs = slice(h * HEAD_DIM, (h + 1) * HEAD_DIM)
        q = (qm[:, cs] * QK_SCALE).astype(BF16)
        s = _dot_nt(q, k_ref[:, cs])
        e = jnp.exp(s - jnp.max(s, axis=-1, keepdims=True))
        p = e / jnp.sum(e, axis=-1, keepdims=True)
        o = _dot(p.astype(BF16), v_ref[:, cs])
        y_ref[:, col0 + h * HEAD_DIM:col0 + (h + 1) * HEAD_DIM] = (o * _silu(zm[:, cs])).astype(y_ref.dtype)


def _mem_kv_kernel(mem_ref, g_ref, w_ref, o_ref):
    hn = _rms_scale(mem_ref[...], g_ref[...]).astype(BF16)
    o_ref[...] = _dot(hn, w_ref[...]).astype(o_ref.dtype)


def _mem_kv(mem2d, mem_norm_g, w_mem_kv_bf16):
    depth = w_mem_kv_bf16.shape[0]
    rows = mem2d.shape[0]
    T = TOK_TILE
    return pl.pallas_call(
        _mem_kv_kernel,
        grid=(depth, rows // T),
        in_specs=[
            pl.BlockSpec((T, D_MODEL), lambda l, i: (i, 0)),
            pl.BlockSpec((None, 1, D_MODEL), lambda l, i: (l, 0, 0)),
            pl.BlockSpec((None, D_MODEL, 2 * MEM_W), lambda l, i: (l, 0, 0)),
        ],
        out_specs=pl.BlockSpec((None, T, 2 * MEM_W), lambda l, i: (l, i, 0)),
        out_shape=jax.ShapeDtypeStruct((depth, rows, 2 * MEM_W), BF16),
        compiler_params=pltpu.CompilerParams(dimension_semantics=("arbitrary", "arbitrary"),
                                             vmem_limit_bytes=VMEM_LIMIT),
        name="mem_kv",
    )(mem2d, mem_norm_g.reshape(depth, 1, D_MODEL), w_mem_kv_bf16)


def _layer0_kernel(tiles_per_seq, x_ref, g_ref, win_ref, wpool_ref, scale_ref, mk_ref, mv_ref, wout_ref,
                   o_ref, hn_ref, proj_ref, uext_ref, y_ref):
    T = x_ref.shape[0]
    tb = pl.program_id(0) % tiles_per_seq
    x = x_ref[...]
    hn_ref[...] = _rms_scale(x, g_ref[...]).astype(BF16)

    @pl.when(tb == 0)
    def _():
        uext_ref[0:POOL_HALO, :] = jnp.zeros((POOL_HALO, POOL_W), F32)

    @pl.when(tb != 0)
    def _():
        uext_ref[0:POOL_HALO, :] = uext_ref[T:T + POOL_HALO, :]

    n_in = win_ref.shape[1]
    for c in range(0, n_in, COL_CHUNK):
        blk = _dot(hn_ref[...], win_ref[:, c:c + COL_CHUNK])
        if c < POOL_W:
            uext_ref[POOL_HALO:POOL_HALO + T, c:c + COL_CHUNK] = blk
        else:
            proj_ref[:, c - POOL_W:c - POOL_W + COL_CHUNK] = blk

    t1 = (tb * T + lax.broadcasted_iota(jnp.int32, (T, 1), 0) + 1).astype(F32)
    for gi, win in enumerate(POOL_WINDOWS):
        cs = slice(gi * POOL_GC, (gi + 1) * POOL_GC)
        u = uext_ref[POOL_HALO:POOL_HALO + T, cs]
        acc = u
        for k in range(1, win):
            acc = acc + uext_ref[POOL_HALO - k:POOL_HALO - k + T, cs]
        pooled = acc / jnp.minimum(t1, float(win)) - u
        mixed = _dot(pooled.astype(BF16), wpool_ref[gi])
        y_ref[:, cs] = (mixed * scale_ref[:, cs] * _silu(proj_ref[:, cs])).astype(BF16)

    for c in range(0, D_MODEL, COL_CHUNK):
        cs = slice(c, c + COL_CHUNK)
        o_ref[:, cs] = x_ref[:, cs] + _dot(y_ref[:, 0:POOL_W], wout_ref[0:POOL_W, cs])

    qm = proj_ref[:, POOL_W:POOL_W + MEM_W]
    zm = proj_ref[:, POOL_W + MEM_W:POOL_W + 2 * MEM_W]
    _memory_attention(qm, zm, mk_ref, mv_ref, y_ref, POOL_W)

    for c in range(0, D_MODEL, COL_CHUNK):
        cs = slice(c, c + COL_CHUNK)
        o_ref[:, cs] = o_ref[:, cs] + _dot(y_ref[:, POOL_W:D_MODEL], wout_ref[POOL_W:D_MODEL, cs])


def _layer0(x2d, seq, norm_g, w_in, w_pool, pool_scale, mem_k, mem_v, w_out):
    n = x2d.shape[0]
    T = TOK_TILE
    tiles_per_seq = seq // T
    n_in = w_in.shape[1]
    return pl.pallas_call(
        functools.partial(_layer0_kernel, tiles_per_seq),
        grid=(n // T,),
        in_specs=[
            pl.BlockSpec((T, D_MODEL), lambda i: (i, 0)),
            _resident((1, D_MODEL)),
            _resident((D_MODEL, n_in)),
            _resident(w_pool.shape),
            _resident((1, POOL_W)),
            pl.BlockSpec((None, MEM_LEN, MEM_W), lambda i: (i // tiles_per_seq, 0, 0)),
            pl.BlockSpec((None, MEM_LEN, MEM_W), lambda i: (i // tiles_per_seq, 0, 0)),
            _resident((D_MODEL, D_MODEL)),
        ],
        out_specs=pl.BlockSpec((T, D_MODEL), lambda i: (i, 0)),
        out_shape=jax.ShapeDtypeStruct((n, D_MODEL), F32),
        scratch_shapes=[
            pltpu.VMEM((T, D_MODEL), BF16),
            pltpu.VMEM((T, n_in - POOL_W), F32),
            pltpu.VMEM((T + POOL_HALO, POOL_W), F32),
            pltpu.VMEM((T, D_MODEL), BF16),
        ],
        compiler_params=pltpu.CompilerParams(dimension_semantics=("arbitrary",), vmem_limit_bytes=VMEM_LIMIT),
        name="layer0",
    )(x2d, norm_g.reshape(1, D_MODEL), w_in, w_pool, pool_scale.reshape(1, POOL_W), mem_k, mem_v, w_out)


def _proj1_kernel(h_ref, gkv_ref, g1_ref, wkv_ref, w1_ref, kv_ref, p1_ref, hn_ref):
    h = h_ref[...]
    ms = jnp.mean(h * h, axis=-1, keepdims=True)
    hs = h * lax.rsqrt(ms + NORM_EPS)
    hn_ref[...] = (hs * gkv_ref[...]).astype(BF16)
    n_kv = wkv_ref.shape[1]
    for c in range(0, n_kv, COL_CHUNK):
        kv_ref[:, c:c + COL_CHUNK] = _dot(hn_ref[...], wkv_ref[:, c:c + COL_CHUNK])
    hn_ref[...] = (hs * g1_ref[...]).astype(BF16)
    n_1 = w1_ref.shape[1]
    for c in range(0, n_1, COL_CHUNK):
        w = min(COL_CHUNK, n_1 - c)
        p1_ref[:, c:c + w] = _dot(hn_ref[...], w1_ref[:, c:c + w])


def _proj1(h2d, kv_norm_g, norm_g1, w_kv, w_1):
    n = h2d.shape[0]
    T = TOK_TILE
    n_kv, n_1 = w_kv.shape[1], w_1.shape[1]
    return pl.pallas_call(
        _proj1_kernel,
        grid=(n // T,),
        in_specs=[
            pl.BlockSpec((T, D_MODEL), lambda i: (i, 0)),
            _resident((1, D_MODEL)),
            _resident((1, D_MODEL)),
            _resident((D_MODEL, n_kv)),
            _resident((D_MODEL, n_1)),
        ],
        out_specs=[
            pl.BlockSpec((T, n_kv), lambda i: (i, 0)),
            pl.BlockSpec((T, n_1), lambda i: (i, 0)),
        ],
        out_shape=[
            jax.ShapeDtypeStruct((n, n_kv), F32),
            jax.ShapeDtypeStruct((n, n_1), F32),
        ],
        scratch_shapes=[pltpu.VMEM((T, D_MODEL), BF16)],
        compiler_params=pltpu.CompilerParams(dimension_semantics=("arbitrary",), vmem_limit_bytes=VMEM_LIMIT),
        name="proj1",
    )(h2d, kv_norm_g.reshape(1, D_MODEL), norm_g1.reshape(1, D_MODEL), w_kv, w_1)


def _rope(x, cos_t, sin_t):
    lane = lax.broadcasted_iota(jnp.int32, x.shape, 1)
    swapped = jnp.where(lane < ROT_HALF, pltpu.roll(x, LANES - ROT_HALF, axis=1), pltpu.roll(x, ROT_HALF, axis=1))
    return x * cos_t + swapped * sin_t


KV_ROW_TILE = 512


def _kv_rot_kernel(seq, ksv_ref, kwv_ref, pos_ref, inv_ref, sgn_ref, ksx_ref, vso_ref, kwo_ref, vwo_ref, cos_ref,
                   sin_ref):
    rows = ksv_ref.shape[0]
    ang = pos_ref[...].astype(F32) * inv_ref[...]
    c = jnp.cos(ang)
    s = jnp.sin(ang) * sgn_ref[...]
    cos_ref[...] = c
    sin_ref[...] = s
    r0 = (pl.program_id(0) * rows) % seq
    blk = (r0 + lax.broadcasted_iota(jnp.int32, (rows, LANES), 0)) // SEL_LEN
    lane = lax.broadcasted_iota(jnp.int32, (rows, LANES), 1)
    ind = jnp.where(blk == lane, 1.0, 0.0).astype(BF16)
    for h in range(NSA_KV_HEADS):
        hs = slice(h * HEAD_DIM, (h + 1) * HEAD_DIM)
        vsl = slice(NSA_KV_W + h * HEAD_DIM, NSA_KV_W + (h + 1) * HEAD_DIM)
        ksx_ref[:, 2 * h * HEAD_DIM:(2 * h + 1) * HEAD_DIM] = _rope(ksv_ref[:, hs], c, s).astype(BF16)
        ksx_ref[:, (2 * h + 1) * HEAD_DIM:(2 * h + 2) * HEAD_DIM] = ind
        kwo_ref[:, hs] = _rope(kwv_ref[:, hs], c, s).astype(BF16)
        vs_t = ksv_ref[:, vsl].T.astype(BF16)
        vw_t = kwv_ref[:, vsl].T.astype(BF16)
        for j in range(rows // Q_TILE):
            vso_ref[h, j] = vs_t[:, j * Q_TILE:(j + 1) * Q_TILE]
            vwo_ref[h, j] = vw_t[:, j * Q_TILE:(j + 1) * Q_TILE]


def _kv_rot(kvp, seq, pos_col, inv_full, sgn_full):
    n = kvp.shape[0]
    T = KV_ROW_TILE
    pair_w = 2 * NSA_KV_W
    row = lambda w: pl.BlockSpec((T, w), lambda i: (i, 0))
    vt_shape = (NSA_KV_HEADS, n // Q_TILE, HEAD_DIM, Q_TILE)
    vt_spec = pl.BlockSpec((NSA_KV_HEADS, T // Q_TILE, HEAD_DIM, Q_TILE), lambda i: (0, i, 0, 0))
    return pl.pallas_call(
        functools.partial(_kv_rot_kernel, seq),
        grid=(n // T,),
        in_specs=[
            pl.BlockSpec((T, pair_w), lambda i: (i, 1)),
            pl.BlockSpec((T, pair_w), lambda i: (i, 2)),
            row(1),
            pl.BlockSpec((1, LANES), lambda i: (0, 0)),
            pl.BlockSpec((1, LANES), lambda i: (0, 0)),
        ],
        out_specs=[row(2 * NSA_KV_W), vt_spec, row(NSA_KV_W), vt_spec, row(LANES), row(LANES)],
        out_shape=[
            jax.ShapeDtypeStruct((n, 2 * NSA_KV_W), BF16),
            jax.ShapeDtypeStruct(vt_shape, BF16),
            jax.ShapeDtypeStruct((n, NSA_KV_W), BF16),
            jax.ShapeDtypeStruct(vt_shape, BF16),
            jax.ShapeDtypeStruct((n, LANES), F32),
            jax.ShapeDtypeStruct((n, LANES), F32),
        ],
        compiler_params=pltpu.CompilerParams(dimension_semantics=("arbitrary",), vmem_limit_bytes=VMEM_LIMIT),
        name="kv_rot",
    )(kvp, kvp, pos_col, inv_full, sgn_full)


def _compress_kernel(kc_ref, vc_ref, cos_ref, sin_ref, pe_ref, w1_ref, w2_ref, kcmp_ref, vcmp_ref):
    S = kc_ref.shape[0]
    n_half = S // CMP_STRIDE
    assert CMP_LEN == 2 * CMP_STRIDE
    for part, (src_ref, dst_ref) in enumerate(((kc_ref, kcmp_ref), (vc_ref, vcmp_ref))):
        acc_first = jnp.zeros((n_half, CMP_HID), F32)
        acc_second = jnp.zeros((n_half, CMP_HID), F32)
        for l in range(CMP_STRIDE):
            z = src_ref[pl.ds(l, n_half, stride=CMP_STRIDE), :]
            z1 = (z + pe_ref[part, l:l + 1, :]).astype(BF16)
            acc_first = acc_first + _dot(z1, w1_ref[part, l * HEAD_DIM:(l + 1) * HEAD_DIM, :])
            l2 = l + CMP_STRIDE
            z2 = (z + pe_ref[part, l2:l2 + 1, :]).astype(BF16)
            acc_second = acc_second + _dot(z2, w1_ref[part, l2 * HEAD_DIM:(l2 + 1) * HEAD_DIM, :])
        pre = acc_first + pltpu.roll(acc_second, n_half - 1, axis=0)
        out = _dot(_silu(pre).astype(BF16), w2_ref[part])
        if part == 0:
            c_end = pltpu.roll(cos_ref[pl.ds(CMP_STRIDE - 1, n_half, stride=CMP_STRIDE), :], n_half - 1, axis=0)
            s_end = pltpu.roll(sin_ref[pl.ds(CMP_STRIDE - 1, n_half, stride=CMP_STRIDE), :], n_half - 1, axis=0)
            out = _rope(out, c_end, s_end)
            dst_ref[...] = out.astype(BF16)
        else:
            dst_ref[...] = out.T.astype(BF16)


def _compress(kvp, batch, seq, cos_tab, sin_tab, cmp_pe, cmp_w1, cmp_w2):
    n_half = seq // CMP_STRIDE
    cmp_out = pl.BlockSpec((None, None, n_half, HEAD_DIM), lambda b, g: (b, g, 0, 0))
    tab = pl.BlockSpec((seq, LANES), lambda b, g: (b, 0))
    return pl.pallas_call(
        _compress_kernel,
        grid=(batch, NSA_KV_HEADS),
        in_specs=[
            pl.BlockSpec((seq, HEAD_DIM), lambda b, g: (b, g)),
            pl.BlockSpec((seq, HEAD_DIM), lambda b, g: (b, NSA_KV_HEADS + g)),
            tab, tab,
            _resident(cmp_pe.shape), _resident(cmp_w1.shape), _resident(cmp_w2.shape),
        ],
        out_specs=[cmp_out, pl.BlockSpec((None, None, HEAD_DIM, n_half), lambda b, g: (b, g, 0, 0))],
        out_shape=[
            jax.ShapeDtypeStruct((batch, NSA_KV_HEADS, n_half, HEAD_DIM), BF16),
            jax.ShapeDtypeStruct((batch, NSA_KV_HEADS, HEAD_DIM, n_half), BF16),
        ],
        compiler_params=pltpu.CompilerParams(dimension_semantics=("arbitrary", "arbitrary"),
                                             vmem_limit_bytes=VMEM_LIMIT),
        name="compress",
    )(kvp, kvp, cos_tab, sin_tab, cmp_pe, cmp_w1, cmp_w2)


def _nsa_kernel(q_ref, zq_ref, gl_ref, cos_ref, sin_ref, ksx_ref, vst_ref, kw_ref, vwt_ref, kcmp_ref, vcmpt_ref,
                o_ref, qxt_ref, sc_ref, m_ref, l_ref, acc_ref, ocmp_ref, osel_ref, s_ref, p_ref, alpha_ref,
                rank_ref, owin_ref):
    TQ = q_ref.shape[0]
    R = NSA_GROUP
    M = R * TQ
    n_cmp_pad = kcmp_ref.shape[0]
    n_sel = ksx_ref.shape[0] // SEL_LEN
    g = pl.program_id(1)
    qi = pl.program_id(2)
    t0 = qi * TQ

    cos_t = cos_ref[...]
    sin_t = sin_ref[...]
    for r in range(R):
        qr = _rope(q_ref[:, r * HEAD_DIM:(r + 1) * HEAD_DIM], cos_t, sin_t) * (QK_SCALE * LOG2E)
        qxt_ref[0:HEAD_DIM, r * TQ:(r + 1) * TQ] = qr.T.astype(BF16)
    qxt_ref[HEAD_DIM + n_sel:2 * HEAD_DIM, :] = jnp.zeros((HEAD_DIM - n_sel, M), BF16)

    lane_q = lax.broadcasted_iota(jnp.int32, (1, M), 1) % TQ
    q_t = qxt_ref[0:HEAD_DIM, :]
    slabs = [slice(r * TQ, (r + 1) * TQ) for r in range(R)]
    key_sub = lax.broadcasted_iota(jnp.int32, (TQ, TQ), 0)
    q_lane = lax.broadcasted_iota(jnp.int32, (TQ, TQ), 1)
    causal = key_sub <= q_lane

    assert WINDOW == 2 * TQ
    w_tiles = (jnp.maximum(qi - 2, 0), jnp.maximum(qi - 1, 0), qi)
    w_masks = ((key_sub > q_lane) & (qi >= 2), (key_sub >= 0) & (qi >= 1), causal)
    w_keys = [kw_ref[pl.ds(pl.multiple_of(kt * TQ, TQ), TQ), :] for kt in w_tiles]
    w_scores = [[_dot(k, qxt_ref[0:HEAD_DIM, rs]) for k in w_keys] for rs in slabs]

    s_t = _dot(kcmp_ref[...], q_t)
    cmp_end = lax.broadcasted_iota(jnp.int32, (n_cmp_pad, 1), 0) * CMP_STRIDE + (CMP_LEN - 1)
    valid = cmp_end <= t0 + lane_q
    s_t = jnp.where(valid, s_t, NEG_INF)
    e = jnp.where(valid, jnp.exp2(s_t - jnp.max(s_t, axis=0, keepdims=True)), 0.0)
    p_cmp = e * (1.0 / jnp.maximum(jnp.sum(e, axis=0, keepdims=True), 1e-30))
    ocmp_ref[...] = _dot(vcmpt_ref[...], p_cmp.astype(BF16))

    w_probs = []
    w_inv_l = []
    for s_head in w_scores:
        s_head = [jnp.where(mk, s, NEG_INF) for s, mk in zip(s_head, w_masks)]
        m_col = jnp.max(s_head[0], axis=0, keepdims=True)
        for s in s_head[1:]:
            m_col = jnp.maximum(m_col, jnp.max(s, axis=0, keepdims=True))
        p_head = [jnp.exp2(s - m_col) for s in s_head]
        l_col = jnp.sum(p_head[0], axis=0, keepdims=True)
        for p in p_head[1:]:
            l_col = l_col + jnp.sum(p, axis=0, keepdims=True)
        w_inv_l.append(1.0 / l_col)
        w_probs.append([p.astype(BF16) for p in p_head])

    p_sum = p_cmp[:, 0:TQ]
    for r in range(1, R):
        p_sum = p_sum + p_cmp[:, r * TQ:(r + 1) * TQ]
    jn = lax.broadcasted_iota(jnp.int32, (n_sel, n_cmp_pad), 0) * SEL_LEN
    cn = lax.broadcasted_iota(jnp.int32, (n_sel, n_cmp_pad), 1) * CMP_STRIDE
    ov = jnp.minimum(cn + (CMP_LEN - 1), jn + (SEL_LEN - 1)) - jnp.maximum(cn, jn) + 1
    ov_t = (jnp.maximum(ov, 0).astype(F32) / CMP_LEN).astype(BF16)
    p_hi = p_sum.astype(BF16)
    p_lo = (p_sum - p_hi.astype(F32)).astype(BF16)
    imp = _dot(ov_t, p_hi) + _dot(ov_t, p_lo)

    for rs, p_head, inv_l in zip(slabs, w_probs, w_inv_l):
        o = _dot(vwt_ref[w_tiles[0]], p_head[0])
        for kt, p in zip(w_tiles[1:], p_head[1:]):
            o = o + _dot(vwt_ref[kt], p)
        owin_ref[:, rs] = o * inv_l

    jb = lax.broadcasted_iota(jnp.int32, (n_sel, TQ), 0)
    tq = t0 + lax.broadcasted_iota(jnp.int32, (n_sel, TQ), 1)
    cur = tq // SEL_LEN
    forced = (jb == 0) | (jb == cur) | (jb == cur - 1)
    ok = jb * SEL_LEN <= tq
    score = jnp.where(ok, jnp.where(forced, FORCE_SCORE, imp), NEG_INF)
    sc_ref[...] = score

    rank_ref[...] = jnp.zeros((n_sel, TQ), F32)
    blocks_per_tile = TQ // SEL_LEN
    for kt in range(n_sel // blocks_per_tile):
        @pl.when(kt <= qi)
        def _(kt=kt):
            others = [sc_ref[j2:j2 + 1, :] for j2 in range(kt * blocks_per_tile, (kt + 1) * blocks_per_tile)]
            for gi in range(n_sel // SUBLANES):
                rows = slice(gi * SUBLANES, (gi + 1) * SUBLANES)
                sg = sc_ref[rows, :]
                cnt = rank_ref[rows, :]
                for u, other in enumerate(others):
                    j2 = kt * blocks_per_tile + u
                    if gi * SUBLANES > j2:
                        before = other >= sg
                    elif (gi + 1) * SUBLANES - 1 < j2:
                        before = other > sg
                    else:
                        row_j = gi * SUBLANES + lax.broadcasted_iota(jnp.int32, (SUBLANES, TQ), 0)
                        before = (other > sg) | ((other == sg) & (row_j > j2))
                    cnt = cnt + jnp.where(before, 1.0, 0.0)
                rank_ref[rows, :] = cnt
    n_top = min(SEL_TOPK, n_sel)
    bias_t = jnp.where(rank_ref[...] < n_top, 0.0, SEL_BIAS).astype(BF16)
    for r in range(R):
        qxt_ref[HEAD_DIM:HEAD_DIM + n_sel, r * TQ:(r + 1) * TQ] = bias_t

    diag = pl.ds(pl.multiple_of(t0, TQ), TQ)

    def sel_scores(k_tile):
        return [_dot(k_tile, qxt_ref[:, rs]) for rs in slabs]

    def sel_pv(v_t):
        return [_dot(v_t, p_ref[:, rs]) for rs in slabs]

    n_full = qi
    s_diag = sel_scores(ksx_ref[diag, :])
    s_first = sel_scores(ksx_ref[0:TQ, :])
    for rs, s_t in zip(slabs, s_diag):
        s_t = jnp.where(causal, s_t, NEG_INF)
        m_tile = jnp.max(s_t, axis=0, keepdims=True)
        p = jnp.exp2(s_t - m_tile)
        m_ref[:, rs] = m_tile
        l_ref[:, rs] = jnp.sum(p, axis=0, keepdims=True)
        p_ref[:, rs] = p.astype(BF16)
    alpha_ref[...] = jnp.ones((1, M), F32)
    acc_ref[...] = jnp.zeros((HEAD_DIM, M), F32)
    for rs, s_t in zip(slabs, s_first):
        s_ref[:, rs] = s_t

    def sel_step(kt):
        alpha_prev = alpha_ref[...]
        nxt = jnp.minimum(kt + 1, n_full - 1)
        s_next = sel_scores(ksx_ref[pl.ds(pl.multiple_of(nxt * TQ, TQ), TQ), :])
        prev = jnp.where(kt == 0, qi, kt - 1)
        pv_prev = sel_pv(vst_ref[prev])
        for rs in slabs:
            s_t = s_ref[:, rs]
            m_prev = m_ref[:, rs]
            m_new = jnp.maximum(m_prev, jnp.max(s_t, axis=0, keepdims=True))
            alpha = jnp.exp2(m_prev - m_new)
            p = jnp.exp2(s_t - m_new)
            m_ref[:, rs] = m_new
            l_ref[:, rs] = alpha * l_ref[:, rs] + jnp.sum(p, axis=0, keepdims=True)
            alpha_ref[:, rs] = alpha
            p_ref[:, rs] = p.astype(BF16)
        for rs, o in zip(slabs, pv_prev):
            acc_ref[:, rs] = alpha_prev[:, rs] * acc_ref[:, rs] + o
        for rs, s_t in zip(slabs, s_next):
            s_ref[:, rs] = s_t

    def sel_pair(j, carry):
        sel_step(2 * j)
        sel_step(2 * j + 1)
        return carry
    lax.fori_loop(0, lax.shift_right_logical(n_full, 1), sel_pair, 0)

    @pl.when(n_full % 2 == 1)
    def _():
        sel_step(n_full - 1)

    last = jnp.where(n_full == 0, qi, n_full - 1)
    inv_l = 1.0 / l_ref[...]
    for rs, o in zip(slabs, sel_pv(vst_ref[last])):
        osel_ref[:, rs] = (alpha_ref[:, rs] * acc_ref[:, rs] + o) * inv_l[:, rs]

    o_win = owin_ref[...]

    gates = 1.0 / (1.0 + jnp.exp(-gl_ref[...]))
    gates_t = pltpu.roll(gates, (LANES - 3 * R * g) % LANES, axis=1).T
    for r in range(R):
        rs = slice(r * TQ, (r + 1) * TQ)
        mix_t = (gates_t[3 * r:3 * r + 1, :] * ocmp_ref[:, rs] + gates_t[3 * r + 1:3 * r + 2, :] * osel_ref[:, rs]
                 + gates_t[3 * r + 2:3 * r + 3, :] * o_win[:, rs])
        cs = slice(r * HEAD_DIM, (r + 1) * HEAD_DIM)
        o_ref[:, cs] = (mix_t.T * _silu(zq_ref[:, cs])).astype(o_ref.dtype)


def _nsa(p1, batch, seq, cos_tab, sin_tab, ksx, vs_t, kw, vw_t, kcmp, vcmp_t):
    n = p1.shape[0]
    TQ = Q_TILE
    R = NSA_GROUP
    q_tiles = seq // TQ
    gw = R * HEAD_DIM
    n_cmp_pad = kcmp.shape[2]
    n_sel = seq // SEL_LEN
    tok = lambda b, g, i: b * q_tiles + i
    head_kv = pl.BlockSpec((seq, HEAD_DIM), lambda b, g, i: (b, g))
    head_vt = pl.BlockSpec((None, q_tiles, HEAD_DIM, TQ), lambda b, g, i: (g, b, 0, 0))
    cmp_kv = pl.BlockSpec((None, None, n_cmp_pad, HEAD_DIM), lambda b, g, i: (b, g, 0, 0))
    cmp_vt = pl.BlockSpec((None, None, HEAD_DIM, n_cmp_pad), lambda b, g, i: (b, g, 0, 0))
    return pl.pallas_call(
        _nsa_kernel,
        grid=(batch, NSA_KV_HEADS, q_tiles),
        in_specs=[
            pl.BlockSpec((TQ, gw), lambda b, g, i: (tok(b, g, i), g)),
            pl.BlockSpec((TQ, gw), lambda b, g, i: (tok(b, g, i), NSA_W // gw + g)),
            pl.BlockSpec((TQ, GATE_PAD), lambda b, g, i: (tok(b, g, i), (2 * NSA_W + 2 * MEM_W) // GATE_PAD)),
            pl.BlockSpec((TQ, LANES), lambda b, g, i: (tok(b, g, i), 0)),
            pl.BlockSpec((TQ, LANES), lambda b, g, i: (tok(b, g, i), 0)),
            pl.BlockSpec((seq, 2 * HEAD_DIM), lambda b, g, i: (b, g)),
            head_vt, head_kv, head_vt, cmp_kv, cmp_vt,
        ],
        out_specs=pl.BlockSpec((TQ, gw), lambda b, g, i: (tok(b, g, i), g)),
        out_shape=jax.ShapeDtypeStruct((n, NSA_W), BF16),
        scratch_shapes=[
            pltpu.VMEM((2 * HEAD_DIM, R * TQ), BF16),
            pltpu.VMEM((n_sel, TQ), F32),
            pltpu.VMEM((1, R * TQ), F32),
            pltpu.VMEM((1, R * TQ), F32),
            pltpu.VMEM((HEAD_DIM, R * TQ), F32),
            pltpu.VMEM((HEAD_DIM, R * TQ), F32),
            pltpu.VMEM((HEAD_DIM, R * TQ), F32),
            pltpu.VMEM((TQ, R * TQ), F32),
            pltpu.VMEM((TQ, R * TQ), BF16),
            pltpu.VMEM((1, R * TQ), F32),
            pltpu.VMEM((n_sel, TQ), F32),
            pltpu.VMEM((HEAD_DIM, R * TQ), F32),
        ],
        compiler_params=pltpu.CompilerParams(dimension_semantics=("arbitrary", "arbitrary", "arbitrary"),
                                             vmem_limit_bytes=VMEM_LIMIT),
        name="nsa",
    )(p1, p1, p1, cos_tab, sin_tab, ksx, vs_t, kw, vw_t, kcmp, vcmp_t)


def _tail_kernel(y_ref, qz_ref, mk_ref, mv_ref, h_ref, wout_ref, g_ref, o_ref, ymem_ref, acc_ref):
    for c in range(0, D_MODEL, COL_CHUNK):
        cs = slice(c, c + COL_CHUNK)
        acc_ref[:, cs] = h_ref[:, cs] + _dot(y_ref[...], wout_ref[0:NSA_W, cs])
    qm = qz_ref[:, 0:MEM_W]
    zm = qz_ref[:, MEM_W:2 * MEM_W]
    _memory_attention(qm, zm, mk_ref, mv_ref, ymem_ref, 0)
    for c in range(0, D_MODEL, COL_CHUNK):
        cs = slice(c, c + COL_CHUNK)
        acc_ref[:, cs] = acc_ref[:, cs] + _dot(ymem_ref[...], wout_ref[NSA_W:D_MODEL, cs])
    o_ref[...] = _rms_scale(acc_ref[...], g_ref[...])


def _tail(y_nsa, p1, seq, mem_k, mem_v, h2d, w_out, final_g):
    n = h2d.shape[0]
    T = TOK_TILE
    tiles_per_seq = seq // T
    return pl.pallas_call(
        _tail_kernel,
        grid=(n // T,),
        in_specs=[
            pl.BlockSpec((T, NSA_W), lambda i: (i, 0)),
            pl.BlockSpec((T, 2 * MEM_W), lambda i: (i, 2 * NSA_W // (2 * MEM_W))),
            pl.BlockSpec((None, MEM_LEN, MEM_W), lambda i: (i // tiles_per_seq, 0, 0)),
            pl.BlockSpec((None, MEM_LEN, MEM_W), lambda i: (i // tiles_per_seq, 0, 0)),
            pl.BlockSpec((T, D_MODEL), lambda i: (i, 0)),
            _resident((D_MODEL, D_MODEL)),
            _resident((1, D_MODEL)),
        ],
        out_specs=pl.BlockSpec((T, D_MODEL), lambda i: (i, 0)),
        out_shape=jax.ShapeDtypeStruct((n, D_MODEL), F32),
        scratch_shapes=[pltpu.VMEM((T, MEM_W), BF16), pltpu.VMEM((T, D_MODEL), F32)],
        compiler_params=pltpu.CompilerParams(dimension_semantics=("arbitrary",), vmem_limit_bytes=VMEM_LIMIT),
        name="tail",
    )(y_nsa, p1, mem_k, mem_v, h2d, w_out, final_g.reshape(1, D_MODEL))


def kernel(x, mem, positions, norm_g, mem_norm_g, w_mem_kv, w_out, a_w_in, a_w_pool, a_pool_scale, b_w_in,
           kv_norm_g, w_kv, cmp_pe, cmp_w1, cmp_w2, final_g):
    batch, seq, _ = x.shape
    assert seq % Q_TILE == 0 and seq % TOK_TILE == 0 and Q_TILE % SEL_LEN == 0
    n = batch * seq
    x2d = x.reshape(n, D_MODEL)

    mkv = _mem_kv(mem.reshape(batch * MEM_LEN, D_MODEL), mem_norm_g, w_mem_kv.astype(BF16))
    mkv = mkv.reshape(mkv.shape[0], batch, MEM_LEN, 2 * MEM_W)
    mem_k, mem_v = mkv[..., :MEM_W], mkv[..., MEM_W:]

    h1 = _layer0(x2d, seq, norm_g[0], a_w_in[0].astype(BF16), a_w_pool[0].astype(BF16), a_pool_scale[0],
                 mem_k[0], mem_v[0], w_out[0].astype(BF16))

    wb = b_w_in[0]
    o1 = NSA_W
    o2 = o1 + 3 * NSA_HEADS
    o3 = o2 + NSA_W
    wb = jnp.concatenate([wb[:, :o1].astype(BF16), wb[:, o2:].astype(BF16), wb[:, o1:o2].astype(BF16),
                          jnp.zeros((D_MODEL, GATE_PAD - 3 * NSA_HEADS), BF16)], axis=1)
    kvp, p1 = _proj1(h1, kv_norm_g, norm_g[1], w_kv.astype(BF16), wb)

    half = jnp.arange(ROT_HALF, dtype=F32)
    inv = ROPE_THETA ** (-half * 2.0 / ROT_DIM)
    pad = jnp.zeros((LANES - ROT_DIM,), F32)
    inv_full = jnp.concatenate([inv, inv, pad]).reshape(1, LANES)
    sgn_full = jnp.concatenate([-jnp.ones((ROT_HALF,), F32), jnp.ones((ROT_HALF,), F32), pad]).reshape(1, LANES)
    ksx, vs, kw, vw, cos_tab, sin_tab = _kv_rot(kvp, seq, positions.reshape(n, 1), inv_full, sgn_full)
    kcmp, vcmp = _compress(kvp, batch, seq, cos_tab, sin_tab, cmp_pe, cmp_w1.astype(BF16), cmp_w2.astype(BF16))

    y_nsa = _nsa(p1, batch, seq, cos_tab, sin_tab, ksx, vs, kw, vw, kcmp, vcmp)
    out = _tail(y_nsa, p1, seq, mem_k[1], mem_v[1], h1, w_out[1].astype(BF16), final_g)
    return out.reshape(batch, seq, D_MODEL)
```

```python
import functools

import jax
import jax.numpy as jnp
from jax import lax
from jax.experimental import pallas as pl
from jax.experimental.pallas import tpu as pltpu

D_MODEL = 2048
MEM_LEN = 256
HEAD_DIM = 128
MEM_HEADS = 4
MEM_W = MEM_HEADS * HEAD_DIM
POOL_W = D_MODEL - MEM_W
POOL_WINDOWS = (2, 4, 8, 16)
POOL_GC = POOL_W // len(POOL_WINDOWS)
POOL_HALO = 16
NSA_W = D_MODEL - MEM_W
NSA_HEADS = NSA_W // HEAD_DIM
NSA_KV_HEADS = 4
NSA_GROUP = NSA_HEADS // NSA_KV_HEADS
NSA_KV_W = NSA_KV_HEADS * HEAD_DIM
CMP_LEN = 32
CMP_STRIDE = 16
CMP_HID = 256
SEL_LEN = 64
SEL_TOPK = 16
WINDOW = 512
ROT_DIM = HEAD_DIM // 4
ROT_HALF = ROT_DIM // 2
ROPE_THETA = 500000.0
NORM_EPS = 1e-6
FORCE_SCORE = 1e4
NEG_INF = -1e30
SEL_BIAS = -1e9
QK_SCALE = HEAD_DIM ** -0.5
LOG2E = 1.4426950408889634

LANES = 128
SUBLANES = 8
TOK_TILE = 256
Q_TILE = 256
COL_CHUNK = 512
GATE_PAD = LANES
N_GATE = 3 * NSA_HEADS
VMEM_LIMIT = 56 * 1024 * 1024

F32 = jnp.float32
BF16 = jnp.bfloat16


def _resident(shape):
    nd = len(shape)
    return pl.BlockSpec(shape, lambda *_: (0,) * nd, pipeline_mode=pl.Buffered(1))


def _rms_scale(x, g):
    ms = jnp.mean(x * x, axis=-1, keepdims=True)
    return x * lax.rsqrt(ms + NORM_EPS) * g


def _silu(z):
    return z * (1.0 / (1.0 + jnp.exp(-z)))


def _dot(a, b):
    return jnp.dot(a, b, preferred_element_type=F32)


def _dot_nt(a, b):
    return lax.dot_general(a, b, (((1,), (1,)), ((), ())), preferred_element_type=F32)


def _memory_attention(qm, zm, k_ref, v_ref, y_ref, col0):
    for h in range(MEM_HEADS):
        cs = slice(h * HEAD_DIM, (h + 1) * HEAD_DIM)
        q = (qm[:, cs] * QK_SCALE).astype(BF16)
        s = _dot_nt(q, k_ref[:, cs])
        e = jnp.exp(s - jnp.max(s, axis=-1, keepdims=True))
        p = e / jnp.sum(e, axis=-1, keepdims=True)
        o = _dot(p.astype(BF16), v_ref[:, cs])
        y_ref[:, col0 + h * HEAD_DIM:col0 + (h + 1) * HEAD_DIM] = (o * _silu(zm[:, cs])).astype(y_ref.dtype)


def _mem_kv_kernel(mem_ref, g_ref, w_ref, o_ref):
    hn = _rms_scale(mem_ref[...], g_ref[...]).astype(BF16)
    o_ref[...] = _dot(hn, w_ref[...]).astype(o_ref.dtype)


def _mem_kv(mem2d, mem_norm_g, w_mem_kv_bf16):
    depth = w_mem_kv_bf16.shape[0]
    rows = mem2d.shape[0]
    T = TOK_TILE
    return pl.pallas_call(
        _mem_kv_kernel,
        grid=(depth, rows // T),
        in_specs=[
            pl.BlockSpec((T, D_MODEL), lambda l, i: (i, 0)),
            pl.BlockSpec((None, 1, D_MODEL), lambda l, i: (l, 0, 0)),
            pl.BlockSpec((None, D_MODEL, 2 * MEM_W), lambda l, i: (l, 0, 0)),
        ],
        out_specs=pl.BlockSpec((None, T, 2 * MEM_W), lambda l, i: (l, i, 0)),
        out_shape=jax.ShapeDtypeStruct((depth, rows, 2 * MEM_W), BF16),
        compiler_params=pltpu.CompilerParams(dimension_semantics=("arbitrary", "arbitrary"),
                                             vmem_limit_bytes=VMEM_LIMIT),
        name="mem_kv",
    )(mem2d, mem_norm_g.reshape(depth, 1, D_MODEL), w_mem_kv_bf16)


def _layer0_kernel(tiles_per_seq, x_ref, g_ref, win_ref, wpool_ref, scale_ref, mk_ref, mv_ref, wout_ref,
                   o_ref, hn_ref, proj_ref, uext_ref, y_ref):
    T = x_ref.shape[0]
    tb = pl.program_id(0) % tiles_per_seq
    x = x_ref[...]
    hn_ref[...] = _rms_scale(x, g_ref[...]).astype(BF16)

    @pl.when(tb == 0)
    def _():
        uext_ref[0:POOL_HALO, :] = jnp.zeros((POOL_HALO, POOL_W), F32)

    @pl.when(tb != 0)
    def _():
        uext_ref[0:POOL_HALO, :] = uext_ref[T:T + POOL_HALO, :]

    n_in = win_ref.shape[1]
    for c in range(0, n_in, COL_CHUNK):
        blk = _dot(hn_ref[...], win_ref[:, c:c + COL_CHUNK])
        if c < POOL_W:
            uext_ref[POOL_HALO:POOL_HALO + T, c:c + COL_CHUNK] = blk
        else:
            proj_ref[:, c - POOL_W:c - POOL_W + COL_CHUNK] = blk

    t1 = (tb * T + lax.broadcasted_iota(jnp.int32, (T, 1), 0) + 1).astype(F32)
    for gi, win in enumerate(POOL_WINDOWS):
        cs = slice(gi * POOL_GC, (gi + 1) * POOL_GC)
        u = uext_ref[POOL_HALO:POOL_HALO + T, cs]
        acc = u
        for k in range(1, win):
            acc = acc + uext_ref[POOL_HALO - k:POOL_HALO - k + T, cs]
        pooled = acc / jnp.minimum(t1, float(win)) - u
        mixed = _dot(pooled.astype(BF16), wpool_ref[gi])
        y_ref[:, cs] = (mixed * scale_ref[:, cs] * _silu(proj_ref[:, cs])).astype(BF16)

    for c in range(0, D_MODEL, COL_CHUNK):
        cs = slice(c, c + COL_CHUNK)
        o_ref[:, cs] = x_ref[:, cs] + _dot(y_ref[:, 0:POOL_W], wout_ref[0:POOL_W, cs])

    qm = proj_ref[:, POOL_W:POOL_W + MEM_W]
    zm = proj_ref[:, POOL_W + MEM_W:POOL_W + 2 * MEM_W]
    _memory_attention(qm, zm, mk_ref, mv_ref, y_ref, POOL_W)

    for c in range(0, D_MODEL, COL_CHUNK):
        cs = slice(c, c + COL_CHUNK)
        o_ref[:, cs] = o_ref[:, cs] + _dot(y_ref[:, POOL_W:D_MODEL], wout_ref[POOL_W:D_MODEL, cs])


def _layer0(x2d, seq, norm_g, w_in, w_pool, pool_scale, mem_k, mem_v, w_out):
    n = x2d.shape[0]
    T = TOK_TILE
    tiles_per_seq = seq // T
    n_in = w_in.shape[1]
    return pl.pallas_call(
        functools.partial(_layer0_kernel, tiles_per_seq),
        grid=(n // T,),
        in_specs=[
            pl.BlockSpec((T, D_MODEL), lambda i: (i, 0)),
            _resident((1, D_MODEL)),
            _resident((D_MODEL, n_in)),
            _resident(w_pool.shape),
            _resident((1, POOL_W)),
            pl.BlockSpec((None, MEM_LEN, MEM_W), lambda i: (i // tiles_per_seq, 0, 0)),
            pl.BlockSpec((None, MEM_LEN, MEM_W), lambda i: (i // tiles_per_seq, 0, 0)),
            _resident((D_MODEL, D_MODEL)),
        ],
        out_specs=pl.BlockSpec((T, D_MODEL), lambda i: (i, 0)),
        out_shape=jax.ShapeDtypeStruct((n, D_MODEL), F32),
        scratch_shapes=[
            pltpu.VMEM((T, D_MODEL), BF16),
            pltpu.VMEM((T, n_in - POOL_W), F32),
            pltpu.VMEM((T + POOL_HALO, POOL_W), F32),
            pltpu.VMEM((T, D_MODEL), BF16),
        ],
        compiler_params=pltpu.CompilerParams(dimension_semantics=("arbitrary",), vmem_limit_bytes=VMEM_LIMIT),
        name="layer0",
    )(x2d, norm_g.reshape(1, D_MODEL), w_in, w_pool, pool_scale.reshape(1, POOL_W), mem_k, mem_v, w_out)


def _proj1_kernel(h_ref, gkv_ref, g1_ref, wkv_ref, wq_ref, wr_ref, kv_ref, p1_ref, hn_ref, rest_ref):
    T = h_ref.shape[0]
    h = h_ref[...]
    ms = jnp.mean(h * h, axis=-1, keepdims=True)
    hs = h * lax.rsqrt(ms + NORM_EPS)
    hn_ref[...] = (hs * gkv_ref[...]).astype(BF16)
    n_kv = wkv_ref.shape[1]
    for c in range(0, n_kv, COL_CHUNK):
        kv_ref[:, c:c + COL_CHUNK] = _dot(hn_ref[...], wkv_ref[:, c:c + COL_CHUNK])
    hn_ref[...] = (hs * g1_ref[...]).astype(BF16)
    n_q = wq_ref.shape[1]
    for c in range(0, n_q, COL_CHUNK):
        p1_ref[:, c:c + COL_CHUNK] = _dot(hn_ref[...], wq_ref[:, c:c + COL_CHUNK])
    n_r = wr_ref.shape[1]
    for c in range(0, n_r, COL_CHUNK):
        w = min(COL_CHUNK, n_r - c)
        rest_ref[:, c:c + w] = _dot(hn_ref[...], wr_ref[:, c:c + w])
    shift = LANES - N_GATE
    lane = lax.broadcasted_iota(jnp.int32, (T, LANES), 1)
    n_tiles = n_r // LANES - 1
    first = rest_ref[:, 0:LANES]
    p1_ref[:, n_q + n_tiles * LANES:n_q + (n_tiles + 1) * LANES] = jnp.where(lane < N_GATE, first, 0.0)
    prev = pltpu.roll(first, shift, axis=1)
    for t in range(n_tiles):
        nxt = pltpu.roll(rest_ref[:, (t + 1) * LANES:(t + 2) * LANES], shift, axis=1)
        p1_ref[:, n_q + t * LANES:n_q + (t + 1) * LANES] = jnp.where(lane < shift, prev, nxt)
        prev = nxt


def _split_w1_kernel(w_ref, wq_ref, wr_ref):
    n_in, n_q, n_r = w_ref.shape[1], wq_ref.shape[1], wr_ref.shape[1]
    wq_ref[...] = w_ref[:, 0:n_q].astype(BF16)
    full = (n_in - n_q) // LANES * LANES
    wr_ref[:, 0:full] = w_ref[:, n_q:n_q + full].astype(BF16)
    wr_ref[:, full:n_r] = jnp.zeros((w_ref.shape[0], n_r - full), BF16)
    wr_ref[:, full:n_in - n_q] = w_ref[:, n_q + full:n_in].astype(BF16)


def _split_w1(w_in):
    n_in = w_in.shape[2]
    n_r = -(-(n_in - NSA_W) // LANES) * LANES
    rows = TOK_TILE
    return pl.pallas_call(
        _split_w1_kernel,
        grid=(D_MODEL // rows,),
        in_specs=[pl.BlockSpec((None, rows, n_in), lambda i: (0, i, 0))],
        out_specs=[pl.BlockSpec((rows, NSA_W), lambda i: (i, 0)), pl.BlockSpec((rows, n_r), lambda i: (i, 0))],
        out_shape=[jax.ShapeDtypeStruct((D_MODEL, NSA_W), BF16), jax.ShapeDtypeStruct((D_MODEL, n_r), BF16)],
        compiler_params=pltpu.CompilerParams(dimension_semantics=("arbitrary",), vmem_limit_bytes=VMEM_LIMIT),
        name="split_w1",
    )(w_in)


def _proj1(h2d, kv_norm_g, norm_g1, w_kv, w_q, w_rest):
    n = h2d.shape[0]
    T = TOK_TILE
    n_kv, n_q, n_r = w_kv.shape[1], w_q.shape[1], w_rest.shape[1]
    assert n_r % LANES == 0 and n_q % COL_CHUNK == 0
    n_1 = n_q + n_r
    return pl.pallas_call(
        _proj1_kernel,
        grid=(n // T,),
        in_specs=[
            pl.BlockSpec((T, D_MODEL), lambda i: (i, 0)),
            _resident((1, D_MODEL)),
            _resident((1, D_MODEL)),
            _resident((D_MODEL, n_kv)),
            _resident((D_MODEL, n_q)),
            _resident((D_MODEL, n_r)),
        ],
        out_specs=[
            pl.BlockSpec((T, n_kv), lambda i: (i, 0)),
            pl.BlockSpec((T, n_1), lambda i: (i, 0)),
        ],
        out_shape=[
            jax.ShapeDtypeStruct((n, n_kv), F32),
            jax.ShapeDtypeStruct((n, n_1), F32),
        ],
        scratch_shapes=[pltpu.VMEM((T, D_MODEL), BF16), pltpu.VMEM((T, n_r), F32)],
        compiler_params=pltpu.CompilerParams(dimension_semantics=("arbitrary",), vmem_limit_bytes=VMEM_LIMIT),
        name="proj1",
    )(h2d, kv_norm_g.reshape(1, D_MODEL), norm_g1.reshape(1, D_MODEL), w_kv, w_q, w_rest)


def _rope(x, cos_t, sin_t):
    lane = lax.broadcasted_iota(jnp.int32, x.shape, 1)
    swapped = jnp.where(lane < ROT_HALF, pltpu.roll(x, LANES - ROT_HALF, axis=1), pltpu.roll(x, ROT_HALF, axis=1))
    return x * cos_t + swapped * sin_t


KV_ROW_TILE = 512


def _kv_rot_kernel(seq, ksv_ref, kwv_ref, pos_ref, inv_ref, sgn_ref, ksx_ref, vso_ref, kwo_ref, vwo_ref, cos_ref,
                   sin_ref):
    rows = ksv_ref.shape[0]
    ang = pos_ref[...].astype(F32) * inv_ref[...]
    c = jnp.cos(ang)
    s = jnp.sin(ang) * sgn_ref[...]
    cos_ref[...] = c
    sin_ref[...] = s
    r0 = (pl.program_id(0) * rows) % seq
    blk = (r0 + lax.broadcasted_iota(jnp.int32, (rows, LANES), 0)) // SEL_LEN
    lane = lax.broadcasted_iota(jnp.int32, (rows, LANES), 1)
    ind = jnp.where(blk == lane, 1.0, 0.0).astype(BF16)
    for h in range(NSA_KV_HEADS):
        hs = slice(h * HEAD_DIM, (h + 1) * HEAD_DIM)
        vsl = slice(NSA_KV_W + h * HEAD_DIM, NSA_KV_W + (h + 1) * HEAD_DIM)
        ksx_ref[:, 2 * h * HEAD_DIM:(2 * h + 1) * HEAD_DIM] = _rope(ksv_ref[:, hs], c, s).astype(BF16)
        ksx_ref[:, (2 * h + 1) * HEAD_DIM:(2 * h + 2) * HEAD_DIM] = ind
        kwo_ref[:, hs] = _rope(kwv_ref[:, hs], c, s).astype(BF16)
        vs_t = ksv_ref[:, vsl].T.astype(BF16)
        vw_t = kwv_ref[:, vsl].T.astype(BF16)
        for j in range(rows // Q_TILE):
            vso_ref[h, j] = vs_t[:, j * Q_TILE:(j + 1) * Q_TILE]
            vwo_ref[h, j] = vw_t[:, j * Q_TILE:(j + 1) * Q_TILE]


def _kv_rot(kvp, seq, pos_col, inv_full, sgn_full):
    n = kvp.shape[0]
    T = KV_ROW_TILE
    pair_w = 2 * NSA_KV_W
    row = lambda w: pl.BlockSpec((T, w), lambda i: (i, 0))
    vt_shape = (NSA_KV_HEADS, n // Q_TILE, HEAD_DIM, Q_TILE)
    vt_spec = pl.BlockSpec((NSA_KV_HEADS, T // Q_TILE, HEAD_DIM, Q_TILE), lambda i: (0, i, 0, 0))
    return pl.pallas_call(
        functools.partial(_kv_rot_kernel, seq),
        grid=(n // T,),
        in_specs=[
            pl.BlockSpec((T, pair_w), lambda i: (i, 1)),
            pl.BlockSpec((T, pair_w), lambda i: (i, 2)),
            row(1),
            pl.BlockSpec((1, LANES), lambda i: (0, 0)),
            pl.BlockSpec((1, LANES), lambda i: (0, 0)),
        ],
        out_specs=[row(2 * NSA_KV_W), vt_spec, row(NSA_KV_W), vt_spec, row(LANES), row(LANES)],
        out_shape=[
            jax.ShapeDtypeStruct((n, 2 * NSA_KV_W), BF16),
            jax.ShapeDtypeStruct(vt_shape, BF16),
            jax.ShapeDtypeStruct((n, NSA_KV_W), BF16),
            jax.ShapeDtypeStruct(vt_shape, BF16),
            jax.ShapeDtypeStruct((n, LANES), F32),
            jax.ShapeDtypeStruct((n, LANES), F32),
        ],
        compiler_params=pltpu.CompilerParams(dimension_semantics=("arbitrary",), vmem_limit_bytes=VMEM_LIMIT),
        name="kv_rot",
    )(kvp, kvp, pos_col, inv_full, sgn_full)


def _compress_kernel(kc_ref, vc_ref, cos_ref, sin_ref, pe_ref, w1_ref, w2_ref, kcmp_ref, vcmp_ref):
    S = kc_ref.shape[0]
    n_half = S // CMP_STRIDE
    assert CMP_LEN == 2 * CMP_STRIDE
    for part, (src_ref, dst_ref) in enumerate(((kc_ref, kcmp_ref), (vc_ref, vcmp_ref))):
        acc_first = jnp.zeros((n_half, CMP_HID), F32)
        acc_second = jnp.zeros((n_half, CMP_HID), F32)
        for l in range(CMP_STRIDE):
            z = src_ref[pl.ds(l, n_half, stride=CMP_STRIDE), :]
            z1 = (z + pe_ref[part, l:l + 1, :]).astype(BF16)
            acc_first = acc_first + _dot(z1, w1_ref[part, l * HEAD_DIM:(l + 1) * HEAD_DIM, :])
            l2 = l + CMP_STRIDE
            z2 = (z + pe_ref[part, l2:l2 + 1, :]).astype(BF16)
            acc_second = acc_second + _dot(z2, w1_ref[part, l2 * HEAD_DIM:(l2 + 1) * HEAD_DIM, :])
        pre = acc_first + pltpu.roll(acc_second, n_half - 1, axis=0)
        out = _dot(_silu(pre).astype(BF16), w2_ref[part])
        if part == 0:
            c_end = pltpu.roll(cos_ref[pl.ds(CMP_STRIDE - 1, n_half, stride=CMP_STRIDE), :], n_half - 1, axis=0)
            s_end = pltpu.roll(sin_ref[pl.ds(CMP_STRIDE - 1, n_half, stride=CMP_STRIDE), :], n_half - 1, axis=0)
            out = _rope(out, c_end, s_end)
            dst_ref[...] = out.astype(BF16)
        else:
            dst_ref[...] = out.T.astype(BF16)


def _compress(kvp, batch, seq, cos_tab, sin_tab, cmp_pe, cmp_w1, cmp_w2):
    n_half = seq // CMP_STRIDE
    cmp_out = pl.BlockSpec((None, None, n_half, HEAD_DIM), lambda b, g: (b, g, 0, 0))
    tab = pl.BlockSpec((seq, LANES), lambda b, g: (b, 0))
    return pl.pallas_call(
        _compress_kernel,
        grid=(batch, NSA_KV_HEADS),
        in_specs=[
            pl.BlockSpec((seq, HEAD_DIM), lambda b, g: (b, g)),
            pl.BlockSpec((seq, HEAD_DIM), lambda b, g: (b, NSA_KV_HEADS + g)),
            tab, tab,
            _resident(cmp_pe.shape), _resident(cmp_w1.shape), _resident(cmp_w2.shape),
        ],
        out_specs=[cmp_out, pl.BlockSpec((None, None, HEAD_DIM, n_half), lambda b, g: (b, g, 0, 0))],
        out_shape=[
            jax.ShapeDtypeStruct((batch, NSA_KV_HEADS, n_half, HEAD_DIM), BF16),
            jax.ShapeDtypeStruct((batch, NSA_KV_HEADS, HEAD_DIM, n_half), BF16),
        ],
        compiler_params=pltpu.CompilerParams(dimension_semantics=("arbitrary", "arbitrary"),
                                             vmem_limit_bytes=VMEM_LIMIT),
        name="compress",
    )(kvp, kvp, cos_tab, sin_tab, cmp_pe, cmp_w1, cmp_w2)


def _nsa_kernel(q_ref, zq_ref, gl_ref, cos_ref, sin_ref, ksx_ref, vst_ref, kw_ref, vwt_ref, kcmp_ref, vcmpt_ref,
                o_ref, qxt_ref, sc_ref, m_ref, l_ref, acc_ref, ocmp_ref, osel_ref, s_ref, p_ref, alpha_ref,
                rank_ref, owin_ref):
    TQ = q_ref.shape[0]
    R = NSA_GROUP
    M = R * TQ
    n_cmp_pad = kcmp_ref.shape[0]
    n_sel = ksx_ref.shape[0] // SEL_LEN
    g = pl.program_id(1)
    qi = pl.program_id(2)
    t0 = qi * TQ

    cos_t = cos_ref[...]
    sin_t = sin_ref[...]
    for r in range(R):
        qr = _rope(q_ref[:, r * HEAD_DIM:(r + 1) * HEAD_DIM], cos_t, sin_t) * (QK_SCALE * LOG2E)
        qxt_ref[0:HEAD_DIM, r * TQ:(r + 1) * TQ] = qr.T.astype(BF16)
    qxt_ref[HEAD_DIM + n_sel:2 * HEAD_DIM, :] = jnp.zeros((HEAD_DIM - n_sel, M), BF16)

    lane_q = lax.broadcasted_iota(jnp.int32, (1, M), 1) % TQ
    q_t = qxt_ref[0:HEAD_DIM, :]
    slabs = [slice(r * TQ, (r + 1) * TQ) for r in range(R)]
    key_sub = lax.broadcasted_iota(jnp.int32, (TQ, TQ), 0)
    q_lane = lax.broadcasted_iota(jnp.int32, (TQ, TQ), 1)
    causal = key_sub <= q_lane

    assert WINDOW == 2 * TQ
    w_tiles = (jnp.maximum(qi - 2, 0), jnp.maximum(qi - 1, 0), qi)
    w_masks = ((key_sub > q_lane) & (qi >= 2), (key_sub >= 0) & (qi >= 1), causal)
    w_keys = [kw_ref[pl.ds(pl.multiple_of(kt * TQ, TQ), TQ), :] for kt in w_tiles]
    w_scores = [[_dot(k, qxt_ref[0:HEAD_DIM, rs]) for k in w_keys] for rs in slabs]

    s_t = _dot(kcmp_ref[...], q_t)
    cmp_end = lax.broadcasted_iota(jnp.int32, (n_cmp_pad, 1), 0) * CMP_STRIDE + (CMP_LEN - 1)
    valid = cmp_end <= t0 + lane_q
    s_t = jnp.where(valid, s_t, NEG_INF)
    e = jnp.where(valid, jnp.exp2(s_t - jnp.max(s_t, axis=0, keepdims=True)), 0.0)
    p_cmp = e * (1.0 / jnp.maximum(jnp.sum(e, axis=0, keepdims=True), 1e-30))
    ocmp_ref[...] = _dot(vcmpt_ref[...], p_cmp.astype(BF16))

    w_probs = []
    w_inv_l = []
    for s_head in w_scores:
        s_head = [jnp.where(mk, s, NEG_INF) for s, mk in zip(s_head, w_masks)]
        m_col = jnp.max(s_head[0], axis=0, keepdims=True)
        for s in s_head[1:]:
            m_col = jnp.maximum(m_col, jnp.max(s, axis=0, keepdims=True))
        p_head = [jnp.exp2(s - m_col) for s in s_head]
        l_col = jnp.sum(p_head[0], axis=0, keepdims=True)
        for p in p_head[1:]:
            l_col = l_col + jnp.sum(p, axis=0, keepdims=True)
        w_inv_l.append(1.0 / l_col)
        w_probs.append([p.astype(BF16) for p in p_head])

    p_sum = p_cmp[:, 0:TQ]
    for r in range(1, R):
        p_sum = p_sum + p_cmp[:, r * TQ:(r + 1) * TQ]
    jn = lax.broadcasted_iota(jnp.int32, (n_sel, n_cmp_pad), 0) * SEL_LEN
    cn = lax.broadcasted_iota(jnp.int32, (n_sel, n_cmp_pad), 1) * CMP_STRIDE
    ov = jnp.minimum(cn + (CMP_LEN - 1), jn + (SEL_LEN - 1)) - jnp.maximum(cn, jn) + 1
    ov_t = (jnp.maximum(ov, 0).astype(F32) / CMP_LEN).astype(BF16)
    p_hi = p_sum.astype(BF16)
    p_lo = (p_sum - p_hi.astype(F32)).astype(BF16)
    imp = _dot(ov_t, p_hi) + _dot(ov_t, p_lo)

    for rs, p_head, inv_l in zip(slabs, w_probs, w_inv_l):
        o = _dot(vwt_ref[w_tiles[0]], p_head[0])
        for kt, p in zip(w_tiles[1:], p_head[1:]):
            o = o + _dot(vwt_ref[kt], p)
        owin_ref[:, rs] = o * inv_l

    jb = lax.broadcasted_iota(jnp.int32, (n_sel, TQ), 0)
    tq = t0 + lax.broadcasted_iota(jnp.int32, (n_sel, TQ), 1)
    cur = tq // SEL_LEN
    forced = (jb == 0) | (jb == cur) | (jb == cur - 1)
    ok = jb * SEL_LEN <= tq
    score = jnp.where(ok, jnp.where(forced, FORCE_SCORE, imp), NEG_INF)
    sc_ref[...] = score

    rank_ref[...] = jnp.zeros((n_sel, TQ), F32)
    blocks_per_tile = TQ // SEL_LEN
    for kt in range(n_sel // blocks_per_tile):
        @pl.when(kt <= qi)
        def _(kt=kt):
            others = [sc_ref[j2:j2 + 1, :] for j2 in range(kt * blocks_per_tile, (kt + 1) * blocks_per_tile)]
            for gi in range(n_sel // SUBLANES):
                rows = slice(gi * SUBLANES, (gi + 1) * SUBLANES)
                sg = sc_ref[rows, :]
                cnt = rank_ref[rows, :]
                for u, other in enumerate(others):
                    j2 = kt * blocks_per_tile + u
                    if gi * SUBLANES > j2:
                        before = other >= sg
                    elif (gi + 1) * SUBLANES - 1 < j2:
                        before = other > sg
                    else:
                        row_j = gi * SUBLANES + lax.broadcasted_iota(jnp.int32, (SUBLANES, TQ), 0)
                        before = (other > sg) | ((other == sg) & (row_j > j2))
                    cnt = cnt + jnp.where(before, 1.0, 0.0)
                rank_ref[rows, :] = cnt
    n_top = min(SEL_TOPK, n_sel)
    bias_t = jnp.where(rank_ref[...] < n_top, 0.0, SEL_BIAS).astype(BF16)
    for r in range(R):
        qxt_ref[HEAD_DIM:HEAD_DIM + n_sel, r * TQ:(r + 1) * TQ] = bias_t

    diag = pl.ds(pl.multiple_of(t0, TQ), TQ)

    def sel_scores(k_tile):
        return [_dot(k_tile, qxt_ref[:, rs]) for rs in slabs]

    def sel_pv(v_t):
        return [_dot(v_t, p_ref[:, rs]) for rs in slabs]

    n_full = qi
    s_diag = sel_scores(ksx_ref[diag, :])
    s_first = sel_scores(ksx_ref[0:TQ, :])
    for rs, s_t in zip(slabs, s_diag):
        s_t = jnp.where(causal, s_t, NEG_INF)
        m_tile = jnp.max(s_t, axis=0, keepdims=True)
        p = jnp.exp2(s_t - m_tile)
        m_ref[:, rs] = m_tile
        l_ref[:, rs] = jnp.sum(p, axis=0, keepdims=True)
        p_ref[:, rs] = p.astype(BF16)
    alpha_ref[...] = jnp.ones((1, M), F32)
    acc_ref[...] = jnp.zeros((HEAD_DIM, M), F32)
    for rs, s_t in zip(slabs, s_first):
        s_ref[:, rs] = s_t

    def sel_step(kt):
        alpha_prev = alpha_ref[...]
        nxt = jnp.minimum(kt + 1, n_full - 1)
        s_next = sel_scores(ksx_ref[pl.ds(pl.multiple_of(nxt * TQ, TQ), TQ), :])
        prev = jnp.where(kt == 0, qi, kt - 1)
        pv_prev = sel_pv(vst_ref[prev])
        for rs in slabs:
            s_t = s_ref[:, rs]
            m_prev = m_ref[:, rs]
            m_new = jnp.maximum(m_prev, jnp.max(s_t, axis=0, keepdims=True))
            alpha = jnp.exp2(m_prev - m_new)
            p = jnp.exp2(s_t - m_new)
            m_ref[:, rs] = m_new
            l_ref[:, rs] = alpha * l_ref[:, rs] + jnp.sum(p, axis=0, keepdims=True)
            alpha_ref[:, rs] = alpha
            p_ref[:, rs] = p.astype(BF16)
        for rs, o in zip(slabs, pv_prev):
            acc_ref[:, rs] = alpha_prev[:, rs] * acc_ref[:, rs] + o
        for rs, s_t in zip(slabs, s_next):
            s_ref[:, rs] = s_t

    def sel_pair(j, carry):
        sel_step(2 * j)
        sel_step(2 * j + 1)
        return carry
    lax.fori_loop(0, lax.shift_right_logical(n_full, 1), sel_pair, 0)

    @pl.when(n_full % 2 == 1)
    def _():
        sel_step(n_full - 1)

    last = jnp.where(n_full == 0, qi, n_full - 1)
    inv_l = 1.0 / l_ref[...]
    for rs, o in zip(slabs, sel_pv(vst_ref[last])):
        osel_ref[:, rs] = (alpha_ref[:, rs] * acc_ref[:, rs] + o) * inv_l[:, rs]

    o_win = owin_ref[...]

    gates = 1.0 / (1.0 + jnp.exp(-gl_ref[...]))
    gates_t = pltpu.roll(gates, (LANES - 3 * R * g) % LANES, axis=1).T
    for r in range(R):
        rs = slice(r * TQ, (r + 1) * TQ)
        mix_t = (gates_t[3 * r:3 * r + 1, :] * ocmp_ref[:, rs] + gates_t[3 * r + 1:3 * r + 2, :] * osel_ref[:, rs]
                 + gates_t[3 * r + 2:3 * r + 3, :] * o_win[:, rs])
        cs = slice(r * HEAD_DIM, (r + 1) * HEAD_DIM)
        o_ref[:, cs] = (mix_t.T * _silu(zq_ref[:, cs])).astype(o_ref.dtype)


def _nsa(p1, batch, seq, cos_tab, sin_tab, ksx, vs_t, kw, vw_t, kcmp, vcmp_t):
    n = p1.shape[0]
    TQ = Q_TILE
    R = NSA_GROUP
    q_tiles = seq // TQ
    gw = R * HEAD_DIM
    n_cmp_pad = kcmp.shape[2]
    n_sel = seq // SEL_LEN
    tok = lambda b, g, i: b * q_tiles + i
    head_kv = pl.BlockSpec((seq, HEAD_DIM), lambda b, g, i: (b, g))
    head_vt = pl.BlockSpec((None, q_tiles, HEAD_DIM, TQ), lambda b, g, i: (g, b, 0, 0))
    cmp_kv = pl.BlockSpec((None, None, n_cmp_pad, HEAD_DIM), lambda b, g, i: (b, g, 0, 0))
    cmp_vt = pl.BlockSpec((None, None, HEAD_DIM, n_cmp_pad), lambda b, g, i: (b, g, 0, 0))
    return pl.pallas_call(
        _nsa_kernel,
        grid=(batch, NSA_KV_HEADS, q_tiles),
        in_specs=[
            pl.BlockSpec((TQ, gw), lambda b, g, i: (tok(b, g, i), g)),
            pl.BlockSpec((TQ, gw), lambda b, g, i: (tok(b, g, i), NSA_W // gw + g)),
            pl.BlockSpec((TQ, GATE_PAD), lambda b, g, i: (tok(b, g, i), (2 * NSA_W + 2 * MEM_W) // GATE_PAD)),
            pl.BlockSpec((TQ, LANES), lambda b, g, i: (tok(b, g, i), 0)),
            pl.BlockSpec((TQ, LANES), lambda b, g, i: (tok(b, g, i), 0)),
            pl.BlockSpec((seq, 2 * HEAD_DIM), lambda b, g, i: (b, g)),
            head_vt, head_kv, head_vt, cmp_kv, cmp_vt,
        ],
        out_specs=pl.BlockSpec((TQ, gw), lambda b, g, i: (tok(b, g, i), g)),
        out_shape=jax.ShapeDtypeStruct((n, NSA_W), BF16),
        scratch_shapes=[
            pltpu.VMEM((2 * HEAD_DIM, R * TQ), BF16),
            pltpu.VMEM((n_sel, TQ), F32),
            pltpu.VMEM((1, R * TQ), F32),
            pltpu.VMEM((1, R * TQ), F32),
            pltpu.VMEM((HEAD_DIM, R * TQ), F32),
            pltpu.VMEM((HEAD_DIM, R * TQ), F32),
            pltpu.VMEM((HEAD_DIM, R * TQ), F32),
            pltpu.VMEM((TQ, R * TQ), F32),
            pltpu.VMEM((TQ, R * TQ), BF16),
            pltpu.VMEM((1, R * TQ), F32),
            pltpu.VMEM((n_sel, TQ), F32),
            pltpu.VMEM((HEAD_DIM, R * TQ), F32),
        ],
        compiler_params=pltpu.CompilerParams(dimension_semantics=("arbitrary", "arbitrary", "arbitrary"),
                                             vmem_limit_bytes=VMEM_LIMIT),
        name="nsa",
    )(p1, p1, p1, cos_tab, sin_tab, ksx, vs_t, kw, vw_t, kcmp, vcmp_t)


def _tail_kernel(y_ref, qz_ref, mk_ref, mv_ref, h_ref, wout_ref, g_ref, o_ref, ymem_ref, acc_ref):
    for c in range(0, D_MODEL, COL_CHUNK):
        cs = slice(c, c + COL_CHUNK)
        acc_ref[:, cs] = h_ref[:, cs] + _dot(y_ref[...], wout_ref[0:NSA_W, cs])
    qm = qz_ref[:, 0:MEM_W]
    zm = qz_ref[:, MEM_W:2 * MEM_W]
    _memory_attention(qm, zm, mk_ref, mv_ref, ymem_ref, 0)
    for c in range(0, D_MODEL, COL_CHUNK):
        cs = slice(c, c + COL_CHUNK)
        acc_ref[:, cs] = acc_ref[:, cs] + _dot(ymem_ref[...], wout_ref[NSA_W:D_MODEL, cs])
    o_ref[...] = _rms_scale(acc_ref[...], g_ref[...])


def _tail(y_nsa, p1, seq, mem_k, mem_v, h2d, w_out, final_g):
    n = h2d.shape[0]
    T = TOK_TILE
    tiles_per_seq = seq // T
    return pl.pallas_call(
        _tail_kernel,
        grid=(n // T,),
        in_specs=[
            pl.BlockSpec((T, NSA_W), lambda i: (i, 0)),
            pl.BlockSpec((T, 2 * MEM_W), lambda i: (i, 2 * NSA_W // (2 * MEM_W))),
            pl.BlockSpec((None, MEM_LEN, MEM_W), lambda i: (i // tiles_per_seq, 0, 0)),
            pl.BlockSpec((None, MEM_LEN, MEM_W), lambda i: (i // tiles_per_seq, 0, 0)),
            pl.BlockSpec((T, D_MODEL), lambda i: (i, 0)),
            _resident((D_MODEL, D_MODEL)),
            _resident((1, D_MODEL)),
        ],
        out_specs=pl.BlockSpec((T, D_MODEL), lambda i: (i, 0)),
        out_shape=jax.ShapeDtypeStruct((n, D_MODEL), F32),
        scratch_shapes=[pltpu.VMEM((T, MEM_W), BF16), pltpu.VMEM((T, D_MODEL), F32)],
        compiler_params=pltpu.CompilerParams(dimension_semantics=("arbitrary",), vmem_limit_bytes=VMEM_LIMIT),
        name="tail",
    )(y_nsa, p1, mem_k, mem_v, h2d, w_out, final_g.reshape(1, D_MODEL))


def kernel(x, mem, positions, norm_g, mem_norm_g, w_mem_kv, w_out, a_w_in, a_w_pool, a_pool_scale, b_w_in,
           kv_norm_g, w_kv, cmp_pe, cmp_w1, cmp_w2, final_g):
    batch, seq, _ = x.shape
    assert seq % Q_TILE == 0 and seq % TOK_TILE == 0 and Q_TILE % SEL_LEN == 0
    n = batch * seq
    x2d = x.reshape(n, D_MODEL)

    mkv = _mem_kv(mem.reshape(batch * MEM_LEN, D_MODEL), mem_norm_g, w_mem_kv.astype(BF16))
    mkv = mkv.reshape(mkv.shape[0], batch, MEM_LEN, 2 * MEM_W)
    mem_k, mem_v = mkv[..., :MEM_W], mkv[..., MEM_W:]

    h1 = _layer0(x2d, seq, norm_g[0], a_w_in[0].astype(BF16), a_w_pool[0].astype(BF16), a_pool_scale[0],
                 mem_k[0], mem_v[0], w_out[0].astype(BF16))

    w_q, w_rest = _split_w1(b_w_in)
    kvp, p1 = _proj1(h1, kv_norm_g, norm_g[1], w_kv.astype(BF16), w_q, w_rest)

    half = jnp.arange(ROT_HALF, dtype=F32)
    inv = ROPE_THETA ** (-half * 2.0 / ROT_DIM)
    pad = jnp.zeros((LANES - ROT_DIM,), F32)
    inv_full = jnp.concatenate([inv, inv, pad]).reshape(1, LANES)
    sgn_full = jnp.concatenate([-jnp.ones((ROT_HALF,), F32), jnp.ones((ROT_HALF,), F32), pad]).reshape(1, LANES)
    ksx, vs, kw, vw, cos_tab, sin_tab = _kv_rot(kvp, seq, positions.reshape(n, 1), inv_full, sgn_full)
    kcmp, vcmp = _compress(kvp, batch, seq, cos_tab, sin_tab, cmp_pe, cmp_w1.astype(BF16), cmp_w2.astype(BF16))

    y_nsa = _nsa(p1, batch, seq, cos_tab, sin_tab, ksx, vs, kw, vw, kcmp, vcmp)
    out = _tail(y_nsa, p1, seq, mem_k[1], mem_v[1], h1, w_out[1].astype(BF16), final_g)
    return out.reshape(batch, seq, D_MODEL)
```

```python
import functools

import jax
import jax.numpy as jnp
from jax import lax
from jax.experimental import pallas as pl
from jax.experimental.pallas import tpu as pltpu

D_MODEL = 2048
MEM_LEN = 256
HEAD_DIM = 128
MEM_HEADS = 4
MEM_W = MEM_HEADS * HEAD_DIM
POOL_W = D_MODEL - MEM_W
POOL_WINDOWS = (2, 4, 8, 16)
POOL_GC = POOL_W // len(POOL_WINDOWS)
POOL_HALO = 16
NSA_W = D_MODEL - MEM_W
NSA_HEADS = NSA_W // HEAD_DIM
NSA_KV_HEADS = 4
NSA_GROUP = NSA_HEADS // NSA_KV_HEADS
NSA_KV_W = NSA_KV_HEADS * HEAD_DIM
CMP_LEN = 32
CMP_STRIDE = 16
CMP_HID = 256
SEL_LEN = 64
SEL_TOPK = 16
WINDOW = 512
ROT_DIM = HEAD_DIM // 4
ROT_HALF = ROT_DIM // 2
ROPE_THETA = 500000.0
NORM_EPS = 1e-6
FORCE_SCORE = 1e4
NEG_INF = -1e30
SEL_BIAS = -1e9
QK_SCALE = HEAD_DIM ** -0.5
LOG2E = 1.4426950408889634

LANES = 128
SUBLANES = 8
TOK_TILE = 256
Q_TILE = 256
COL_CHUNK = 512
GATE_PAD = LANES
N_GATE = 3 * NSA_HEADS
VMEM_LIMIT = 56 * 1024 * 1024

F32 = jnp.float32
BF16 = jnp.bfloat16


def _resident(shape):
    nd = len(shape)
    return pl.BlockSpec(shape, lambda *_: (0,) * nd, pipeline_mode=pl.Buffered(1))


def _rms_scale(x, g):
    ms = jnp.mean(x * x, axis=-1, keepdims=True)
    return x * lax.rsqrt(ms + NORM_EPS) * g


def _silu(z):
    return z * (1.0 / (1.0 + jnp.exp(-z)))


def _dot(a, b):
    return jnp.dot(a, b, preferred_element_type=F32)


def _dot_nt(a, b):
    return lax.dot_general(a, b, (((1,), (1,)), ((), ())), preferred_element_type=F32)


def _memory_scores(qm, k_ref):
    scores = []
    for h in range(MEM_HEADS):
        cs = slice(h * HEAD_DIM, (h + 1) * HEAD_DIM)
        scores.append(_dot_nt((qm[:, cs] * QK_SCALE).astype(BF16), k_ref[:, cs]))
    return scores


def _memory_finish(scores, zm, v_ref, y_ref, col0):
    for h, s in enumerate(scores):
        cs = slice(h * HEAD_DIM, (h + 1) * HEAD_DIM)
        e = jnp.exp(s - jnp.max(s, axis=-1, keepdims=True))
        p = e / jnp.sum(e, axis=-1, keepdims=True)
        o = _dot(p.astype(BF16), v_ref[:, cs])
        y_ref[:, col0 + h * HEAD_DIM:col0 + (h + 1) * HEAD_DIM] = (o * _silu(zm[:, cs])).astype(y_ref.dtype)


def _mem_kv_kernel(mem_ref, g_ref, w_ref, o_ref):
    hn = _rms_scale(mem_ref[...], g_ref[...]).astype(BF16)
    o_ref[...] = _dot(hn, w_ref[...]).astype(o_ref.dtype)


def _mem_kv(mem2d, mem_norm_g, w_mem_kv_bf16):
    depth = w_mem_kv_bf16.shape[0]
    rows = mem2d.shape[0]
    T = TOK_TILE
    return pl.pallas_call(
        _mem_kv_kernel,
        grid=(depth, rows // T),
        in_specs=[
            pl.BlockSpec((T, D_MODEL), lambda l, i: (i, 0)),
            pl.BlockSpec((None, 1, D_MODEL), lambda l, i: (l, 0, 0)),
            pl.BlockSpec((None, D_MODEL, 2 * MEM_W), lambda l, i: (l, 0, 0)),
        ],
        out_specs=pl.BlockSpec((None, T, 2 * MEM_W), lambda l, i: (l, i, 0)),
        out_shape=jax.ShapeDtypeStruct((depth, rows, 2 * MEM_W), BF16),
        compiler_params=pltpu.CompilerParams(dimension_semantics=("arbitrary", "arbitrary"),
                                             vmem_limit_bytes=VMEM_LIMIT),
        name="mem_kv",
    )(mem2d, mem_norm_g.reshape(depth, 1, D_MODEL), w_mem_kv_bf16)


def _layer0_kernel(tiles_per_seq, x_ref, g_ref, win_ref, wpool_ref, scale_ref, mk_ref, mv_ref, wout_ref,
                   o_ref, hn_ref, proj_ref, uext_ref, y_ref):
    T = x_ref.shape[0]
    tb = pl.program_id(0) % tiles_per_seq
    x = x_ref[...]
    hn_ref[...] = _rms_scale(x, g_ref[...]).astype(BF16)

    @pl.when(tb == 0)
    def _():
        uext_ref[0:POOL_HALO, :] = jnp.zeros((POOL_HALO, POOL_W), F32)

    @pl.when(tb != 0)
    def _():
        uext_ref[0:POOL_HALO, :] = uext_ref[T:T + POOL_HALO, :]

    n_in = win_ref.shape[1]
    for c in range(0, n_in, COL_CHUNK):
        blk = _dot(hn_ref[...], win_ref[:, c:c + COL_CHUNK])
        if c < POOL_W:
            uext_ref[POOL_HALO:POOL_HALO + T, c:c + COL_CHUNK] = blk
        else:
            proj_ref[:, c - POOL_W:c - POOL_W + COL_CHUNK] = blk

    t1 = (tb * T + lax.broadcasted_iota(jnp.int32, (T, 1), 0) + 1).astype(F32)
    for gi, win in enumerate(POOL_WINDOWS):
        cs = slice(gi * POOL_GC, (gi + 1) * POOL_GC)
        u = uext_ref[POOL_HALO:POOL_HALO + T, cs]
        acc = u
        for k in range(1, win):
            acc = acc + uext_ref[POOL_HALO - k:POOL_HALO - k + T, cs]
        pooled = acc / jnp.minimum(t1, float(win)) - u
        mixed = _dot(pooled.astype(BF16), wpool_ref[gi])
        y_ref[:, cs] = (mixed * scale_ref[:, cs] * _silu(proj_ref[:, cs])).astype(BF16)

    mem_scores = _memory_scores(proj_ref[:, POOL_W:POOL_W + MEM_W], mk_ref)
    for c in range(0, D_MODEL, COL_CHUNK):
        cs = slice(c, c + COL_CHUNK)
        o_ref[:, cs] = x_ref[:, cs] + _dot(y_ref[:, 0:POOL_W], wout_ref[0:POOL_W, cs])

    zm = proj_ref[:, POOL_W + MEM_W:POOL_W + 2 * MEM_W]
    _memory_finish(mem_scores, zm, mv_ref, y_ref, POOL_W)

    for c in range(0, D_MODEL, COL_CHUNK):
        cs = slice(c, c + COL_CHUNK)
        o_ref[:, cs] = o_ref[:, cs] + _dot(y_ref[:, POOL_W:D_MODEL], wout_ref[POOL_W:D_MODEL, cs])


def _layer0(x2d, seq, norm_g, w_in, w_pool, pool_scale, mem_k, mem_v, w_out):
    n = x2d.shape[0]
    T = TOK_TILE
    tiles_per_seq = seq // T
    n_in = w_in.shape[1]
    return pl.pallas_call(
        functools.partial(_layer0_kernel, tiles_per_seq),
        grid=(n // T,),
        in_specs=[
            pl.BlockSpec((T, D_MODEL), lambda i: (i, 0)),
            _resident((1, D_MODEL)),
            _resident((D_MODEL, n_in)),
            _resident(w_pool.shape),
            _resident((1, POOL_W)),
            pl.BlockSpec((None, MEM_LEN, MEM_W), lambda i: (i // tiles_per_seq, 0, 0)),
            pl.BlockSpec((None, MEM_LEN, MEM_W), lambda i: (i // tiles_per_seq, 0, 0)),
            _resident((D_MODEL, D_MODEL)),
        ],
        out_specs=pl.BlockSpec((T, D_MODEL), lambda i: (i, 0)),
        out_shape=jax.ShapeDtypeStruct((n, D_MODEL), F32),
        scratch_shapes=[
            pltpu.VMEM((T, D_MODEL), BF16),
            pltpu.VMEM((T, n_in - POOL_W), F32),
            pltpu.VMEM((T + POOL_HALO, POOL_W), F32),
            pltpu.VMEM((T, D_MODEL), BF16),
        ],
        compiler_params=pltpu.CompilerParams(dimension_semantics=("arbitrary",), vmem_limit_bytes=VMEM_LIMIT),
        name="layer0",
    )(x2d, norm_g.reshape(1, D_MODEL), w_in, w_pool, pool_scale.reshape(1, POOL_W), mem_k, mem_v, w_out)


def _proj1_kernel(h_ref, gkv_ref, g1_ref, wkv_ref, wq_ref, wr_ref, kv_ref, p1_ref, hn_ref, rest_ref):
    T = h_ref.shape[0]
    h = h_ref[...]
    ms = jnp.mean(h * h, axis=-1, keepdims=True)
    hs = h * lax.rsqrt(ms + NORM_EPS)
    hn_ref[...] = (hs * gkv_ref[...]).astype(BF16)
    n_kv = wkv_ref.shape[1]
    for c in range(0, n_kv, COL_CHUNK):
        kv_ref[:, c:c + COL_CHUNK] = _dot(hn_ref[...], wkv_ref[:, c:c + COL_CHUNK])
    hn_ref[...] = (hs * g1_ref[...]).astype(BF16)
    n_q = wq_ref.shape[1]
    for c in range(0, n_q, COL_CHUNK):
        p1_ref[:, c:c + COL_CHUNK] = _dot(hn_ref[...], wq_ref[:, c:c + COL_CHUNK])
    n_r = wr_ref.shape[1]
    for c in range(0, n_r, COL_CHUNK):
        w = min(COL_CHUNK, n_r - c)
        rest_ref[:, c:c + w] = _dot(hn_ref[...], wr_ref[:, c:c + w])
    shift = LANES - N_GATE
    lane = lax.broadcasted_iota(jnp.int32, (T, LANES), 1)
    n_tiles = n_r // LANES - 1
    first = rest_ref[:, 0:LANES]
    p1_ref[:, n_q + n_tiles * LANES:n_q + (n_tiles + 1) * LANES] = jnp.where(lane < N_GATE, first, 0.0)
    prev = pltpu.roll(first, shift, axis=1)
    for t in range(n_tiles):
        nxt = pltpu.roll(rest_ref[:, (t + 1) * LANES:(t + 2) * LANES], shift, axis=1)
        p1_ref[:, n_q + t * LANES:n_q + (t + 1) * LANES] = jnp.where(lane < shift, prev, nxt)
        prev = nxt


def _split_w1_kernel(w_ref, wq_ref, wr_ref):
    n_in, n_q, n_r = w_ref.shape[1], wq_ref.shape[1], wr_ref.shape[1]
    wq_ref[...] = w_ref[:, 0:n_q].astype(BF16)
    full = (n_in - n_q) // LANES * LANES
    wr_ref[:, 0:full] = w_ref[:, n_q:n_q + full].astype(BF16)
    wr_ref[:, full:n_r] = jnp.zeros((w_ref.shape[0], n_r - full), BF16)
    wr_ref[:, full:n_in - n_q] = w_ref[:, n_q + full:n_in].astype(BF16)


def _split_w1(w_in):
    n_in = w_in.shape[2]
    n_r = -(-(n_in - NSA_W) // LANES) * LANES
    rows = TOK_TILE
    return pl.pallas_call(
        _split_w1_kernel,
        grid=(D_MODEL // rows,),
        in_specs=[pl.BlockSpec((None, rows, n_in), lambda i: (0, i, 0))],
        out_specs=[pl.BlockSpec((rows, NSA_W), lambda i: (i, 0)), pl.BlockSpec((rows, n_r), lambda i: (i, 0))],
        out_shape=[jax.ShapeDtypeStruct((D_MODEL, NSA_W), BF16), jax.ShapeDtypeStruct((D_MODEL, n_r), BF16)],
        compiler_params=pltpu.CompilerParams(dimension_semantics=("arbitrary",), vmem_limit_bytes=VMEM_LIMIT),
        name="split_w1",
    )(w_in)


def _proj1(h2d, kv_norm_g, norm_g1, w_kv, w_q, w_rest):
    n = h2d.shape[0]
    T = TOK_TILE
    n_kv, n_q, n_r = w_kv.shape[1], w_q.shape[1], w_rest.shape[1]
    assert n_r % LANES == 0 and n_q % COL_CHUNK == 0
    n_1 = n_q + n_r
    return pl.pallas_call(
        _proj1_kernel,
        grid=(n // T,),
        in_specs=[
            pl.BlockSpec((T, D_MODEL), lambda i: (i, 0)),
            _resident((1, D_MODEL)),
            _resident((1, D_MODEL)),
            _resident((D_MODEL, n_kv)),
            _resident((D_MODEL, n_q)),
            _resident((D_MODEL, n_r)),
        ],
        out_specs=[
            pl.BlockSpec((T, n_kv), lambda i: (i, 0)),
            pl.BlockSpec((T, n_1), lambda i: (i, 0)),
        ],
        out_shape=[
            jax.ShapeDtypeStruct((n, n_kv), F32),
            jax.ShapeDtypeStruct((n, n_1), F32),
        ],
        scratch_shapes=[pltpu.VMEM((T, D_MODEL), BF16), pltpu.VMEM((T, n_r), F32)],
        compiler_params=pltpu.CompilerParams(dimension_semantics=("arbitrary",), vmem_limit_bytes=VMEM_LIMIT),
        name="proj1",
    )(h2d, kv_norm_g.reshape(1, D_MODEL), norm_g1.reshape(1, D_MODEL), w_kv, w_q, w_rest)


def _rope(x, cos_t, sin_t):
    lane = lax.broadcasted_iota(jnp.int32, x.shape, 1)
    swapped = jnp.where(lane < ROT_HALF, pltpu.roll(x, LANES - ROT_HALF, axis=1), pltpu.roll(x, ROT_HALF, axis=1))
    return x * cos_t + swapped * sin_t


KV_ROW_TILE = 512


def _kv_rot_kernel(seq, ksv_ref, kwv_ref, pos_ref, inv_ref, sgn_ref, ksx_ref, vso_ref, kwo_ref, vwo_ref, cos_ref,
                   sin_ref):
    rows = ksv_ref.shape[0]
    ang = pos_ref[...].astype(F32) * inv_ref[...]
    c = jnp.cos(ang)
    s = jnp.sin(ang) * sgn_ref[...]
    cos_ref[...] = c
    sin_ref[...] = s
    r0 = (pl.program_id(0) * rows) % seq
    blk = (r0 + lax.broadcasted_iota(jnp.int32, (rows, LANES), 0)) // SEL_LEN
    lane = lax.broadcasted_iota(jnp.int32, (rows, LANES), 1)
    ind = jnp.where(blk == lane, 1.0, 0.0).astype(BF16)
    for h in range(NSA_KV_HEADS):
        hs = slice(h * HEAD_DIM, (h + 1) * HEAD_DIM)
        vsl = slice(NSA_KV_W + h * HEAD_DIM, NSA_KV_W + (h + 1) * HEAD_DIM)
        ksx_ref[:, 2 * h * HEAD_DIM:(2 * h + 1) * HEAD_DIM] = _rope(ksv_ref[:, hs], c, s).astype(BF16)
        ksx_ref[:, (2 * h + 1) * HEAD_DIM:(2 * h + 2) * HEAD_DIM] = ind
        kwo_ref[:, hs] = _rope(kwv_ref[:, hs], c, s).astype(BF16)
        vs_t = ksv_ref[:, vsl].T.astype(BF16)
        vw_t = kwv_ref[:, vsl].T.astype(BF16)
        for j in range(rows // Q_TILE):
            vso_ref[h, j] = vs_t[:, j * Q_TILE:(j + 1) * Q_TILE]
            vwo_ref[h, j] = vw_t[:, j * Q_TILE:(j + 1) * Q_TILE]


def _kv_rot(kvp, seq, pos_col, inv_full, sgn_full):
    n = kvp.shape[0]
    T = KV_ROW_TILE
    pair_w = 2 * NSA_KV_W
    row = lambda w: pl.BlockSpec((T, w), lambda i: (i, 0))
    vt_shape = (NSA_KV_HEADS, n // Q_TILE, HEAD_DIM, Q_TILE)
    vt_spec = pl.BlockSpec((NSA_KV_HEADS, T // Q_TILE, HEAD_DIM, Q_TILE), lambda i: (0, i, 0, 0))
    return pl.pallas_call(
        functools.partial(_kv_rot_kernel, seq),
        grid=(n // T,),
        in_specs=[
            pl.BlockSpec((T, pair_w), lambda i: (i, 1)),
            pl.BlockSpec((T, pair_w), lambda i: (i, 2)),
            row(1),
            pl.BlockSpec((1, LANES), lambda i: (0, 0)),
            pl.BlockSpec((1, LANES), lambda i: (0, 0)),
        ],
        out_specs=[row(2 * NSA_KV_W), vt_spec, row(NSA_KV_W), vt_spec, row(LANES), row(LANES)],
        out_shape=[
            jax.ShapeDtypeStruct((n, 2 * NSA_KV_W), BF16),
            jax.ShapeDtypeStruct(vt_shape, BF16),
            jax.ShapeDtypeStruct((n, NSA_KV_W), BF16),
            jax.ShapeDtypeStruct(vt_shape, BF16),
            jax.ShapeDtypeStruct((n, LANES), F32),
            jax.ShapeDtypeStruct((n, LANES), F32),
        ],
        compiler_params=pltpu.CompilerParams(dimension_semantics=("arbitrary",), vmem_limit_bytes=VMEM_LIMIT),
        name="kv_rot",
    )(kvp, kvp, pos_col, inv_full, sgn_full)


def _compress_kernel(kc_ref, vc_ref, cos_ref, sin_ref, pe_ref, w1_ref, w2_ref, kcmp_ref, vcmp_ref):
    S = kc_ref.shape[0]
    n_half = S // CMP_STRIDE
    assert CMP_LEN == 2 * CMP_STRIDE
    for part, (src_ref, dst_ref) in enumerate(((kc_ref, kcmp_ref), (vc_ref, vcmp_ref))):
        acc_first = jnp.zeros((n_half, CMP_HID), F32)
        acc_second = jnp.zeros((n_half, CMP_HID), F32)
        for l in range(CMP_STRIDE):
            z = src_ref[pl.ds(l, n_half, stride=CMP_STRIDE), :]
            z1 = (z + pe_ref[part, l:l + 1, :]).astype(BF16)
            acc_first = acc_first + _dot(z1, w1_ref[part, l * HEAD_DIM:(l + 1) * HEAD_DIM, :])
            l2 = l + CMP_STRIDE
            z2 = (z + pe_ref[part, l2:l2 + 1, :]).astype(BF16)
            acc_second = acc_second + _dot(z2, w1_ref[part, l2 * HEAD_DIM:(l2 + 1) * HEAD_DIM, :])
        pre = acc_first + pltpu.roll(acc_second, n_half - 1, axis=0)
        out = _dot(_silu(pre).astype(BF16), w2_ref[part])
        if part == 0:
            c_end = pltpu.roll(cos_ref[pl.ds(CMP_STRIDE - 1, n_half, stride=CMP_STRIDE), :], n_half - 1, axis=0)
            s_end = pltpu.roll(sin_ref[pl.ds(CMP_STRIDE - 1, n_half, stride=CMP_STRIDE), :], n_half - 1, axis=0)
            out = _rope(out, c_end, s_end)
            dst_ref[...] = out.astype(BF16)
        else:
            dst_ref[...] = out.T.astype(BF16)


def _compress(kvp, batch, seq, cos_tab, sin_tab, cmp_pe, cmp_w1, cmp_w2):
    n_half = seq // CMP_STRIDE
    cmp_out = pl.BlockSpec((None, None, n_half, HEAD_DIM), lambda b, g: (b, g, 0, 0))
    tab = pl.BlockSpec((seq, LANES), lambda b, g: (b, 0))
    return pl.pallas_call(
        _compress_kernel,
        grid=(batch, NSA_KV_HEADS),
        in_specs=[
            pl.BlockSpec((seq, HEAD_DIM), lambda b, g: (b, g)),
            pl.BlockSpec((seq, HEAD_DIM), lambda b, g: (b, NSA_KV_HEADS + g)),
            tab, tab,
            _resident(cmp_pe.shape), _resident(cmp_w1.shape), _resident(cmp_w2.shape),
        ],
        out_specs=[cmp_out, pl.BlockSpec((None, None, HEAD_DIM, n_half), lambda b, g: (b, g, 0, 0))],
        out_shape=[
            jax.ShapeDtypeStruct((batch, NSA_KV_HEADS, n_half, HEAD_DIM), BF16),
            jax.ShapeDtypeStruct((batch, NSA_KV_HEADS, HEAD_DIM, n_half), BF16),
        ],
        compiler_params=pltpu.CompilerParams(dimension_semantics=("arbitrary", "arbitrary"),
                                             vmem_limit_bytes=VMEM_LIMIT),
        name="compress",
    )(kvp, kvp, cos_tab, sin_tab, cmp_pe, cmp_w1, cmp_w2)


def _nsa_kernel(q_ref, zq_ref, gl_ref, cos_ref, sin_ref, ksx_ref, vst_ref, kw_ref, vwt_ref, kcmp_ref, vcmpt_ref,
                o_ref, qxt_ref, sc_ref, m_ref, l_ref, acc_ref, ocmp_ref, osel_ref, s_ref, p_ref, alpha_ref,
                rank_ref, owin_ref):
    TQ = q_ref.shape[0]
    R = NSA_GROUP
    M = R * TQ
    n_cmp_pad = kcmp_ref.shape[0]
    n_sel = ksx_ref.shape[0] // SEL_LEN
    g = pl.program_id(1)
    qi = pl.program_id(2)
    t0 = qi * TQ

    cos_t = cos_ref[...]
    sin_t = sin_ref[...]
    for r in range(R):
        qr = _rope(q_ref[:, r * HEAD_DIM:(r + 1) * HEAD_DIM], cos_t, sin_t) * (QK_SCALE * LOG2E)
        qxt_ref[0:HEAD_DIM, r * TQ:(r + 1) * TQ] = qr.T.astype(BF16)
    qxt_ref[HEAD_DIM + n_sel:2 * HEAD_DIM, :] = jnp.zeros((HEAD_DIM - n_sel, M), BF16)

    lane_q = lax.broadcasted_iota(jnp.int32, (1, M), 1) % TQ
    q_t = qxt_ref[0:HEAD_DIM, :]
    slabs = [slice(r * TQ, (r + 1) * TQ) for r in range(R)]
    key_sub = lax.broadcasted_iota(jnp.int32, (TQ, TQ), 0)
    q_lane = lax.broadcasted_iota(jnp.int32, (TQ, TQ), 1)
    causal = key_sub <= q_lane

    assert WINDOW == 2 * TQ
    w_tiles = (jnp.maximum(qi - 2, 0), jnp.maximum(qi - 1, 0), qi)
    w_masks = ((key_sub > q_lane) & (qi >= 2), (key_sub >= 0) & (qi >= 1), causal)
    w_keys = [kw_ref[pl.ds(pl.multiple_of(kt * TQ, TQ), TQ), :] for kt in w_tiles]
    w_scores = [[_dot(k, qxt_ref[0:HEAD_DIM, rs]) for k in w_keys] for rs in slabs]

    s_t = _dot(kcmp_ref[...], q_t)
    cmp_end = lax.broadcasted_iota(jnp.int32, (n_cmp_pad, 1), 0) * CMP_STRIDE + (CMP_LEN - 1)
    valid = cmp_end <= t0 + lane_q
    s_t = jnp.where(valid, s_t, NEG_INF)
    e = jnp.where(valid, jnp.exp2(s_t - jnp.max(s_t, axis=0, keepdims=True)), 0.0)
    p_cmp = e * (1.0 / jnp.maximum(jnp.sum(e, axis=0, keepdims=True), 1e-30))
    ocmp_ref[...] = _dot(vcmpt_ref[...], p_cmp.astype(BF16))

    w_probs = []
    w_inv_l = []
    for s_head in w_scores:
        s_head = [jnp.where(mk, s, NEG_INF) for s, mk in zip(s_head, w_masks)]
        m_col = jnp.max(s_head[0], axis=0, keepdims=True)
        for s in s_head[1:]:
            m_col = jnp.maximum(m_col, jnp.max(s, axis=0, keepdims=True))
        p_head = [jnp.exp2(s - m_col) for s in s_head]
        l_col = jnp.sum(p_head[0], axis=0, keepdims=True)
        for p in p_head[1:]:
            l_col = l_col + jnp.sum(p, axis=0, keepdims=True)
        w_inv_l.append(1.0 / l_col)
        w_probs.append([p.astype(BF16) for p in p_head])

    p_sum = p_cmp[:, 0:TQ]
    for r in range(1, R):
        p_sum = p_sum + p_cmp[:, r * TQ:(r + 1) * TQ]
    jn = lax.broadcasted_iota(jnp.int32, (n_sel, n_cmp_pad), 0) * SEL_LEN
    cn = lax.broadcasted_iota(jnp.int32, (n_sel, n_cmp_pad), 1) * CMP_STRIDE
    ov = jnp.minimum(cn + (CMP_LEN - 1), jn + (SEL_LEN - 1)) - jnp.maximum(cn, jn) + 1
    ov_t = (jnp.maximum(ov, 0).astype(F32) / CMP_LEN).astype(BF16)
    p_hi = p_sum.astype(BF16)
    p_lo = (p_sum - p_hi.astype(F32)).astype(BF16)
    imp = _dot(ov_t, p_hi) + _dot(ov_t, p_lo)

    for rs, p_head, inv_l in zip(slabs, w_probs, w_inv_l):
        o = _dot(vwt_ref[w_tiles[0]], p_head[0])
        for kt, p in zip(w_tiles[1:], p_head[1:]):
            o = o + _dot(vwt_ref[kt], p)
        owin_ref[:, rs] = o * inv_l

    jb = lax.broadcasted_iota(jnp.int32, (n_sel, TQ), 0)
    tq = t0 + lax.broadcasted_iota(jnp.int32, (n_sel, TQ), 1)
    cur = tq // SEL_LEN
    forced = (jb == 0) | (jb == cur) | (jb == cur - 1)
    ok = jb * SEL_LEN <= tq
    score = jnp.where(ok, jnp.where(forced, FORCE_SCORE, imp), NEG_INF)
    sc_ref[...] = score

    rank_ref[...] = jnp.zeros((n_sel, TQ), F32)
    blocks_per_tile = TQ // SEL_LEN
    for kt in range(n_sel // blocks_per_tile):
        @pl.when(kt <= qi)
        def _(kt=kt):
            others = [sc_ref[j2:j2 + 1, :] for j2 in range(kt * blocks_per_tile, (kt + 1) * blocks_per_tile)]
            for gi in range(n_sel // SUBLANES):
                rows = slice(gi * SUBLANES, (gi + 1) * SUBLANES)
                sg = sc_ref[rows, :]
                cnt = rank_ref[rows, :]
                for u, other in enumerate(others):
                    j2 = kt * blocks_per_tile + u
                    if gi * SUBLANES > j2:
                        before = other >= sg
                    elif (gi + 1) * SUBLANES - 1 < j2:
                        before = other > sg
                    else:
                        row_j = gi * SUBLANES + lax.broadcasted_iota(jnp.int32, (SUBLANES, TQ), 0)
                        before = (other > sg) | ((other == sg) & (row_j > j2))
                    cnt = cnt + jnp.where(before, 1.0, 0.0)
                rank_ref[rows, :] = cnt
    n_top = min(SEL_TOPK, n_sel)
    bias_t = jnp.where(rank_ref[...] < n_top, 0.0, SEL_BIAS).astype(BF16)
    for r in range(R):
        qxt_ref[HEAD_DIM:HEAD_DIM + n_sel, r * TQ:(r + 1) * TQ] = bias_t

    diag = pl.ds(pl.multiple_of(t0, TQ), TQ)

    def sel_scores(k_tile):
        return [_dot(k_tile, qxt_ref[:, rs]) for rs in slabs]

    def sel_pv(v_t):
        return [_dot(v_t, p_ref[:, rs]) for rs in slabs]

    n_full = qi
    s_diag = sel_scores(ksx_ref[diag, :])
    s_first = sel_scores(ksx_ref[0:TQ, :])
    for rs, s_t in zip(slabs, s_diag):
        s_t = jnp.where(causal, s_t, NEG_INF)
        m_tile = jnp.max(s_t, axis=0, keepdims=True)
        p = jnp.exp2(s_t - m_tile)
        m_ref[:, rs] = m_tile
        l_ref[:, rs] = jnp.sum(p, axis=0, keepdims=True)
        p_ref[:, rs] = p.astype(BF16)
    alpha_ref[...] = jnp.ones((1, M), F32)
    acc_ref[...] = jnp.zeros((HEAD_DIM, M), F32)
    for rs, s_t in zip(slabs, s_first):
        s_ref[:, rs] = s_t

    def sel_step(kt):
        alpha_prev = alpha_ref[...]
        nxt = jnp.minimum(kt + 1, n_full - 1)
        s_next = sel_scores(ksx_ref[pl.ds(pl.multiple_of(nxt * TQ, TQ), TQ), :])
        prev = jnp.where(kt == 0, qi, kt - 1)
        pv_prev = sel_pv(vst_ref[prev])
        for rs in slabs:
            s_t = s_ref[:, rs]
            m_prev = m_ref[:, rs]
            m_new = jnp.maximum(m_prev, jnp.max(s_t, axis=0, keepdims=True))
            alpha = jnp.exp2(m_prev - m_new)
            p = jnp.exp2(s_t - m_new)
            m_ref[:, rs] = m_new
            l_ref[:, rs] = alpha * l_ref[:, rs] + jnp.sum(p, axis=0, keepdims=True)
            alpha_ref[:, rs] = alpha
            p_ref[:, rs] = p.astype(BF16)
        for rs, o in zip(slabs, pv_prev):
            acc_ref[:, rs] = alpha_prev[:, rs] * acc_ref[:, rs] + o
        for rs, s_t in zip(slabs, s_next):
            s_ref[:, rs] = s_t

    def sel_pair(j, carry):
        sel_step(2 * j)
        sel_step(2 * j + 1)
        return carry
    lax.fori_loop(0, lax.shift_right_logical(n_full, 1), sel_pair, 0)

    @pl.when(n_full % 2 == 1)
    def _():
        sel_step(n_full - 1)

    last = jnp.where(n_full == 0, qi, n_full - 1)
    inv_l = 1.0 / l_ref[...]
    for rs, o in zip(slabs, sel_pv(vst_ref[last])):
        osel_ref[:, rs] = (alpha_ref[:, rs] * acc_ref[:, rs] + o) * inv_l[:, rs]

    o_win = owin_ref[...]

    gates = 1.0 / (1.0 + jnp.exp(-gl_ref[...]))
    gates_t = pltpu.roll(gates, (LANES - 3 * R * g) % LANES, axis=1).T
    for r in range(R):
        rs = slice(r * TQ, (r + 1) * TQ)
        mix_t = (gates_t[3 * r:3 * r + 1, :] * ocmp_ref[:, rs] + gates_t[3 * r + 1:3 * r + 2, :] * osel_ref[:, rs]
                 + gates_t[3 * r + 2:3 * r + 3, :] * o_win[:, rs])
        cs = slice(r * HEAD_DIM, (r + 1) * HEAD_DIM)
        o_ref[:, cs] = (mix_t.T * _silu(zq_ref[:, cs])).astype(o_ref.dtype)


def _nsa(p1, batch, seq, cos_tab, sin_tab, ksx, vs_t, kw, vw_t, kcmp, vcmp_t):
    n = p1.shape[0]
    TQ = Q_TILE
    R = NSA_GROUP
    q_tiles = seq // TQ
    gw = R * HEAD_DIM
    n_cmp_pad = kcmp.shape[2]
    n_sel = seq // SEL_LEN
    tok = lambda b, g, i: b * q_tiles + i
    head_kv = pl.BlockSpec((seq, HEAD_DIM), lambda b, g, i: (b, g))
    head_vt = pl.BlockSpec((None, q_tiles, HEAD_DIM, TQ), lambda b, g, i: (g, b, 0, 0))
    cmp_kv = pl.BlockSpec((None, None, n_cmp_pad, HEAD_DIM), lambda b, g, i: (b, g, 0, 0))
    cmp_vt = pl.BlockSpec((None, None, HEAD_DIM, n_cmp_pad), lambda b, g, i: (b, g, 0, 0))
    return pl.pallas_call(
        _nsa_kernel,
        grid=(batch, NSA_KV_HEADS, q_tiles),
        in_specs=[
            pl.BlockSpec((TQ, gw), lambda b, g, i: (tok(b, g, i), g)),
            pl.BlockSpec((TQ, gw), lambda b, g, i: (tok(b, g, i), NSA_W // gw + g)),
            pl.BlockSpec((TQ, GATE_PAD), lambda b, g, i: (tok(b, g, i), (2 * NSA_W + 2 * MEM_W) // GATE_PAD)),
            pl.BlockSpec((TQ, LANES), lambda b, g, i: (tok(b, g, i), 0)),
            pl.BlockSpec((TQ, LANES), lambda b, g, i: (tok(b, g, i), 0)),
            pl.BlockSpec((seq, 2 * HEAD_DIM), lambda b, g, i: (b, g)),
            head_vt, head_kv, head_vt, cmp_kv, cmp_vt,
        ],
        out_specs=pl.BlockSpec((TQ, gw), lambda b, g, i: (tok(b, g, i), g)),
        out_shape=jax.ShapeDtypeStruct((n, NSA_W), BF16),
        scratch_shapes=[
            pltpu.VMEM((2 * HEAD_DIM, R * TQ), BF16),
            pltpu.VMEM((n_sel, TQ), F32),
            pltpu.VMEM((1, R * TQ), F32),
            pltpu.VMEM((1, R * TQ), F32),
            pltpu.VMEM((HEAD_DIM, R * TQ), F32),
            pltpu.VMEM((HEAD_DIM, R * TQ), F32),
            pltpu.VMEM((HEAD_DIM, R * TQ), F32),
            pltpu.VMEM((TQ, R * TQ), F32),
            pltpu.VMEM((TQ, R * TQ), BF16),
            pltpu.VMEM((1, R * TQ), F32),
            pltpu.VMEM((n_sel, TQ), F32),
            pltpu.VMEM((HEAD_DIM, R * TQ), F32),
        ],
        compiler_params=pltpu.CompilerParams(dimension_semantics=("arbitrary", "arbitrary", "arbitrary"),
                                             vmem_limit_bytes=VMEM_LIMIT),
        name="nsa",
    )(p1, p1, p1, cos_tab, sin_tab, ksx, vs_t, kw, vw_t, kcmp, vcmp_t)


def _tail_kernel(y_ref, qz_ref, mk_ref, mv_ref, h_ref, wout_ref, g_ref, o_ref, ymem_ref, acc_ref):
    mem_scores = _memory_scores(qz_ref[:, 0:MEM_W], mk_ref)
    for c in range(0, D_MODEL, COL_CHUNK):
        cs = slice(c, c + COL_CHUNK)
        acc_ref[:, cs] = h_ref[:, cs] + _dot(y_ref[...], wout_ref[0:NSA_W, cs])
    _memory_finish(mem_scores, qz_ref[:, MEM_W:2 * MEM_W], mv_ref, ymem_ref, 0)
    for c in range(0, D_MODEL, COL_CHUNK):
        cs = slice(c, c + COL_CHUNK)
        acc_ref[:, cs] = acc_ref[:, cs] + _dot(ymem_ref[...], wout_ref[NSA_W:D_MODEL, cs])
    o_ref[...] = _rms_scale(acc_ref[...], g_ref[...])


def _tail(y_nsa, p1, seq, mem_k, mem_v, h2d, w_out, final_g):
    n = h2d.shape[0]
    T = TOK_TILE
    tiles_per_seq = seq // T
    return pl.pallas_call(
        _tail_kernel,
        grid=(n // T,),
        in_specs=[
            pl.BlockSpec((T, NSA_W), lambda i: (i, 0)),
            pl.BlockSpec((T, 2 * MEM_W), lambda i: (i, 2 * NSA_W // (2 * MEM_W))),
            pl.BlockSpec((None, MEM_LEN, MEM_W), lambda i: (i // tiles_per_seq, 0, 0)),
            pl.BlockSpec((None, MEM_LEN, MEM_W), lambda i: (i // tiles_per_seq, 0, 0)),
            pl.BlockSpec((T, D_MODEL), lambda i: (i, 0)),
            _resident((D_MODEL, D_MODEL)),
            _resident((1, D_MODEL)),
        ],
        out_specs=pl.BlockSpec((T, D_MODEL), lambda i: (i, 0)),
        out_shape=jax.ShapeDtypeStruct((n, D_MODEL), F32),
        scratch_shapes=[pltpu.VMEM((T, MEM_W), BF16), pltpu.VMEM((T, D_MODEL), F32)],
        compiler_params=pltpu.CompilerParams(dimension_semantics=("arbitrary",), vmem_limit_bytes=VMEM_LIMIT),
        name="tail",
    )(y_nsa, p1, mem_k, mem_v, h2d, w_out, final_g.reshape(1, D_MODEL))


def kernel(x, mem, positions, norm_g, mem_norm_g, w_mem_kv, w_out, a_w_in, a_w_pool, a_pool_scale, b_w_in,
           kv_norm_g, w_kv, cmp_pe, cmp_w1, cmp_w2, final_g):
    batch, seq, _ = x.shape
    assert seq % Q_TILE == 0 and seq % TOK_TILE == 0 and Q_TILE % SEL_LEN == 0
    n = batch * seq
    x2d = x.reshape(n, D_MODEL)

    mkv = _mem_kv(mem.reshape(batch * MEM_LEN, D_MODEL), mem_norm_g, w_mem_kv.astype(BF16))
    mkv = mkv.reshape(mkv.shape[0], batch, MEM_LEN, 2 * MEM_W)
    mem_k, mem_v = mkv[..., :MEM_W], mkv[..., MEM_W:]

    h1 = _layer0(x2d, seq, norm_g[0], a_w_in[0].astype(BF16), a_w_pool[0].astype(BF16), a_pool_scale[0],
                 mem_k[0], mem_v[0], w_out[0].astype(BF16))

    w_q, w_rest = _split_w1(b_w_in)
    kvp, p1 = _proj1(h1, kv_norm_g, norm_g[1], w_kv.astype(BF16), w_q, w_rest)

    half = jnp.arange(ROT_HALF, dtype=F32)
    inv = ROPE_THETA ** (-half * 2.0 / ROT_DIM)
    pad = jnp.zeros((LANES - ROT_DIM,), F32)
    inv_full = jnp.concatenate([inv, inv, pad]).reshape(1, LANES)
    sgn_full = jnp.concatenate([-jnp.ones((ROT_HALF,), F32), jnp.ones((ROT_HALF,), F32), pad]).reshape(1, LANES)
    ksx, vs, kw, vw, cos_tab, sin_tab = _kv_rot(kvp, seq, positions.reshape(n, 1), inv_full, sgn_full)
    kcmp, vcmp = _compress(kvp, batch, seq, cos_tab, sin_tab, cmp_pe, cmp_w1.astype(BF16), cmp_w2.astype(BF16))

    y_nsa = _nsa(p1, batch, seq, cos_tab, sin_tab, ksx, vs, kw, vw, kcmp, vcmp)
    out = _tail(y_nsa, p1, seq, mem_k[1], mem_v[1], h1, w_out[1].astype(BF16), final_g)
    return out.reshape(batch, seq, D_MODEL)
```

```python
import functools

import jax
import jax.numpy as jnp
from jax import lax
from jax.experimental import pallas as pl
from jax.experimental.pallas import tpu as pltpu

D_MODEL = 2048
MEM_LEN = 256
HEAD_DIM = 128
MEM_HEADS = 4
MEM_W = MEM_HEADS * HEAD_DIM
POOL_W = D_MODEL - MEM_W
POOL_WINDOWS = (2, 4, 8, 16)
POOL_GC = POOL_W // len(POOL_WINDOWS)
POOL_HALO = 16
NSA_W = D_MODEL - MEM_W
NSA_HEADS = NSA_W // HEAD_DIM
NSA_KV_HEADS = 4
NSA_GROUP = NSA_HEADS // NSA_KV_HEADS
NSA_KV_W = NSA_KV_HEADS * HEAD_DIM
CMP_LEN = 32
CMP_STRIDE = 16
CMP_HID = 256
SEL_LEN = 64
SEL_TOPK = 16
WINDOW = 512
ROT_DIM = HEAD_DIM // 4
ROT_HALF = ROT_DIM // 2
ROPE_THETA = 500000.0
NORM_EPS = 1e-6
FORCE_SCORE = 1e4
NEG_INF = -1e30
SEL_BIAS = -1e9
QK_SCALE = HEAD_DIM ** -0.5
LOG2E = 1.4426950408889634

LANES = 128
SUBLANES = 8
TOK_TILE = 256
Q_TILE = 256
COL_CHUNK = 512
GATE_PAD = LANES
N_GATE = 3 * NSA_HEADS
VMEM_LIMIT = 56 * 1024 * 1024

F32 = jnp.float32
BF16 = jnp.bfloat16


def _resident(shape):
    nd = len(shape)
    return pl.BlockSpec(shape, lambda *_: (0,) * nd, pipeline_mode=pl.Buffered(1))


def _rms_scale(x, g):
    ms = jnp.mean(x * x, axis=-1, keepdims=True)
    return x * lax.rsqrt(ms + NORM_EPS) * g


def _silu(z):
    return z * (1.0 / (1.0 + jnp.exp(-z)))


def _dot(a, b):
    return jnp.dot(a, b, preferred_element_type=F32)


def _dot_nt(a, b):
    return lax.dot_general(a, b, (((1,), (1,)), ((), ())), preferred_element_type=F32)


def _memory_scores(qm, k_ref):
    scores = []
    for h in range(MEM_HEADS):
        cs = slice(h * HEAD_DIM, (h + 1) * HEAD_DIM)
        scores.append(_dot_nt((qm[:, cs] * QK_SCALE).astype(BF16), k_ref[:, cs]))
    return scores


def _memory_finish(scores, zm, v_ref, y_ref, col0):
    for h, s in enumerate(scores):
        cs = slice(h * HEAD_DIM, (h + 1) * HEAD_DIM)
        e = jnp.exp(s - jnp.max(s, axis=-1, keepdims=True))
        p = e / jnp.sum(e, axis=-1, keepdims=True)
        o = _dot(p.astype(BF16), v_ref[:, cs])
        y_ref[:, col0 + h * HEAD_DIM:col0 + (h + 1) * HEAD_DIM] = (o * _silu(zm[:, cs])).astype(y_ref.dtype)


def _mem_kv_kernel(mem_ref, g_ref, w_ref, o_ref):
    hn = _rms_scale(mem_ref[...], g_ref[...]).astype(BF16)
    o_ref[...] = _dot(hn, w_ref[...]).astype(o_ref.dtype)


def _mem_kv(mem2d, mem_norm_g, w_mem_kv_bf16):
    depth = w_mem_kv_bf16.shape[0]
    rows = mem2d.shape[0]
    T = TOK_TILE
    return pl.pallas_call(
        _mem_kv_kernel,
        grid=(depth, rows // T),
        in_specs=[
            pl.BlockSpec((T, D_MODEL), lambda l, i: (i, 0)),
            pl.BlockSpec((None, 1, D_MODEL), lambda l, i: (l, 0, 0)),
            pl.BlockSpec((None, D_MODEL, 2 * MEM_W), lambda l, i: (l, 0, 0)),
        ],
        out_specs=pl.BlockSpec((None, T, 2 * MEM_W), lambda l, i: (l, i, 0)),
        out_shape=jax.ShapeDtypeStruct((depth, rows, 2 * MEM_W), BF16),
        compiler_params=pltpu.CompilerParams(dimension_semantics=("arbitrary", "arbitrary"),
                                             vmem_limit_bytes=VMEM_LIMIT),
        name="mem_kv",
    )(mem2d, mem_norm_g.reshape(depth, 1, D_MODEL), w_mem_kv_bf16)


def _layer0_kernel(tiles_per_seq, x_ref, g_ref, win_ref, wpool_ref, scale_ref, mk_ref, mv_ref, wout_ref,
                   o_ref, hn_ref, proj_ref, uext_ref, y_ref):
    T = x_ref.shape[0]
    tb = pl.program_id(0) % tiles_per_seq
    x = x_ref[...]
    hn_ref[...] = _rms_scale(x, g_ref[...]).astype(BF16)

    @pl.when(tb == 0)
    def _():
        uext_ref[0:POOL_HALO, :] = jnp.zeros((POOL_HALO, POOL_W), F32)

    @pl.when(tb != 0)
    def _():
        uext_ref[0:POOL_HALO, :] = uext_ref[T:T + POOL_HALO, :]

    n_in = win_ref.shape[1]
    for c in range(0, n_in, COL_CHUNK):
        blk = _dot(hn_ref[...], win_ref[:, c:c + COL_CHUNK])
        if c < POOL_W:
            uext_ref[POOL_HALO:POOL_HALO + T, c:c + COL_CHUNK] = blk
        else:
            proj_ref[:, c - POOL_W:c - POOL_W + COL_CHUNK] = blk

    t1 = (tb * T + lax.broadcasted_iota(jnp.int32, (T, 1), 0) + 1).astype(F32)
    for gi, win in enumerate(POOL_WINDOWS):
        cs = slice(gi * POOL_GC, (gi + 1) * POOL_GC)
        u = uext_ref[POOL_HALO:POOL_HALO + T, cs]
        acc = u
        for k in range(1, win):
            acc = acc + uext_ref[POOL_HALO - k:POOL_HALO - k + T, cs]
        pooled = acc / jnp.minimum(t1, float(win)) - u
        mixed = _dot(pooled.astype(BF16), wpool_ref[gi])
        y_ref[:, cs] = (mixed * scale_ref[:, cs] * _silu(proj_ref[:, cs])).astype(BF16)

    mem_scores = _memory_scores(proj_ref[:, POOL_W:POOL_W + MEM_W], mk_ref)
    for c in range(0, D_MODEL, COL_CHUNK):
        cs = slice(c, c + COL_CHUNK)
        o_ref[:, cs] = x_ref[:, cs] + _dot(y_ref[:, 0:POOL_W], wout_ref[0:POOL_W, cs])

    zm = proj_ref[:, POOL_W + MEM_W:POOL_W + 2 * MEM_W]
    _memory_finish(mem_scores, zm, mv_ref, y_ref, POOL_W)

    for c in range(0, D_MODEL, COL_CHUNK):
        cs = slice(c, c + COL_CHUNK)
        o_ref[:, cs] = o_ref[:, cs] + _dot(y_ref[:, POOL_W:D_MODEL], wout_ref[POOL_W:D_MODEL, cs])


def _layer0(x2d, seq, norm_g, w_in, w_pool, pool_scale, mem_k, mem_v, w_out):
    n = x2d.shape[0]
    T = TOK_TILE
    tiles_per_seq = seq // T
    n_in = w_in.shape[1]
    return pl.pallas_call(
        functools.partial(_layer0_kernel, tiles_per_seq),
        grid=(n // T,),
        in_specs=[
            pl.BlockSpec((T, D_MODEL), lambda i: (i, 0)),
            _resident((1, D_MODEL)),
            _resident((D_MODEL, n_in)),
            _resident(w_pool.shape),
            _resident((1, POOL_W)),
            pl.BlockSpec((None, MEM_LEN, MEM_W), lambda i: (i // tiles_per_seq, 0, 0)),
            pl.BlockSpec((None, MEM_LEN, MEM_W), lambda i: (i // tiles_per_seq, 0, 0)),
            _resident((D_MODEL, D_MODEL)),
        ],
        out_specs=pl.BlockSpec((T, D_MODEL), lambda i: (i, 0)),
        out_shape=jax.ShapeDtypeStruct((n, D_MODEL), F32),
        scratch_shapes=[
            pltpu.VMEM((T, D_MODEL), BF16),
            pltpu.VMEM((T, n_in - POOL_W), F32),
            pltpu.VMEM((T + POOL_HALO, POOL_W), F32),
            pltpu.VMEM((T, D_MODEL), BF16),
        ],
        compiler_params=pltpu.CompilerParams(dimension_semantics=("arbitrary",), vmem_limit_bytes=VMEM_LIMIT),
        name="layer0",
    )(x2d, norm_g.reshape(1, D_MODEL), w_in, w_pool, pool_scale.reshape(1, POOL_W), mem_k, mem_v, w_out)


def _proj1_kernel(h_ref, gkv_ref, g1_ref, wkv_ref, wq_ref, wr_ref, kv_ref, p1_ref, hn_ref, rest_ref):
    T = h_ref.shape[0]
    h = h_ref[...]
    ms = jnp.mean(h * h, axis=-1, keepdims=True)
    hs = h * lax.rsqrt(ms + NORM_EPS)
    hn_ref[...] = (hs * gkv_ref[...]).astype(BF16)
    n_kv = wkv_ref.shape[1]
    for c in range(0, n_kv, COL_CHUNK):
        kv_ref[:, c:c + COL_CHUNK] = _dot(hn_ref[...], wkv_ref[:, c:c + COL_CHUNK])
    hn_ref[...] = (hs * g1_ref[...]).astype(BF16)
    n_q = wq_ref.shape[1]
    for c in range(0, n_q, COL_CHUNK):
        p1_ref[:, c:c + COL_CHUNK] = _dot(hn_ref[...], wq_ref[:, c:c + COL_CHUNK])
    n_r = wr_ref.shape[1]
    for c in range(0, n_r, COL_CHUNK):
        w = min(COL_CHUNK, n_r - c)
        rest_ref[:, c:c + w] = _dot(hn_ref[...], wr_ref[:, c:c + w])
    shift = LANES - N_GATE
    lane = lax.broadcasted_iota(jnp.int32, (T, LANES), 1)
    n_tiles = n_r // LANES - 1
    first = rest_ref[:, 0:LANES]
    p1_ref[:, n_q + n_tiles * LANES:n_q + (n_tiles + 1) * LANES] = jnp.where(lane < N_GATE, first, 0.0)
    prev = pltpu.roll(first, shift, axis=1)
    for t in range(n_tiles):
        nxt = pltpu.roll(rest_ref[:, (t + 1) * LANES:(t + 2) * LANES], shift, axis=1)
        p1_ref[:, n_q + t * LANES:n_q + (t + 1) * LANES] = jnp.where(lane < shift, prev, nxt)
        prev = nxt


def _split_w1_kernel(w_ref, wq_ref, wr_ref):
    n_in, n_q, n_r = w_ref.shape[1], wq_ref.shape[1], wr_ref.shape[1]
    wq_ref[...] = w_ref[:, 0:n_q].astype(BF16)
    full = (n_in - n_q) // LANES * LANES
    wr_ref[:, 0:full] = w_ref[:, n_q:n_q + full].astype(BF16)
    wr_ref[:, full:n_r] = jnp.zeros((w_ref.shape[0], n_r - full), BF16)
    wr_ref[:, full:n_in - n_q] = w_ref[:, n_q + full:n_in].astype(BF16)


def _split_w1(w_in):
    n_in = w_in.shape[2]
    n_r = -(-(n_in - NSA_W) // LANES) * LANES
    rows = TOK_TILE
    return pl.pallas_call(
        _split_w1_kernel,
        grid=(D_MODEL // rows,),
        in_specs=[pl.BlockSpec((None, rows, n_in), lambda i: (0, i, 0))],
        out_specs=[pl.BlockSpec((rows, NSA_W), lambda i: (i, 0)), pl.BlockSpec((rows, n_r), lambda i: (i, 0))],
        out_shape=[jax.ShapeDtypeStruct((D_MODEL, NSA_W), BF16), jax.ShapeDtypeStruct((D_MODEL, n_r), BF16)],
        compiler_params=pltpu.CompilerParams(dimension_semantics=("arbitrary",), vmem_limit_bytes=VMEM_LIMIT),
        name="split_w1",
    )(w_in)


def _proj1(h2d, kv_norm_g, norm_g1, w_kv, w_q, w_rest):
    n = h2d.shape[0]
    T = TOK_TILE
    n_kv, n_q, n_r = w_kv.shape[1], w_q.shape[1], w_rest.shape[1]
    assert n_r % LANES == 0 and n_q % COL_CHUNK == 0
    n_1 = n_q + n_r
    return pl.pallas_call(
        _proj1_kernel,
        grid=(n // T,),
        in_specs=[
            pl.BlockSpec((T, D_MODEL), lambda i: (i, 0)),
            _resident((1, D_MODEL)),
            _resident((1, D_MODEL)),
            _resident((D_MODEL, n_kv)),
            _resident((D_MODEL, n_q)),
            _resident((D_MODEL, n_r)),
        ],
        out_specs=[
            pl.BlockSpec((T, n_kv), lambda i: (i, 0)),
            pl.BlockSpec((T, n_1), lambda i: (i, 0)),
        ],
        out_shape=[
            jax.ShapeDtypeStruct((n, n_kv), F32),
            jax.ShapeDtypeStruct((n, n_1), F32),
        ],
        scratch_shapes=[pltpu.VMEM((T, D_MODEL), BF16), pltpu.VMEM((T, n_r), F32)],
        compiler_params=pltpu.CompilerParams(dimension_semantics=("arbitrary",), vmem_limit_bytes=VMEM_LIMIT),
        name="proj1",
    )(h2d, kv_norm_g.reshape(1, D_MODEL), norm_g1.reshape(1, D_MODEL), w_kv, w_q, w_rest)


def _rope(x, cos_t, sin_t):
    lane = lax.broadcasted_iota(jnp.int32, x.shape, 1)
    swapped = jnp.where(lane < ROT_HALF, pltpu.roll(x, LANES - ROT_HALF, axis=1), pltpu.roll(x, ROT_HALF, axis=1))
    return x * cos_t + swapped * sin_t


KV_ROW_TILE = 512


def _kv_rot_kernel(seq, ksv_ref, kwv_ref, pos_ref, inv_ref, sgn_ref, ksx_ref, vso_ref, kwo_ref, vwo_ref, cos_ref,
                   sin_ref):
    rows = ksv_ref.shape[0]
    groups = LANES // ROT_DIM
    gr = rows // groups
    pos = pos_ref[...].astype(F32)
    lane_g = lax.broadcasted_iota(jnp.int32, (gr, LANES), 1)
    packed = jnp.broadcast_to(pos[0:gr], (gr, LANES))
    inv_t = inv_ref[...]
    sgn_t = sgn_ref[...]
    for k in range(1, groups):
        packed = jnp.where(lane_g >= k * ROT_DIM, pos[k * gr:(k + 1) * gr], packed)
        inv_t = inv_t + pltpu.roll(inv_ref[...], k * ROT_DIM, axis=1)
        sgn_t = sgn_t + pltpu.roll(sgn_ref[...], k * ROT_DIM, axis=1)
    ang = packed * inv_t
    c_packed = jnp.cos(ang)
    s_packed = jnp.sin(ang) * sgn_t
    c_parts, s_parts = [], []
    for k in range(groups):
        back = (LANES - k * ROT_DIM) % LANES
        c_k = c_packed if k == 0 else pltpu.roll(c_packed, back, axis=1)
        s_k = s_packed if k == 0 else pltpu.roll(s_packed, back, axis=1)
        c_parts.append(jnp.where(lane_g < ROT_DIM, c_k, 1.0))
        s_parts.append(jnp.where(lane_g < ROT_DIM, s_k, 0.0))
    c = jnp.concatenate(c_parts, axis=0)
    s = jnp.concatenate(s_parts, axis=0)
    cos_ref[...] = c
    sin_ref[...] = s
    r0 = (pl.program_id(0) * rows) % seq
    blk = (r0 + lax.broadcasted_iota(jnp.int32, (rows, LANES), 0)) // SEL_LEN
    lane = lax.broadcasted_iota(jnp.int32, (rows, LANES), 1)
    ind = jnp.where(blk == lane, 1.0, 0.0).astype(BF16)
    for h in range(NSA_KV_HEADS):
        hs = slice(h * HEAD_DIM, (h + 1) * HEAD_DIM)
        vsl = slice(NSA_KV_W + h * HEAD_DIM, NSA_KV_W + (h + 1) * HEAD_DIM)
        ksx_ref[:, 2 * h * HEAD_DIM:(2 * h + 1) * HEAD_DIM] = _rope(ksv_ref[:, hs], c, s).astype(BF16)
        ksx_ref[:, (2 * h + 1) * HEAD_DIM:(2 * h + 2) * HEAD_DIM] = ind
        kwo_ref[:, hs] = _rope(kwv_ref[:, hs], c, s).astype(BF16)
        vs_t = ksv_ref[:, vsl].T.astype(BF16)
        vw_t = kwv_ref[:, vsl].T.astype(BF16)
        for j in range(rows // Q_TILE):
            vso_ref[h, j] = vs_t[:, j * Q_TILE:(j + 1) * Q_TILE]
            vwo_ref[h, j] = vw_t[:, j * Q_TILE:(j + 1) * Q_TILE]


def _kv_rot(kvp, seq, pos_col, inv_full, sgn_full):
    n = kvp.shape[0]
    T = KV_ROW_TILE
    pair_w = 2 * NSA_KV_W
    row = lambda w: pl.BlockSpec((T, w), lambda i: (i, 0))
    vt_shape = (NSA_KV_HEADS, n // Q_TILE, HEAD_DIM, Q_TILE)
    vt_spec = pl.BlockSpec((NSA_KV_HEADS, T // Q_TILE, HEAD_DIM, Q_TILE), lambda i: (0, i, 0, 0))
    return pl.pallas_call(
        functools.partial(_kv_rot_kernel, seq),
        grid=(n // T,),
        in_specs=[
            pl.BlockSpec((T, pair_w), lambda i: (i, 1)),
            pl.BlockSpec((T, pair_w), lambda i: (i, 2)),
            row(1),
            pl.BlockSpec((1, LANES), lambda i: (0, 0)),
            pl.BlockSpec((1, LANES), lambda i: (0, 0)),
        ],
        out_specs=[row(2 * NSA_KV_W), vt_spec, row(NSA_KV_W), vt_spec, row(LANES), row(LANES)],
        out_shape=[
            jax.ShapeDtypeStruct((n, 2 * NSA_KV_W), BF16),
            jax.ShapeDtypeStruct(vt_shape, BF16),
            jax.ShapeDtypeStruct((n, NSA_KV_W), BF16),
            jax.ShapeDtypeStruct(vt_shape, BF16),
            jax.ShapeDtypeStruct((n, LANES), F32),
            jax.ShapeDtypeStruct((n, LANES), F32),
        ],
        compiler_params=pltpu.CompilerParams(dimension_semantics=("arbitrary",), vmem_limit_bytes=VMEM_LIMIT),
        name="kv_rot",
    )(kvp, kvp, pos_col, inv_full, sgn_full)


def _compress_kernel(kc_ref, vc_ref, cos_ref, sin_ref, pe_ref, w1_ref, w2_ref, kcmp_ref, vcmp_ref):
    S = kc_ref.shape[0]
    n_half = S // CMP_STRIDE
    assert CMP_LEN == 2 * CMP_STRIDE
    for part, (src_ref, dst_ref) in enumerate(((kc_ref, kcmp_ref), (vc_ref, vcmp_ref))):
        acc_first = jnp.zeros((n_half, CMP_HID), F32)
        acc_second = jnp.zeros((n_half, CMP_HID), F32)
        for l in range(CMP_STRIDE):
            z = src_ref[pl.ds(l, n_half, stride=CMP_STRIDE), :]
            z1 = (z + pe_ref[part, l:l + 1, :]).astype(BF16)
            acc_first = acc_first + _dot(z1, w1_ref[part, l * HEAD_DIM:(l + 1) * HEAD_DIM, :])
            l2 = l + CMP_STRIDE
            z2 = (z + pe_ref[part, l2:l2 + 1, :]).astype(BF16)
            acc_second = acc_second + _dot(z2, w1_ref[part, l2 * HEAD_DIM:(l2 + 1) * HEAD_DIM, :])
        pre = acc_first + pltpu.roll(acc_second, n_half - 1, axis=0)
        out = _dot(_silu(pre).astype(BF16), w2_ref[part])
        if part == 0:
            c_end = pltpu.roll(cos_ref[pl.ds(CMP_STRIDE - 1, n_half, stride=CMP_STRIDE), :], n_half - 1, axis=0)
            s_end = pltpu.roll(sin_ref[pl.ds(CMP_STRIDE - 1, n_half, stride=CMP_STRIDE), :], n_half - 1, axis=0)
            out = _rope(out, c_end, s_end)
            dst_ref[...] = out.astype(BF16)
        else:
            dst_ref[...] = out.T.astype(BF16)


def _compress(kvp, batch, seq, cos_tab, sin_tab, cmp_pe, cmp_w1, cmp_w2):
    n_half = seq // CMP_STRIDE
    cmp_out = pl.BlockSpec((None, None, n_half, HEAD_DIM), lambda b, g: (b, g, 0, 0))
    tab = pl.BlockSpec((seq, LANES), lambda b, g: (b, 0))
    return pl.pallas_call(
        _compress_kernel,
        grid=(batch, NSA_KV_HEADS),
        in_specs=[
            pl.BlockSpec((seq, HEAD_DIM), lambda b, g: (b, g)),
            pl.BlockSpec((seq, HEAD_DIM), lambda b, g: (b, NSA_KV_HEADS + g)),
            tab, tab,
            _resident(cmp_pe.shape), _resident(cmp_w1.shape), _resident(cmp_w2.shape),
        ],
        out_specs=[cmp_out, pl.BlockSpec((None, None, HEAD_DIM, n_half), lambda b, g: (b, g, 0, 0))],
        out_shape=[
            jax.ShapeDtypeStruct((batch, NSA_KV_HEADS, n_half, HEAD_DIM), BF16),
            jax.ShapeDtypeStruct((batch, NSA_KV_HEADS, HEAD_DIM, n_half), BF16),
        ],
        compiler_params=pltpu.CompilerParams(dimension_semantics=("arbitrary", "arbitrary"),
                                             vmem_limit_bytes=VMEM_LIMIT),
        name="compress",
    )(kvp, kvp, cos_tab, sin_tab, cmp_pe, cmp_w1, cmp_w2)


def _nsa_kernel(q_ref, zq_ref, gl_ref, cos_ref, sin_ref, ksx_ref, vst_ref, kw_ref, vwt_ref, kcmp_ref, vcmpt_ref,
                o_ref, qxt_ref, sc_ref, m_ref, l_ref, acc_ref, ocmp_ref, osel_ref, s_ref, p_ref, alpha_ref,
                rank_ref, owin_ref):
    TQ = q_ref.shape[0]
    R = NSA_GROUP
    M = R * TQ
    n_cmp_pad = kcmp_ref.shape[0]
    n_sel = ksx_ref.shape[0] // SEL_LEN
    g = pl.program_id(1)
    qi = pl.program_id(2)
    t0 = qi * TQ

    cos_t = cos_ref[...]
    sin_t = sin_ref[...]
    for r in range(R):
        qr = _rope(q_ref[:, r * HEAD_DIM:(r + 1) * HEAD_DIM], cos_t, sin_t) * (QK_SCALE * LOG2E)
        qxt_ref[0:HEAD_DIM, r * TQ:(r + 1) * TQ] = qr.T.astype(BF16)
    qxt_ref[HEAD_DIM + n_sel:2 * HEAD_DIM, :] = jnp.zeros((HEAD_DIM - n_sel, M), BF16)

    lane_q = lax.broadcasted_iota(jnp.int32, (1, M), 1) % TQ
    q_t = qxt_ref[0:HEAD_DIM, :]
    slabs = [slice(r * TQ, (r + 1) * TQ) for r in range(R)]
    key_sub = lax.broadcasted_iota(jnp.int32, (TQ, TQ), 0)
    q_lane = lax.broadcasted_iota(jnp.int32, (TQ, TQ), 1)
    causal = key_sub <= q_lane

    assert WINDOW == 2 * TQ
    w_tiles = (jnp.maximum(qi - 2, 0), jnp.maximum(qi - 1, 0), qi)
    w_masks = ((key_sub > q_lane) & (qi >= 2), (key_sub >= 0) & (qi >= 1), causal)
    w_keys = [kw_ref[pl.ds(pl.multiple_of(kt * TQ, TQ), TQ), :] for kt in w_tiles]
    w_scores = [[_dot(k, qxt_ref[0:HEAD_DIM, rs]) for k in w_keys] for rs in slabs]

    s_t = _dot(kcmp_ref[...], q_t)
    cmp_end = lax.broadcasted_iota(jnp.int32, (n_cmp_pad, 1), 0) * CMP_STRIDE + (CMP_LEN - 1)
    valid = cmp_end <= t0 + lane_q
    s_t = jnp.where(valid, s_t, NEG_INF)
    e = jnp.where(valid, jnp.exp2(s_t - jnp.max(s_t, axis=0, keepdims=True)), 0.0)
    p_cmp = e * (1.0 / jnp.maximum(jnp.sum(e, axis=0, keepdims=True), 1e-30))
    ocmp_ref[...] = _dot(vcmpt_ref[...], p_cmp.astype(BF16))

    w_probs = []
    w_inv_l = []
    for s_head in w_scores:
        s_head = [jnp.where(mk, s, NEG_INF) for s, mk in zip(s_head, w_masks)]
        m_col = jnp.max(s_head[0], axis=0, keepdims=True)
        for s in s_head[1:]:
            m_col = jnp.maximum(m_col, jnp.max(s, axis=0, keepdims=True))
        p_head = [jnp.exp2(s - m_col) for s in s_head]
        l_col = jnp.sum(p_head[0], axis=0, keepdims=True)
        for p in p_head[1:]:
            l_col = l_col + jnp.sum(p, axis=0, keepdims=True)
        w_inv_l.append(1.0 / l_col)
        w_probs.append([p.astype(BF16) for p in p_head])

    p_sum = p_cmp[:, 0:TQ]
    for r in range(1, R):
        p_sum = p_sum + p_cmp[:, r * TQ:(r + 1) * TQ]
    jn = lax.broadcasted_iota(jnp.int32, (n_sel, n_cmp_pad), 0) * SEL_LEN
    cn = lax.broadcasted_iota(jnp.int32, (n_sel, n_cmp_pad), 1) * CMP_STRIDE
    ov = jnp.minimum(cn + (CMP_LEN - 1), jn + (SEL_LEN - 1)) - jnp.maximum(cn, jn) + 1
    ov_t = (jnp.maximum(ov, 0).astype(F32) / CMP_LEN).astype(BF16)
    p_hi = p_sum.astype(BF16)
    p_lo = (p_sum - p_hi.astype(F32)).astype(BF16)
    imp = _dot(ov_t, p_hi) + _dot(ov_t, p_lo)

    for rs, p_head, inv_l in zip(slabs, w_probs, w_inv_l):
        o = _dot(vwt_ref[w_tiles[0]], p_head[0])
        for kt, p in zip(w_tiles[1:], p_head[1:]):
            o = o + _dot(vwt_ref[kt], p)
        owin_ref[:, rs] = o * inv_l

    jb = lax.broadcasted_iota(jnp.int32, (n_sel, TQ), 0)
    tq = t0 + lax.broadcasted_iota(jnp.int32, (n_sel, TQ), 1)
    cur = tq // SEL_LEN
    forced = (jb == 0) | (jb == cur) | (jb == cur - 1)
    ok = jb * SEL_LEN <= tq
    score = jnp.where(ok, jnp.where(forced, FORCE_SCORE, imp), NEG_INF)
    sc_ref[...] = score

    rank_ref[...] = jnp.zeros((n_sel, TQ), F32)
    blocks_per_tile = TQ // SEL_LEN
    for kt in range(n_sel // blocks_per_tile):
        @pl.when(kt <= qi)
        def _(kt=kt):
            others = [sc_ref[j2:j2 + 1, :] for j2 in range(kt * blocks_per_tile, (kt + 1) * blocks_per_tile)]
            for gi in range(n_sel // SUBLANES):
                rows = slice(gi * SUBLANES, (gi + 1) * SUBLANES)
                sg = sc_ref[rows, :]
                cnt = rank_ref[rows, :]
                for u, other in enumerate(others):
                    j2 = kt * blocks_per_tile + u
                    if gi * SUBLANES > j2:
                        before = other >= sg
                    elif (gi + 1) * SUBLANES - 1 < j2:
                        before = other > sg
                    else:
                        row_j = gi * SUBLANES + lax.broadcasted_iota(jnp.int32, (SUBLANES, TQ), 0)
                        before = (other > sg) | ((other == sg) & (row_j > j2))
                    cnt = cnt + jnp.where(before, 1.0, 0.0)
                rank_ref[rows, :] = cnt
    n_top = min(SEL_TOPK, n_sel)
    bias_t = jnp.where(rank_ref[...] < n_top, 0.0, SEL_BIAS).astype(BF16)
    for r in range(R):
        qxt_ref[HEAD_DIM:HEAD_DIM + n_sel, r * TQ:(r + 1) * TQ] = bias_t

    diag = pl.ds(pl.multiple_of(t0, TQ), TQ)

    def sel_scores(k_tile):
        return [_dot(k_tile, qxt_ref[:, rs]) for rs in slabs]

    def sel_pv(v_t):
        return [_dot(v_t, p_ref[:, rs]) for rs in slabs]

    n_full = qi
    s_diag = sel_scores(ksx_ref[diag, :])
    s_first = sel_scores(ksx_ref[0:TQ, :])
    for rs, s_t in zip(slabs, s_diag):
        s_t = jnp.where(causal, s_t, NEG_INF)
        m_tile = jnp.max(s_t, axis=0, keepdims=True)
        p = jnp.exp2(s_t - m_tile)
        m_ref[:, rs] = m_tile
        l_ref[:, rs] = jnp.sum(p, axis=0, keepdims=True)
        p_ref[:, rs] = p.astype(BF16)
    alpha_ref[...] = jnp.ones((1, M), F32)
    acc_ref[...] = jnp.zeros((HEAD_DIM, M), F32)
    for rs, s_t in zip(slabs, s_first):
        s_ref[:, rs] = s_t

    def sel_step(kt):
        alpha_prev = alpha_ref[...]
        nxt = jnp.minimum(kt + 1, n_full - 1)
        s_next = sel_scores(ksx_ref[pl.ds(pl.multiple_of(nxt * TQ, TQ), TQ), :])
        prev = jnp.where(kt == 0, qi, kt - 1)
        pv_prev = sel_pv(vst_ref[prev])
        for rs in slabs:
            s_t = s_ref[:, rs]
            m_prev = m_ref[:, rs]
            m_new = jnp.maximum(m_prev, jnp.max(s_t, axis=0, keepdims=True))
            alpha = jnp.exp2(m_prev - m_new)
            p = jnp.exp2(s_t - m_new)
            m_ref[:, rs] = m_new
            l_ref[:, rs] = alpha * l_ref[:, rs] + jnp.sum(p, axis=0, keepdims=True)
            alpha_ref[:, rs] = alpha
            p_ref[:, rs] = p.astype(BF16)
        for rs, o in zip(slabs, pv_prev):
            acc_ref[:, rs] = alpha_prev[:, rs] * acc_ref[:, rs] + o
        for rs, s_t in zip(slabs, s_next):
            s_ref[:, rs] = s_t

    def sel_pair(j, carry):
        sel_step(2 * j)
        sel_step(2 * j + 1)
        return carry
    lax.fori_loop(0, lax.shift_right_logical(n_full, 1), sel_pair, 0)

    @pl.when(n_full % 2 == 1)
    def _():
        sel_step(n_full - 1)

    last = jnp.where(n_full == 0, qi, n_full - 1)
    inv_l = 1.0 / l_ref[...]
    for rs, o in zip(slabs, sel_pv(vst_ref[last])):
        osel_ref[:, rs] = (alpha_ref[:, rs] * acc_ref[:, rs] + o) * inv_l[:, rs]

    o_win = owin_ref[...]

    gates = 1.0 / (1.0 + jnp.exp(-gl_ref[...]))
    gates_t = pltpu.roll(gates, (LANES - 3 * R * g) % LANES, axis=1).T
    for r in range(R):
        rs = slice(r * TQ, (r + 1) * TQ)
        mix_t = (gates_t[3 * r:3 * r + 1, :] * ocmp_ref[:, rs] + gates_t[3 * r + 1:3 * r + 2, :] * osel_ref[:, rs]
                 + gates_t[3 * r + 2:3 * r + 3, :] * o_win[:, rs])
        cs = slice(r * HEAD_DIM, (r + 1) * HEAD_DIM)
        o_ref[:, cs] = (mix_t.T * _silu(zq_ref[:, cs])).astype(o_ref.dtype)


def _nsa(p1, batch, seq, cos_tab, sin_tab, ksx, vs_t, kw, vw_t, kcmp, vcmp_t):
    n = p1.shape[0]
    TQ = Q_TILE
    R = NSA_GROUP
    q_tiles = seq // TQ
    gw = R * HEAD_DIM
    n_cmp_pad = kcmp.shape[2]
    n_sel = seq // SEL_LEN
    tok = lambda b, g, i: b * q_tiles + i
    head_kv = pl.BlockSpec((seq, HEAD_DIM), lambda b, g, i: (b, g))
    head_vt = pl.BlockSpec((None, q_tiles, HEAD_DIM, TQ), lambda b, g, i: (g, b, 0, 0))
    cmp_kv = pl.BlockSpec((None, None, n_cmp_pad, HEAD_DIM), lambda b, g, i: (b, g, 0, 0))
    cmp_vt = pl.BlockSpec((None, None, HEAD_DIM, n_cmp_pad), lambda b, g, i: (b, g, 0, 0))
    return pl.pallas_call(
        _nsa_kernel,
        grid=(batch, NSA_KV_HEADS, q_tiles),
        in_specs=[
            pl.BlockSpec((TQ, gw), lambda b, g, i: (tok(b, g, i), g)),
            pl.BlockSpec((TQ, gw), lambda b, g, i: (tok(b, g, i), NSA_W // gw + g)),
            pl.BlockSpec((TQ, GATE_PAD), lambda b, g, i: (tok(b, g, i), (2 * NSA_W + 2 * MEM_W) // GATE_PAD)),
            pl.BlockSpec((TQ, LANES), lambda b, g, i: (tok(b, g, i), 0)),
            pl.BlockSpec((TQ, LANES), lambda b, g, i: (tok(b, g, i), 0)),
            pl.BlockSpec((seq, 2 * HEAD_DIM), lambda b, g, i: (b, g)),
            head_vt, head_kv, head_vt, cmp_kv, cmp_vt,
        ],
        out_specs=pl.BlockSpec((TQ, gw), lambda b, g, i: (tok(b, g, i), g)),
        out_shape=jax.ShapeDtypeStruct((n, NSA_W), BF16),
        scratch_shapes=[
            pltpu.VMEM((2 * HEAD_DIM, R * TQ), BF16),
            pltpu.VMEM((n_sel, TQ), F32),
            pltpu.VMEM((1, R * TQ), F32),
            pltpu.VMEM((1, R * TQ), F32),
            pltpu.VMEM((HEAD_DIM, R * TQ), F32),
            pltpu.VMEM((HEAD_DIM, R * TQ), F32),
            pltpu.VMEM((HEAD_DIM, R * TQ), F32),
            pltpu.VMEM((TQ, R * TQ), F32),
            pltpu.VMEM((TQ, R * TQ), BF16),
            pltpu.VMEM((1, R * TQ), F32),
            pltpu.VMEM((n_sel, TQ), F32),
            pltpu.VMEM((HEAD_DIM, R * TQ), F32),
        ],
        compiler_params=pltpu.CompilerParams(dimension_semantics=("arbitrary", "arbitrary", "arbitrary"),
                                             vmem_limit_bytes=VMEM_LIMIT),
        name="nsa",
    )(p1, p1, p1, cos_tab, sin_tab, ksx, vs_t, kw, vw_t, kcmp, vcmp_t)


def _tail_kernel(y_ref, qz_ref, mk_ref, mv_ref, h_ref, wout_ref, g_ref, o_ref, ymem_ref, acc_ref):
    mem_scores = _memory_scores(qz_ref[:, 0:MEM_W], mk_ref)
    for c in range(0, D_MODEL, COL_CHUNK):
        cs = slice(c, c + COL_CHUNK)
        acc_ref[:, cs] = h_ref[:, cs] + _dot(y_ref[...], wout_ref[0:NSA_W, cs])
    _memory_finish(mem_scores, qz_ref[:, MEM_W:2 * MEM_W], mv_ref, ymem_ref, 0)
    for c in range(0, D_MODEL, COL_CHUNK):
        cs = slice(c, c + COL_CHUNK)
        acc_ref[:, cs] = acc_ref[:, cs] + _dot(ymem_ref[...], wout_ref[NSA_W:D_MODEL, cs])
    o_ref[...] = _rms_scale(acc_ref[...], g_ref[...])


def _tail(y_nsa, p1, seq, mem_k, mem_v, h2d, w_out, final_g):
    n = h2d.shape[0]
    T = TOK_TILE
    tiles_per_seq = seq // T
    return pl.pallas_call(
        _tail_kernel,
        grid=(n // T,),
        in_specs=[
            pl.BlockSpec((T, NSA_W), lambda i: (i, 0)),
            pl.BlockSpec((T, 2 * MEM_W), lambda i: (i, 2 * NSA_W // (2 * MEM_W))),
            pl.BlockSpec((None, MEM_LEN, MEM_W), lambda i: (i // tiles_per_seq, 0, 0)),
            pl.BlockSpec((None, MEM_LEN, MEM_W), lambda i: (i // tiles_per_seq, 0, 0)),
            pl.BlockSpec((T, D_MODEL), lambda i: (i, 0)),
            _resident((D_MODEL, D_MODEL)),
            _resident((1, D_MODEL)),
        ],
        out_specs=pl.BlockSpec((T, D_MODEL), lambda i: (i, 0)),
        out_shape=jax.ShapeDtypeStruct((n, D_MODEL), F32),
        scratch_shapes=[pltpu.VMEM((T, MEM_W), BF16), pltpu.VMEM((T, D_MODEL), F32)],
        compiler_params=pltpu.CompilerParams(dimension_semantics=("arbitrary",), vmem_limit_bytes=VMEM_LIMIT),
        name="tail",
    )(y_nsa, p1, mem_k, mem_v, h2d, w_out, final_g.reshape(1, D_MODEL))


def kernel(x, mem, positions, norm_g, mem_norm_g, w_mem_kv, w_out, a_w_in, a_w_pool, a_pool_scale, b_w_in,
           kv_norm_g, w_kv, cmp_pe, cmp_w1, cmp_w2, final_g):
    batch, seq, _ = x.shape
    assert seq % Q_TILE == 0 and seq % TOK_TILE == 0 and Q_TILE % SEL_LEN == 0
    n = batch * seq
    x2d = x.reshape(n, D_MODEL)

    mkv = _mem_kv(mem.reshape(batch * MEM_LEN, D_MODEL), mem_norm_g, w_mem_kv.astype(BF16))
    mkv = mkv.reshape(mkv.shape[0], batch, MEM_LEN, 2 * MEM_W)
    mem_k, mem_v = mkv[..., :MEM_W], mkv[..., MEM_W:]

    h1 = _layer0(x2d, seq, norm_g[0], a_w_in[0].astype(BF16), a_w_pool[0].astype(BF16), a_pool_scale[0],
                 mem_k[0], mem_v[0], w_out[0].astype(BF16))

    w_q, w_rest = _split_w1(b_w_in)
    kvp, p1 = _proj1(h1, kv_norm_g, norm_g[1], w_kv.astype(BF16), w_q, w_rest)

    half = jnp.arange(ROT_HALF, dtype=F32)
    inv = ROPE_THETA ** (-half * 2.0 / ROT_DIM)
    pad = jnp.zeros((LANES - ROT_DIM,), F32)
    inv_full = jnp.concatenate([inv, inv, pad]).reshape(1, LANES)
    sgn_full = jnp.concatenate([-jnp.ones((ROT_HALF,), F32), jnp.ones((ROT_HALF,), F32), pad]).reshape(1, LANES)
    ksx, vs, kw, vw, cos_tab, sin_tab = _kv_rot(kvp, seq, positions.reshape(n, 1), inv_full, sgn_full)
    kcmp, vcmp = _compress(kvp, batch, seq, cos_tab, sin_tab, cmp_pe, cmp_w1.astype(BF16), cmp_w2.astype(BF16))

    y_nsa = _nsa(p1, batch, seq, cos_tab, sin_tab, ksx, vs, kw, vw, kcmp, vcmp)
    out = _tail(y_nsa, p1, seq, mem_k[1], mem_v[1], h1, w_out[1].astype(BF16), final_g)
    return out.reshape(batch, seq, D_MODEL)
```

```python
import functools

import jax
import jax.numpy as jnp
from jax import lax
from jax.experimental import pallas as pl
from jax.experimental.pallas import tpu as pltpu

D_MODEL = 2048
MEM_LEN = 256
HEAD_DIM = 128
MEM_HEADS = 4
MEM_W = MEM_HEADS * HEAD_DIM
POOL_W = D_MODEL - MEM_W
POOL_WINDOWS = (2, 4, 8, 16)
POOL_GC = POOL_W // len(POOL_WINDOWS)
POOL_HALO = 16
NSA_W = D_MODEL - MEM_W
NSA_HEADS = NSA_W // HEAD_DIM
NSA_KV_HEADS = 4
NSA_GROUP = NSA_HEADS // NSA_KV_HEADS
NSA_KV_W = NSA_KV_HEADS * HEAD_DIM
CMP_LEN = 32
CMP_STRIDE = 16
CMP_HID = 256
SEL_LEN = 64
SEL_TOPK = 16
WINDOW = 512
ROT_DIM = HEAD_DIM // 4
ROT_HALF = ROT_DIM // 2
ROPE_THETA = 500000.0
NORM_EPS = 1e-6
FORCE_SCORE = 1e4
NEG_INF = -1e30
SEL_BIAS = -1e9
QK_SCALE = HEAD_DIM ** -0.5
LOG2E = 1.4426950408889634

LANES = 128
SUBLANES = 8
TOK_TILE = 256
Q_TILE = 256
COL_CHUNK = 512
GATE_PAD = LANES
N_GATE = 3 * NSA_HEADS
VMEM_LIMIT = 56 * 1024 * 1024

F32 = jnp.float32
BF16 = jnp.bfloat16


def _resident(shape):
    nd = len(shape)
    return pl.BlockSpec(shape, lambda *_: (0,) * nd, pipeline_mode=pl.Buffered(1))


def _rms_scale(x, g):
    ms = jnp.mean(x * x, axis=-1, keepdims=True)
    return x * lax.rsqrt(ms + NORM_EPS) * g


def _silu(z):
    return z * (1.0 / (1.0 + jnp.exp(-z)))


def _dot(a, b):
    return jnp.dot(a, b, preferred_element_type=F32)


def _dot_nt(a, b):
    return lax.dot_general(a, b, (((1,), (1,)), ((), ())), preferred_element_type=F32)


def _memory_scores(qm, k_ref):
    scores = []
    for h in range(MEM_HEADS):
        cs = slice(h * HEAD_DIM, (h + 1) * HEAD_DIM)
        scores.append(_dot_nt((qm[:, cs] * QK_SCALE).astype(BF16), k_ref[:, cs]))
    return scores


def _memory_finish(scores, zm, v_ref, y_ref, col0):
    for h, s in enumerate(scores):
        cs = slice(h * HEAD_DIM, (h + 1) * HEAD_DIM)
        e = jnp.exp(s - jnp.max(s, axis=-1, keepdims=True))
        p = e / jnp.sum(e, axis=-1, keepdims=True)
        o = _dot(p.astype(BF16), v_ref[:, cs])
        y_ref[:, col0 + h * HEAD_DIM:col0 + (h + 1) * HEAD_DIM] = (o * _silu(zm[:, cs])).astype(y_ref.dtype)


def _mem_kv_kernel(mem_ref, g_ref, w_ref, o_ref):
    hn = _rms_scale(mem_ref[...], g_ref[...]).astype(BF16)
    o_ref[...] = _dot(hn, w_ref[...]).astype(o_ref.dtype)


def _mem_kv(mem2d, mem_norm_g, w_mem_kv_bf16):
    depth = w_mem_kv_bf16.shape[0]
    rows = mem2d.shape[0]
    T = TOK_TILE
    return pl.pallas_call(
        _mem_kv_kernel,
        grid=(depth, rows // T),
        in_specs=[
            pl.BlockSpec((T, D_MODEL), lambda l, i: (i, 0)),
            pl.BlockSpec((None, 1, D_MODEL), lambda l, i: (l, 0, 0)),
            pl.BlockSpec((None, D_MODEL, 2 * MEM_W), lambda l, i: (l, 0, 0)),
        ],
        out_specs=pl.BlockSpec((None, T, 2 * MEM_W), lambda l, i: (l, i, 0)),
        out_shape=jax.ShapeDtypeStruct((depth, rows, 2 * MEM_W), BF16),
        compiler_params=pltpu.CompilerParams(dimension_semantics=("arbitrary", "arbitrary"),
                                             vmem_limit_bytes=VMEM_LIMIT),
        name="mem_kv",
    )(mem2d, mem_norm_g.reshape(depth, 1, D_MODEL), w_mem_kv_bf16)


def _layer0_kernel(tiles_per_seq, x_ref, g_ref, win_ref, wpool_ref, scale_ref, mk_ref, mv_ref, wout_ref,
                   o_ref, hn_ref, proj_ref, uext_ref, y_ref):
    T = x_ref.shape[0]
    tb = pl.program_id(0) % tiles_per_seq
    x = x_ref[...]
    hn_ref[...] = _rms_scale(x, g_ref[...]).astype(BF16)

    @pl.when(tb == 0)
    def _():
        uext_ref[0:POOL_HALO, :] = jnp.zeros((POOL_HALO, POOL_W), F32)

    @pl.when(tb != 0)
    def _():
        uext_ref[0:POOL_HALO, :] = uext_ref[T:T + POOL_HALO, :]

    n_in = win_ref.shape[1]
    for c in range(0, n_in, COL_CHUNK):
        blk = _dot(hn_ref[...], win_ref[:, c:c + COL_CHUNK])
        if c < POOL_W:
            uext_ref[POOL_HALO:POOL_HALO + T, c:c + COL_CHUNK] = blk
        else:
            proj_ref[:, c - POOL_W:c - POOL_W + COL_CHUNK] = blk

    t1 = (tb * T + lax.broadcasted_iota(jnp.int32, (T, 1), 0) + 1).astype(F32)
    for gi, win in enumerate(POOL_WINDOWS):
        cs = slice(gi * POOL_GC, (gi + 1) * POOL_GC)
        u = uext_ref[POOL_HALO:POOL_HALO + T, cs]
        acc = u
        for k in range(1, win):
            acc = acc + uext_ref[POOL_HALO - k:POOL_HALO - k + T, cs]
        pooled = acc / jnp.minimum(t1, float(win)) - u
        mixed = _dot(pooled.astype(BF16), wpool_ref[gi])
        y_ref[:, cs] = (mixed * scale_ref[:, cs] * _silu(proj_ref[:, cs])).astype(BF16)

    mem_scores = _memory_scores(proj_ref[:, POOL_W:POOL_W + MEM_W], mk_ref)
    for c in range(0, D_MODEL, COL_CHUNK):
        cs = slice(c, c + COL_CHUNK)
        o_ref[:, cs] = x_ref[:, cs] + _dot(y_ref[:, 0:POOL_W], wout_ref[0:POOL_W, cs])

    zm = proj_ref[:, POOL_W + MEM_W:POOL_W + 2 * MEM_W]
    _memory_finish(mem_scores, zm, mv_ref, y_ref, POOL_W)

    for c in range(0, D_MODEL, COL_CHUNK):
        cs = slice(c, c + COL_CHUNK)
        o_ref[:, cs] = o_ref[:, cs] + _dot(y_ref[:, POOL_W:D_MODEL], wout_ref[POOL_W:D_MODEL, cs])


def _layer0(x2d, seq, norm_g, w_in, w_pool, pool_scale, mem_k, mem_v, w_out):
    n = x2d.shape[0]
    T = TOK_TILE
    tiles_per_seq = seq // T
    n_in = w_in.shape[1]
    return pl.pallas_call(
        functools.partial(_layer0_kernel, tiles_per_seq),
        grid=(n // T,),
        in_specs=[
            pl.BlockSpec((T, D_MODEL), lambda i: (i, 0)),
            _resident((1, D_MODEL)),
            _resident((D_MODEL, n_in)),
            _resident(w_pool.shape),
            _resident((1, POOL_W)),
            pl.BlockSpec((None, MEM_LEN, MEM_W), lambda i: (i // tiles_per_seq, 0, 0)),
            pl.BlockSpec((None, MEM_LEN, MEM_W), lambda i: (i // tiles_per_seq, 0, 0)),
            _resident((D_MODEL, D_MODEL)),
        ],
        out_specs=pl.BlockSpec((T, D_MODEL), lambda i: (i, 0)),
        out_shape=jax.ShapeDtypeStruct((n, D_MODEL), F32),
        scratch_shapes=[
            pltpu.VMEM((T, D_MODEL), BF16),
            pltpu.VMEM((T, n_in - POOL_W), F32),
            pltpu.VMEM((T + POOL_HALO, POOL_W), F32),
            pltpu.VMEM((T, D_MODEL), BF16),
        ],
        compiler_params=pltpu.CompilerParams(dimension_semantics=("arbitrary",), vmem_limit_bytes=VMEM_LIMIT),
        name="layer0",
    )(x2d, norm_g.reshape(1, D_MODEL), w_in, w_pool, pool_scale.reshape(1, POOL_W), mem_k, mem_v, w_out)


def _proj1_kernel(h_ref, gkv_ref, g1_ref, wkv_ref, wq_ref, wr_ref, kv_ref, p1_ref, hn_ref, rest_ref):
    T = h_ref.shape[0]
    h = h_ref[...]
    ms = jnp.mean(h * h, axis=-1, keepdims=True)
    hs = h * lax.rsqrt(ms + NORM_EPS)
    hn_ref[...] = (hs * gkv_ref[...]).astype(BF16)
    n_kv = wkv_ref.shape[1]
    for c in range(0, n_kv, COL_CHUNK):
        kv_ref[:, c:c + COL_CHUNK] = _dot(hn_ref[...], wkv_ref[:, c:c + COL_CHUNK])
    hn_ref[...] = (hs * g1_ref[...]).astype(BF16)
    n_q = wq_ref.shape[1]
    for c in range(0, n_q, COL_CHUNK):
        p1_ref[:, c:c + COL_CHUNK] = _dot(hn_ref[...], wq_ref[:, c:c + COL_CHUNK])
    n_r = wr_ref.shape[1]
    for c in range(0, n_r, COL_CHUNK):
        w = min(COL_CHUNK, n_r - c)
        rest_ref[:, c:c + w] = _dot(hn_ref[...], wr_ref[:, c:c + w])
    shift = LANES - N_GATE
    lane = lax.broadcasted_iota(jnp.int32, (T, LANES), 1)
    n_tiles = n_r // LANES - 1
    first = rest_ref[:, 0:LANES]
    p1_ref[:, n_q + n_tiles * LANES:n_q + (n_tiles + 1) * LANES] = jnp.where(lane < N_GATE, first, 0.0)
    prev = pltpu.roll(first, shift, axis=1)
    for t in range(n_tiles):
        nxt = pltpu.roll(rest_ref[:, (t + 1) * LANES:(t + 2) * LANES], shift, axis=1)
        p1_ref[:, n_q + t * LANES:n_q + (t + 1) * LANES] = jnp.where(lane < shift, prev, nxt)
        prev = nxt


def _split_w1_kernel(w_ref, wq_ref, wr_ref):
    n_in, n_q, n_r = w_ref.shape[1], wq_ref.shape[1], wr_ref.shape[1]
    wq_ref[...] = w_ref[:, 0:n_q].astype(BF16)
    full = (n_in - n_q) // LANES * LANES
    wr_ref[:, 0:full] = w_ref[:, n_q:n_q + full].astype(BF16)
    wr_ref[:, full:n_r] = jnp.zeros((w_ref.shape[0], n_r - full), BF16)
    wr_ref[:, full:n_in - n_q] = w_ref[:, n_q + full:n_in].astype(BF16)


def _split_w1(w_in):
    n_in = w_in.shape[2]
    n_r = -(-(n_in - NSA_W) // LANES) * LANES
    rows = TOK_TILE
    return pl.pallas_call(
        _split_w1_kernel,
        grid=(D_MODEL // rows,),
        in_specs=[pl.BlockSpec((None, rows, n_in), lambda i: (0, i, 0))],
        out_specs=[pl.BlockSpec((rows, NSA_W), lambda i: (i, 0)), pl.BlockSpec((rows, n_r), lambda i: (i, 0))],
        out_shape=[jax.ShapeDtypeStruct((D_MODEL, NSA_W), BF16), jax.ShapeDtypeStruct((D_MODEL, n_r), BF16)],
        compiler_params=pltpu.CompilerParams(dimension_semantics=("arbitrary",), vmem_limit_bytes=VMEM_LIMIT),
        name="split_w1",
    )(w_in)


def _proj1(h2d, kv_norm_g, norm_g1, w_kv, w_q, w_rest):
    n = h2d.shape[0]
    T = TOK_TILE
    n_kv, n_q, n_r = w_kv.shape[1], w_q.shape[1], w_rest.shape[1]
    assert n_r % LANES == 0 and n_q % COL_CHUNK == 0
    n_1 = n_q + n_r
    return pl.pallas_call(
        _proj1_kernel,
        grid=(n // T,),
        in_specs=[
            pl.BlockSpec((T, D_MODEL), lambda i: (i, 0)),
            _resident((1, D_MODEL)),
            _resident((1, D_MODEL)),
            _resident((D_MODEL, n_kv)),
            _resident((D_MODEL, n_q)),
            _resident((D_MODEL, n_r)),
        ],
        out_specs=[
            pl.BlockSpec((T, n_kv), lambda i: (i, 0)),
            pl.BlockSpec((T, n_1), lambda i: (i, 0)),
        ],
        out_shape=[
            jax.ShapeDtypeStruct((n, n_kv), F32),
            jax.ShapeDtypeStruct((n, n_1), F32),
        ],
        scratch_shapes=[pltpu.VMEM((T, D_MODEL), BF16), pltpu.VMEM((T, n_r), F32)],
        compiler_params=pltpu.CompilerParams(dimension_semantics=("arbitrary",), vmem_limit_bytes=VMEM_LIMIT),
        name="proj1",
    )(h2d, kv_norm_g.reshape(1, D_MODEL), norm_g1.reshape(1, D_MODEL), w_kv, w_q, w_rest)


def _rope(x, cos_t, sin_t):
    lane = lax.broadcasted_iota(jnp.int32, x.shape, 1)
    swapped = jnp.where(lane < ROT_HALF, pltpu.roll(x, LANES - ROT_HALF, axis=1), pltpu.roll(x, ROT_HALF, axis=1))
    return x * cos_t + swapped * sin_t


KV_ROW_TILE = 512


def _kv_rot_kernel(seq, ksv_ref, kwv_ref, pos_ref, inv_ref, sgn_ref, ksx_ref, vso_ref, kwo_ref, vwo_ref, cos_ref,
                   sin_ref):
    rows = ksv_ref.shape[0]
    groups = LANES // ROT_DIM
    gr = rows // groups
    pos = pos_ref[...].astype(F32)
    lane_g = lax.broadcasted_iota(jnp.int32, (gr, LANES), 1)
    packed = jnp.broadcast_to(pos[0:gr], (gr, LANES))
    inv_t = inv_ref[...]
    sgn_t = sgn_ref[...]
    for k in range(1, groups):
        packed = jnp.where(lane_g >= k * ROT_DIM, pos[k * gr:(k + 1) * gr], packed)
        inv_t = inv_t + pltpu.roll(inv_ref[...], k * ROT_DIM, axis=1)
        sgn_t = sgn_t + pltpu.roll(sgn_ref[...], k * ROT_DIM, axis=1)
    ang = packed * inv_t
    c_packed = jnp.cos(ang)
    s_packed = jnp.sin(ang) * sgn_t
    c_parts, s_parts = [], []
    for k in range(groups):
        back = (LANES - k * ROT_DIM) % LANES
        c_k = c_packed if k == 0 else pltpu.roll(c_packed, back, axis=1)
        s_k = s_packed if k == 0 else pltpu.roll(s_packed, back, axis=1)
        c_parts.append(jnp.where(lane_g < ROT_DIM, c_k, 1.0))
        s_parts.append(jnp.where(lane_g < ROT_DIM, s_k, 0.0))
    c = jnp.concatenate(c_parts, axis=0)
    s = jnp.concatenate(s_parts, axis=0)
    cos_ref[...] = c
    sin_ref[...] = s
    r0 = (pl.program_id(0) * rows) % seq
    blk = (r0 + lax.broadcasted_iota(jnp.int32, (rows, LANES), 0)) // SEL_LEN
    lane = lax.broadcasted_iota(jnp.int32, (rows, LANES), 1)
    ind = jnp.where(blk == lane, 1.0, 0.0).astype(BF16)
    for h in range(NSA_KV_HEADS):
        hs = slice(h * HEAD_DIM, (h + 1) * HEAD_DIM)
        vsl = slice(NSA_KV_W + h * HEAD_DIM, NSA_KV_W + (h + 1) * HEAD_DIM)
        ksx_ref[:, 2 * h * HEAD_DIM:(2 * h + 1) * HEAD_DIM] = _rope(ksv_ref[:, hs], c, s).astype(BF16)
        ksx_ref[:, (2 * h + 1) * HEAD_DIM:(2 * h + 2) * HEAD_DIM] = ind
        kwo_ref[:, hs] = _rope(kwv_ref[:, hs], c, s).astype(BF16)
        vs_t = ksv_ref[:, vsl].T.astype(BF16)
        vw_t = kwv_ref[:, vsl].T.astype(BF16)
        for j in range(rows // Q_TILE):
            vso_ref[h, j] = vs_t[:, j * Q_TILE:(j + 1) * Q_TILE]
            vwo_ref[h, j] = vw_t[:, j * Q_TILE:(j + 1) * Q_TILE]


def _kv_rot(kvp, seq, pos_col, inv_full, sgn_full):
    n = kvp.shape[0]
    T = KV_ROW_TILE
    pair_w = 2 * NSA_KV_W
    row = lambda w: pl.BlockSpec((T, w), lambda i: (i, 0))
    vt_shape = (NSA_KV_HEADS, n // Q_TILE, HEAD_DIM, Q_TILE)
    vt_spec = pl.BlockSpec((NSA_KV_HEADS, T // Q_TILE, HEAD_DIM, Q_TILE), lambda i: (0, i, 0, 0))
    return pl.pallas_call(
        functools.partial(_kv_rot_kernel, seq),
        grid=(n // T,),
        in_specs=[
            pl.BlockSpec((T, pair_w), lambda i: (i, 1)),
            pl.BlockSpec((T, pair_w), lambda i: (i, 2)),
            row(1),
            pl.BlockSpec((1, LANES), lambda i: (0, 0)),
            pl.BlockSpec((1, LANES), lambda i: (0, 0)),
        ],
        out_specs=[row(2 * NSA_KV_W), vt_spec, row(NSA_KV_W), vt_spec, row(LANES), row(LANES)],
        out_shape=[
            jax.ShapeDtypeStruct((n, 2 * NSA_KV_W), BF16),
            jax.ShapeDtypeStruct(vt_shape, BF16),
            jax.ShapeDtypeStruct((n, NSA_KV_W), BF16),
            jax.ShapeDtypeStruct(vt_shape, BF16),
            jax.ShapeDtypeStruct((n, LANES), F32),
            jax.ShapeDtypeStruct((n, LANES), F32),
        ],
        compiler_params=pltpu.CompilerParams(dimension_semantics=("arbitrary",), vmem_limit_bytes=VMEM_LIMIT),
        name="kv_rot",
    )(kvp, kvp, pos_col, inv_full, sgn_full)


def _compress_kernel(kc_ref, vc_ref, cos_ref, sin_ref, pe_ref, w1_ref, w2_ref, kcmp_ref, vcmp_ref):
    S = kc_ref.shape[0]
    n_half = S // CMP_STRIDE
    assert CMP_LEN == 2 * CMP_STRIDE
    for part, (src_ref, dst_ref) in enumerate(((kc_ref, kcmp_ref), (vc_ref, vcmp_ref))):
        acc_first = jnp.zeros((n_half, CMP_HID), F32)
        acc_second = jnp.zeros((n_half, CMP_HID), F32)
        for l in range(CMP_STRIDE):
            z = src_ref[pl.ds(l, n_half, stride=CMP_STRIDE), :]
            z1 = (z + pe_ref[part, l:l + 1, :]).astype(BF16)
            acc_first = acc_first + _dot(z1, w1_ref[part, l * HEAD_DIM:(l + 1) * HEAD_DIM, :])
            l2 = l + CMP_STRIDE
            z2 = (z + pe_ref[part, l2:l2 + 1, :]).astype(BF16)
            acc_second = acc_second + _dot(z2, w1_ref[part, l2 * HEAD_DIM:(l2 + 1) * HEAD_DIM, :])
        pre = acc_first + pltpu.roll(acc_second, n_half - 1, axis=0)
        out = _dot(_silu(pre).astype(BF16), w2_ref[part])
        if part == 0:
            c_end = pltpu.roll(cos_ref[pl.ds(CMP_STRIDE - 1, n_half, stride=CMP_STRIDE), :], n_half - 1, axis=0)
            s_end = pltpu.roll(sin_ref[pl.ds(CMP_STRIDE - 1, n_half, stride=CMP_STRIDE), :], n_half - 1, axis=0)
            out = _rope(out, c_end, s_end)
            dst_ref[...] = out.astype(BF16)
        else:
            dst_ref[...] = out.T.astype(BF16)


def _compress(kvp, batch, seq, cos_tab, sin_tab, cmp_pe, cmp_w1, cmp_w2):
    n_half = seq // CMP_STRIDE
    cmp_out = pl.BlockSpec((None, None, n_half, HEAD_DIM), lambda b, g: (b, g, 0, 0))
    tab = pl.BlockSpec((seq, LANES), lambda b, g: (b, 0))
    return pl.pallas_call(
        _compress_kernel,
        grid=(batch, NSA_KV_HEADS),
        in_specs=[
            pl.BlockSpec((seq, HEAD_DIM), lambda b, g: (b, g)),
            pl.BlockSpec((seq, HEAD_DIM), lambda b, g: (b, NSA_KV_HEADS + g)),
            tab, tab,
            _resident(cmp_pe.shape), _resident(cmp_w1.shape), _resident(cmp_w2.shape),
        ],
        out_specs=[cmp_out, pl.BlockSpec((None, None, HEAD_DIM, n_half), lambda b, g: (b, g, 0, 0))],
        out_shape=[
            jax.ShapeDtypeStruct((batch, NSA_KV_HEADS, n_half, HEAD_DIM), BF16),
            jax.ShapeDtypeStruct((batch, NSA_KV_HEADS, HEAD_DIM, n_half), BF16),
        ],
        compiler_params=pltpu.CompilerParams(dimension_semantics=("arbitrary", "arbitrary"),
                                             vmem_limit_bytes=VMEM_LIMIT),
        name="compress",
    )(kvp, kvp, cos_tab, sin_tab, cmp_pe, cmp_w1, cmp_w2)


def _nsa_kernel(q_ref, zq_ref, gl_ref, cos_ref, sin_ref, ksx_ref, vst_ref, kw_ref, vwt_ref, kcmp_ref, vcmpt_ref,
                o_ref, qxt_ref, sc_ref, acc_ref, ocmp_ref, osel_ref, s_ref, p_ref, rank_ref, owin_ref,
                m_ref, l_ref, alpha_ref):
    TQ = q_ref.shape[0]
    R = NSA_GROUP
    M = R * TQ
    n_cmp_pad = kcmp_ref.shape[0]
    n_sel = ksx_ref.shape[0] // SEL_LEN
    g = pl.program_id(1)
    qi = pl.program_id(2)
    t0 = qi * TQ

    cos_t = cos_ref[...]
    sin_t = sin_ref[...]
    for r in range(R):
        qr = _rope(q_ref[:, r * HEAD_DIM:(r + 1) * HEAD_DIM], cos_t, sin_t) * (QK_SCALE * LOG2E)
        qxt_ref[0:HEAD_DIM, r * TQ:(r + 1) * TQ] = qr.T.astype(BF16)
    qxt_ref[HEAD_DIM + n_sel:2 * HEAD_DIM, :] = jnp.zeros((HEAD_DIM - n_sel, M), BF16)

    lane_q = lax.broadcasted_iota(jnp.int32, (1, M), 1) % TQ
    q_t = qxt_ref[0:HEAD_DIM, :]
    slabs = [slice(r * TQ, (r + 1) * TQ) for r in range(R)]
    key_sub = lax.broadcasted_iota(jnp.int32, (TQ, TQ), 0)
    q_lane = lax.broadcasted_iota(jnp.int32, (TQ, TQ), 1)
    causal = key_sub <= q_lane

    assert WINDOW == 2 * TQ
    w_tiles = (jnp.maximum(qi - 2, 0), jnp.maximum(qi - 1, 0), qi)
    w_masks = ((key_sub > q_lane) & (qi >= 2), (key_sub >= 0) & (qi >= 1), causal)
    w_keys = [kw_ref[pl.ds(pl.multiple_of(kt * TQ, TQ), TQ), :] for kt in w_tiles]
    w_scores = [[_dot(k, qxt_ref[0:HEAD_DIM, rs]) for k in w_keys] for rs in slabs]

    s_t = _dot(kcmp_ref[...], q_t)
    cmp_end = lax.broadcasted_iota(jnp.int32, (n_cmp_pad, 1), 0) * CMP_STRIDE + (CMP_LEN - 1)
    valid = cmp_end <= t0 + lane_q
    s_t = jnp.where(valid, s_t, NEG_INF)
    e = jnp.where(valid, jnp.exp2(s_t - jnp.max(s_t, axis=0, keepdims=True)), 0.0)
    p_cmp = e * (1.0 / jnp.maximum(jnp.sum(e, axis=0, keepdims=True), 1e-30))
    ocmp_ref[...] = _dot(vcmpt_ref[...], p_cmp.astype(BF16))

    w_probs = []
    w_inv_l = []
    for s_head in w_scores:
        s_head = [jnp.where(mk, s, NEG_INF) for s, mk in zip(s_head, w_masks)]
        m_col = jnp.max(s_head[0], axis=0, keepdims=True)
        for s in s_head[1:]:
            m_col = jnp.maximum(m_col, jnp.max(s, axis=0, keepdims=True))
        p_head = [jnp.exp2(s - m_col) for s in s_head]
        l_col = jnp.sum(p_head[0], axis=0, keepdims=True)
        for p in p_head[1:]:
            l_col = l_col + jnp.sum(p, axis=0, keepdims=True)
        w_inv_l.append(1.0 / l_col)
        w_probs.append([p.astype(BF16) for p in p_head])

    p_sum = p_cmp[:, 0:TQ]
    for r in range(1, R):
        p_sum = p_sum + p_cmp[:, r * TQ:(r + 1) * TQ]
    jn = lax.broadcasted_iota(jnp.int32, (n_sel, n_cmp_pad), 0) * SEL_LEN
    cn = lax.broadcasted_iota(jnp.int32, (n_sel, n_cmp_pad), 1) * CMP_STRIDE
    ov = jnp.minimum(cn + (CMP_LEN - 1), jn + (SEL_LEN - 1)) - jnp.maximum(cn, jn) + 1
    ov_t = (jnp.maximum(ov, 0).astype(F32) / CMP_LEN).astype(BF16)
    p_hi = p_sum.astype(BF16)
    p_lo = (p_sum - p_hi.astype(F32)).astype(BF16)
    imp = _dot(ov_t, p_hi) + _dot(ov_t, p_lo)

    for rs, p_head, inv_l in zip(slabs, w_probs, w_inv_l):
        o = _dot(vwt_ref[w_tiles[0]], p_head[0])
        for kt, p in zip(w_tiles[1:], p_head[1:]):
            o = o + _dot(vwt_ref[kt], p)
        owin_ref[:, rs] = o * inv_l

    jb = lax.broadcasted_iota(jnp.int32, (n_sel, TQ), 0)
    tq = t0 + lax.broadcasted_iota(jnp.int32, (n_sel, TQ), 1)
    cur = tq // SEL_LEN
    forced = (jb == 0) | (jb == cur) | (jb == cur - 1)
    ok = jb * SEL_LEN <= tq
    score = jnp.where(ok, jnp.where(forced, FORCE_SCORE, imp), NEG_INF)
    sc_ref[...] = score

    rank_ref[...] = jnp.zeros((n_sel, TQ), F32)
    blocks_per_tile = TQ // SEL_LEN
    for kt in range(n_sel // blocks_per_tile):
        @pl.when(kt <= qi)
        def _(kt=kt):
            others = [sc_ref[j2:j2 + 1, :] for j2 in range(kt * blocks_per_tile, (kt + 1) * blocks_per_tile)]
            for gi in range(n_sel // SUBLANES):
                rows = slice(gi * SUBLANES, (gi + 1) * SUBLANES)
                sg = sc_ref[rows, :]
                cnt = rank_ref[rows, :]
                for u, other in enumerate(others):
                    j2 = kt * blocks_per_tile + u
                    if gi * SUBLANES > j2:
                        before = other >= sg
                    elif (gi + 1) * SUBLANES - 1 < j2:
                        before = other > sg
                    else:
                        row_j = gi * SUBLANES + lax.broadcasted_iota(jnp.int32, (SUBLANES, TQ), 0)
                        before = (other > sg) | ((other == sg) & (row_j > j2))
                    cnt = cnt + jnp.where(before, 1.0, 0.0)
                rank_ref[rows, :] = cnt
    n_top = min(SEL_TOPK, n_sel)
    bias_t = jnp.where(rank_ref[...] < n_top, 0.0, SEL_BIAS).astype(BF16)
    for r in range(R):
        qxt_ref[HEAD_DIM:HEAD_DIM + n_sel, r * TQ:(r + 1) * TQ] = bias_t

    diag = pl.ds(pl.multiple_of(t0, TQ), TQ)

    def sel_scores(k_tile):
        return [_dot(k_tile, qxt_ref[:, rs]) for rs in slabs]

    def sel_pv(v_t):
        return [_dot(v_t, p_ref[:, rs]) for rs in slabs]

    n_full = qi
    s_diag = sel_scores(ksx_ref[diag, :])
    s_first = sel_scores(ksx_ref[0:TQ, :])
    for rs, s_t in zip(slabs, s_diag):
        s_t = jnp.where(causal, s_t, NEG_INF)
        m_tile = jnp.max(s_t, axis=0, keepdims=True)
        p = jnp.exp2(s_t - m_tile)
        m_ref[:, rs] = m_tile
        l_ref[:, rs] = jnp.sum(p, axis=0, keepdims=True)
        p_ref[:, rs] = p.astype(BF16)
    alpha_ref[...] = jnp.ones((1, M), F32)
    acc_ref[...] = jnp.zeros((HEAD_DIM, M), F32)
    for rs, s_t in zip(slabs, s_first):
        s_ref[:, rs] = s_t

    def sel_step(kt):
        alpha_prev = alpha_ref[...]
        nxt = jnp.minimum(kt + 1, n_full - 1)
        s_next = sel_scores(ksx_ref[pl.ds(pl.multiple_of(nxt * TQ, TQ), TQ), :])
        prev = jnp.where(kt == 0, qi, kt - 1)
        pv_prev = sel_pv(vst_ref[prev])
        for rs in slabs:
            s_t = s_ref[:, rs]
            m_prev = m_ref[:, rs]
            m_new = jnp.maximum(m_prev, jnp.max(s_t, axis=0, keepdims=True))
            alpha = jnp.exp2(m_prev - m_new)
            p = jnp.exp2(s_t - m_new)
            m_ref[:, rs] = m_new
            l_ref[:, rs] = alpha * l_ref[:, rs] + jnp.sum(p, axis=0, keepdims=True)
            alpha_ref[:, rs] = alpha
            p_ref[:, rs] = p.astype(BF16)
        for rs, o in zip(slabs, pv_prev):
            acc_ref[:, rs] = alpha_prev[:, rs] * acc_ref[:, rs] + o
        for rs, s_t in zip(slabs, s_next):
            s_ref[:, rs] = s_t

    def sel_pair(j, carry):
        sel_step(2 * j)
        sel_step(2 * j + 1)
        return carry
    lax.fori_loop(0, lax.shift_right_logical(n_full, 1), sel_pair, 0)

    @pl.when(n_full % 2 == 1)
    def _():
        sel_step(n_full - 1)

    last = jnp.where(n_full == 0, qi, n_full - 1)
    inv_l = 1.0 / l_ref[...]
    for rs, o in zip(slabs, sel_pv(vst_ref[last])):
        osel_ref[:, rs] = (alpha_ref[:, rs] * acc_ref[:, rs] + o) * inv_l[:, rs]

    o_win = owin_ref[...]

    gates = 1.0 / (1.0 + jnp.exp(-gl_ref[...]))
    gates_t = pltpu.roll(gates, (LANES - 3 * R * g) % LANES, axis=1).T
    for r in range(R):
        rs = slice(r * TQ, (r + 1) * TQ)
        mix_t = (gates_t[3 * r:3 * r + 1, :] * ocmp_ref[:, rs] + gates_t[3 * r + 1:3 * r + 2, :] * osel_ref[:, rs]
                 + gates_t[3 * r + 2:3 * r + 3, :] * o_win[:, rs])
        cs = slice(r * HEAD_DIM, (r + 1) * HEAD_DIM)
        o_ref[:, cs] = (mix_t.T * _silu(zq_ref[:, cs])).astype(o_ref.dtype)


def _nsa(p1, batch, seq, cos_tab, sin_tab, ksx, vs_t, kw, vw_t, kcmp, vcmp_t):
    n = p1.shape[0]
    TQ = Q_TILE
    R = NSA_GROUP
    q_tiles = seq // TQ
    gw = R * HEAD_DIM
    n_cmp_pad = kcmp.shape[2]
    n_sel = seq // SEL_LEN
    tok = lambda b, g, i: b * q_tiles + i
    head_kv = pl.BlockSpec((seq, HEAD_DIM), lambda b, g, i: (b, g))
    head_vt = pl.BlockSpec((None, q_tiles, HEAD_DIM, TQ), lambda b, g, i: (g, b, 0, 0))
    cmp_kv = pl.BlockSpec((None, None, n_cmp_pad, HEAD_DIM), lambda b, g, i: (b, g, 0, 0))
    cmp_vt = pl.BlockSpec((None, None, HEAD_DIM, n_cmp_pad), lambda b, g, i: (b, g, 0, 0))
    return pl.pallas_call(
        _nsa_kernel,
        grid=(batch, NSA_KV_HEADS, q_tiles),
        in_specs=[
            pl.BlockSpec((TQ, gw), lambda b, g, i: (tok(b, g, i), g)),
            pl.BlockSpec((TQ, gw), lambda b, g, i: (tok(b, g, i), NSA_W // gw + g)),
            pl.BlockSpec((TQ, GATE_PAD), lambda b, g, i: (tok(b, g, i), (2 * NSA_W + 2 * MEM_W) // GATE_PAD)),
            pl.BlockSpec((TQ, LANES), lambda b, g, i: (tok(b, g, i), 0)),
            pl.BlockSpec((TQ, LANES), lambda b, g, i: (tok(b, g, i), 0)),
            pl.BlockSpec((seq, 2 * HEAD_DIM), lambda b, g, i: (b, g)),
            head_vt, head_kv, head_vt, cmp_kv, cmp_vt,
        ],
        out_specs=pl.BlockSpec((TQ, gw), lambda b, g, i: (tok(b, g, i), g)),
        out_shape=jax.ShapeDtypeStruct((n, NSA_W), BF16),
        scratch_shapes=[
            pltpu.VMEM((2 * HEAD_DIM, R * TQ), BF16),
            pltpu.VMEM((n_sel, TQ), F32),
            pltpu.VMEM((HEAD_DIM, R * TQ), F32),
            pltpu.VMEM((HEAD_DIM, R * TQ), F32),
            pltpu.VMEM((HEAD_DIM, R * TQ), F32),
            pltpu.VMEM((TQ, R * TQ), F32),
            pltpu.VMEM((TQ, R * TQ), BF16),
            pltpu.VMEM((n_sel, TQ), F32),
            pltpu.VMEM((HEAD_DIM, R * TQ), F32),
            pltpu.VMEM((1, R * TQ), F32),
            pltpu.VMEM((1, R * TQ), F32),
            pltpu.VMEM((1, R * TQ), F32),
        ],
        compiler_params=pltpu.CompilerParams(dimension_semantics=("arbitrary", "arbitrary", "arbitrary"),
                                             vmem_limit_bytes=VMEM_LIMIT),
        name="nsa",
    )(p1, p1, p1, cos_tab, sin_tab, ksx, vs_t, kw, vw_t, kcmp, vcmp_t)


def _tail_kernel(y_ref, qz_ref, mk_ref, mv_ref, h_ref, wout_ref, g_ref, o_ref, ymem_ref, acc_ref):
    mem_scores = _memory_scores(qz_ref[:, 0:MEM_W], mk_ref)
    for c in range(0, D_MODEL, COL_CHUNK):
        cs = slice(c, c + COL_CHUNK)
        acc_ref[:, cs] = h_ref[:, cs] + _dot(y_ref[...], wout_ref[0:NSA_W, cs])
    _memory_finish(mem_scores, qz_ref[:, MEM_W:2 * MEM_W], mv_ref, ymem_ref, 0)
    for c in range(0, D_MODEL, COL_CHUNK):
        cs = slice(c, c + COL_CHUNK)
        acc_ref[:, cs] = acc_ref[:, cs] + _dot(ymem_ref[...], wout_ref[NSA_W:D_MODEL, cs])
    o_ref[...] = _rms_scale(acc_ref[...], g_ref[...])


def _tail(y_nsa, p1, seq, mem_k, mem_v, h2d, w_out, final_g):
    n = h2d.shape[0]
    T = TOK_TILE
    tiles_per_seq = seq // T
    return pl.pallas_call(
        _tail_kernel,
        grid=(n // T,),
        in_specs=[
            pl.BlockSpec((T, NSA_W), lambda i: (i, 0)),
            pl.BlockSpec((T, 2 * MEM_W), lambda i: (i, 2 * NSA_W // (2 * MEM_W))),
            pl.BlockSpec((None, MEM_LEN, MEM_W), lambda i: (i // tiles_per_seq, 0, 0)),
            pl.BlockSpec((None, MEM_LEN, MEM_W), lambda i: (i // tiles_per_seq, 0, 0)),
            pl.BlockSpec((T, D_MODEL), lambda i: (i, 0)),
            _resident((D_MODEL, D_MODEL)),
            _resident((1, D_MODEL)),
        ],
        out_specs=pl.BlockSpec((T, D_MODEL), lambda i: (i, 0)),
        out_shape=jax.ShapeDtypeStruct((n, D_MODEL), F32),
        scratch_shapes=[pltpu.VMEM((T, MEM_W), BF16), pltpu.VMEM((T, D_MODEL), F32)],
        compiler_params=pltpu.CompilerParams(dimension_semantics=("arbitrary",), vmem_limit_bytes=VMEM_LIMIT),
        name="tail",
    )(y_nsa, p1, mem_k, mem_v, h2d, w_out, final_g.reshape(1, D_MODEL))


def kernel(x, mem, positions, norm_g, mem_norm_g, w_mem_kv, w_out, a_w_in, a_w_pool, a_pool_scale, b_w_in,
           kv_norm_g, w_kv, cmp_pe, cmp_w1, cmp_w2, final_g):
    batch, seq, _ = x.shape
    assert seq % Q_TILE == 0 and seq % TOK_TILE == 0 and Q_TILE % SEL_LEN == 0
    n = batch * seq
    x2d = x.reshape(n, D_MODEL)

    mkv = _mem_kv(mem.reshape(batch * MEM_LEN, D_MODEL), mem_norm_g, w_mem_kv.astype(BF16))
    mkv = mkv.reshape(mkv.shape[0], batch, MEM_LEN, 2 * MEM_W)
    mem_k, mem_v = mkv[..., :MEM_W], mkv[..., MEM_W:]

    h1 = _layer0(x2d, seq, norm_g[0], a_w_in[0].astype(BF16), a_w_pool[0].astype(BF16), a_pool_scale[0],
                 mem_k[0], mem_v[0], w_out[0].astype(BF16))

    w_q, w_rest = _split_w1(b_w_in)
    kvp, p1 = _proj1(h1, kv_norm_g, norm_g[1], w_kv.astype(BF16), w_q, w_rest)

    half = jnp.arange(ROT_HALF, dtype=F32)
    inv = ROPE_THETA ** (-half * 2.0 / ROT_DIM)
    pad = jnp.zeros((LANES - ROT_DIM,), F32)
    inv_full = jnp.concatenate([inv, inv, pad]).reshape(1, LANES)
    sgn_full = jnp.concatenate([-jnp.ones((ROT_HALF,), F32), jnp.ones((ROT_HALF,), F32), pad]).reshape(1, LANES)
    ksx, vs, kw, vw, cos_tab, sin_tab = _kv_rot(kvp, seq, positions.reshape(n, 1), inv_full, sgn_full)
    kcmp, vcmp = _compress(kvp, batch, seq, cos_tab, sin_tab, cmp_pe, cmp_w1.astype(BF16), cmp_w2.astype(BF16))

    y_nsa = _nsa(p1, batch, seq, cos_tab, sin_tab, ksx, vs, kw, vw, kcmp, vcmp)
    out = _tail(y_nsa, p1, seq, mem_k[1], mem_v[1], h1, w_out[1].astype(BF16), final_g)
    return out.reshape(batch, seq, D_MODEL)
```
